```python
import math
import jax
import jax.numpy as jnp
from jax import lax
import numpy as np

D_MODEL = 1024
BATCH = 2
SEQ = 8192
DEPTH = 2

GRID_W = 64
CTX_LEN = 256
EPS = 1e-6
ROPE_BASE = 10000.0
N_MOD = 9

A_HEADS = 4
A_QK = 32
A_V = 2 * A_QK
A_QCOLS = A_HEADS * 2 * A_QK
A_COLS = 2 * A_QCOLS + A_HEADS * A_V
A_SCALE = A_QK ** -0.5
Q_BLOCK = 128

B_HEADS = 4
B_KV_HEADS = 2
B_DIM = 64
WINDOW = 128
BLOCK = 128
B_QCOLS = B_HEADS * B_DIM
B_KCOLS = B_KV_HEADS * B_DIM
B_COLS = B_QCOLS + 2 * B_KCOLS
B_SCALE = B_DIM ** -0.5

C_HEADS = 8
C_HEAD_DIM = 64
C_INNER = C_HEADS * C_HEAD_DIM
C_GROUPS = 2
C_STATE = 64
C_CONV = 5
CHUNK = 128
C_XBC = C_INNER + 2 * C_GROUPS * C_STATE
C_COLS = C_INNER + C_XBC + 2 * C_HEADS

IN_COLS = A_COLS + B_COLS + C_COLS
MIX_WIDTH = A_HEADS * A_V + B_HEADS * B_DIM + C_INNER
D_FF = 2816

kernel_name = 'hybrid_dit_diffattn_swa_ssd_macaron'


def rms_norm(x, w):
    xf = x.astype(jnp.float32)
    y = xf * lax.rsqrt(jnp.mean(xf * xf, axis=-1, keepdims=True) + EPS)
    return (y * w.astype(jnp.float32)).astype(x.dtype)


def modulate(h, shift, scale):
    return h * (1 + scale) + shift


def adaln_chunks(mod):
    m = mod.reshape(-1, 1, N_MOD, D_MODEL)
    return [m[:, :, i] for i in range(N_MOD)]


def swiglu(h, w13, w2):
    g, u = jnp.split(h @ w13, 2, axis=-1)
    return (jax.nn.silu(g) * u) @ w2


def axial_rope_tables(seq_len, dim):
    rows = seq_len // GRID_W
    row = jnp.repeat(jnp.arange(rows, dtype=jnp.float32), GRID_W)
    col = jnp.tile(jnp.arange(GRID_W, dtype=jnp.float32), rows)
    quarter = dim // 4
    inv_freq = ROPE_BASE ** (-jnp.arange(quarter, dtype=jnp.float32) / quarter)
    ar = row[:, None] * inv_freq
    ac = col[:, None] * inv_freq
    ang = jnp.concatenate([ar, ar, ac, ac], axis=-1)
    return jnp.cos(ang), jnp.sin(ang)


def axial_rope(x, cos, sin):
    d = x.shape[-1]
    half, quarter = d // 2, d // 4

    def rot(u):
        return jnp.concatenate([-u[..., quarter:], u[..., :quarter]], axis=-1)

    xr = jnp.concatenate([rot(x[..., :half]), rot(x[..., half:])], axis=-1)
    shape = (1, x.shape[1]) + (1,) * (x.ndim - 3) + (d,)
    return x * cos.reshape(shape).astype(x.dtype) + xr * sin.reshape(shape).astype(x.dtype)


def diff_heads(p, qn, kn):
    b, t = p.shape[:2]
    q = rms_norm(p[..., :A_QCOLS].reshape(b, t, A_HEADS, 2, A_QK), qn)
    k = rms_norm(p[..., A_QCOLS:2 * A_QCOLS].reshape(b, t, A_HEADS, 2, A_QK), kn)
    v = p[..., 2 * A_QCOLS:].reshape(b, t, A_HEADS, A_V)
    return q, k, v


def diff_attend(q, k, v, lam, lam_init, subln):
    s = jnp.einsum('bqhcd,bkhcd->bhcqk', q, k).astype(jnp.float32) * A_SCALE
    p = jax.nn.softmax(s, axis=-1)
    a = p[:, :, 0] - lam * p[:, :, 1]
    o = jnp.einsum('bhqk,bkhd->bqhd', a.astype(v.dtype), v)
    return rms_norm(o, subln) * (1.0 - lam_init)


def diff_attend_latent(q, k_all, v_all, lam, lam_init, subln):
    b, s = q.shape[:2]
    nb = s // Q_BLOCK
    qb = jnp.moveaxis(q.reshape((b, nb, Q_BLOCK) + q.shape[2:]), 1, 0)
    ob = lax.map(lambda qi: diff_attend(qi, k_all, v_all, lam, lam_init, subln), qb)
    return jnp.moveaxis(ob, 0, 1).reshape(b, s, A_HEADS * A_V)


def gqa_heads(p, qn, kn):
    b, t = p.shape[:2]
    q = rms_norm(p[..., :B_QCOLS].reshape(b, t, B_HEADS, B_DIM), qn)
    k = rms_norm(p[..., B_QCOLS:B_QCOLS + B_KCOLS].reshape(b, t, B_KV_HEADS, B_DIM), kn)
    v = p[..., B_QCOLS + B_KCOLS:].reshape(b, t, B_KV_HEADS, B_DIM)
    return q, k, v


def window_attend_latent(q, k, v, k_ctx, v_ctx, sink):
    b, s, hq, d = q.shape
    g = hq // B_KV_HEADS
    nb = s // BLOCK
    qb = q.reshape(b, nb, BLOCK, B_KV_HEADS, g, d)
    pad = ((0, 0), (BLOCK, BLOCK), (0, 0), (0, 0))
    kp = jnp.pad(k, pad).reshape(b, nb + 2, BLOCK, B_KV_HEADS, d)
    vp = jnp.pad(v, pad).reshape(b, nb + 2, BLOCK, B_KV_HEADS, d)
    kw = jnp.concatenate([kp[:, :-2], kp[:, 1:-1], kp[:, 2:]], axis=2)
    vw = jnp.concatenate([vp[:, :-2], vp[:, 1:-1], vp[:, 2:]], axis=2)
    s_win = jnp.einsum('bnqhgd,bnkhd->bnhgqk', qb, kw).astype(jnp.float32) * B_SCALE
    qpos = jnp.arange(nb)[:, None] * BLOCK + jnp.arange(BLOCK)[None, :]
    kpos = (jnp.arange(nb)[:, None] - 1) * BLOCK + jnp.arange(3 * BLOCK)[None, :]
    valid = ((jnp.abs(kpos[:, None, :] - qpos[:, :, None]) <= WINDOW)
             & (kpos >= 0)[:, None, :] & (kpos < s)[:, None, :])
    s_win = jnp.where(valid[None, :, None, None], s_win, -jnp.inf)
    s_ctx = jnp.einsum('bnqhgd,bkhd->bnhgqk', qb, k_ctx).astype(jnp.float32) * B_SCALE
    sink_col = jnp.broadcast_to(sink.reshape(1, 1, B_KV_HEADS, g, 1, 1).astype(jnp.float32),
                                s_win.shape[:-1] + (1,))
    p = jax.nn.softmax(jnp.concatenate([s_win, s_ctx, sink_col], axis=-1), axis=-1)
    nw, nc = 3 * BLOCK, k_ctx.shape[1]
    p_win = p[..., :nw].astype(v.dtype)
    p_ctx = p[..., nw:nw + nc].astype(v.dtype)
    o = (jnp.einsum('bnhgqk,bnkhd->bnqhgd', p_win, vw)
         + jnp.einsum('bnhgqk,bkhd->bnqhgd', p_ctx, v_ctx))
    return o.reshape(b, s, hq * d)


def sink_attend_ctx(q, k, v, sink):
    b, t, hq, d = q.shape
    g = hq // B_KV_HEADS
    qg = q.reshape(b, t, B_KV_HEADS, g, d)
    s = jnp.einsum('bqhgd,bkhd->bhgqk', qg, k).astype(jnp.float32) * B_SCALE
    sink_col = jnp.broadcast_to(sink.reshape(1, B_KV_HEADS, g, 1, 1).astype(jnp.float32), s.shape[:-1] + (1,))
    p = jax.nn.softmax(jnp.concatenate([s, sink_col], axis=-1), axis=-1)[..., :-1]
    o = jnp.einsum('bhgqk,bkhd->bqhgd', p.astype(v.dtype), v)
    return o.reshape(b, t, hq * d)


def conv_centred(u, w, bias):
    ch = u.shape[-1]
    y = lax.conv_general_dilated(u, w.reshape(C_CONV, 1, ch).astype(u.dtype), window_strides=(1,),
                                 padding=[(C_CONV // 2, C_CONV // 2)],
                                 dimension_numbers=('NWC', 'WIO', 'NWC'), feature_group_count=ch)
    return jax.nn.silu(y + bias.astype(u.dtype))


def ssm_inputs(p, conv_w, conv_b):
    b, t = p.shape[:2]
    z = p[..., :C_INNER]
    xbc = conv_centred(p[..., C_INNER:C_INNER + C_XBC], conv_w, conv_b)
    xs = xbc[..., :C_INNER].reshape(b, t, C_HEADS, C_HEAD_DIM)
    bm = xbc[..., C_INNER:C_INNER + C_GROUPS * C_STATE].reshape(b, t, C_GROUPS, C_STATE)
    cm = xbc[..., C_INNER + C_GROUPS * C_STATE:].reshape(b, t, C_GROUPS, C_STATE)
    dt_raw = p[..., C_INNER + C_XBC:].reshape(b, t, 2, C_HEADS)
    return z, xs, bm, cm, dt_raw


def segsum(a):
    t = a.shape[-1]
    rep = jnp.broadcast_to(a[..., :, None], a.shape + (t,))
    cs = jnp.cumsum(jnp.where(jnp.tril(jnp.ones((t, t), bool), -1), rep, 0.0), axis=-2)
    return jnp.where(jnp.tril(jnp.ones((t, t), bool)), cs, -jnp.inf)


def ssd_scan(x, dt, A, bm, cm, init):
    b, t, h, pdim = x.shape
    g, n = bm.shape[2], bm.shape[3]
    r = h // g
    nc = t // CHUNK
    f32 = jnp.float32
    xdt = (x.astype(f32) * dt[..., None]).reshape(b, nc, CHUNK, g, r, pdim)
    a = jnp.moveaxis((dt * A).reshape(b, nc, CHUNK, g, r), 2, -1)
    a_cs = jnp.cumsum(a, axis=-1)
    bc = bm.astype(f32).reshape(b, nc, CHUNK, g, n)
    cc = cm.astype(f32).reshape(b, nc, CHUNK, g, n)
    decay_in = jnp.exp(segsum(a))
    cb = jnp.einsum('bclgn,bcsgn->bcgls', cc, bc)
    y_diag = jnp.einsum('bcgls,bcgrls,bcsgrp->bclgrp', cb, decay_in, xdt)
    decay_to_end = jnp.exp(a_cs[..., -1:] - a_cs)
    states = jnp.einsum('bclgn,bcgrl,bclgrp->bcgrpn', bc, decay_to_end, xdt)
    states = jnp.concatenate([init[:, None].astype(f32), states], axis=1)
    chunk_a = jnp.pad(jnp.moveaxis(a_cs[..., -1], 1, -1), ((0, 0), (0, 0), (0, 0), (1, 0)))
    decay_chunk = jnp.exp(segsum(chunk_a))
    new_states = jnp.einsum('bgrzc,bcgrpn->bzgrpn', decay_chunk, states)
    y_off = jnp.einsum('bclgn,bcgrpn,bcgrl->bclgrp', cc, new_states[:, :-1], jnp.exp(a_cs))
    y = (y_diag + y_off).reshape(b, t, h, pdim).astype(x.dtype)
    return y, new_states[:, -1]


def bidir_ssd(xs_c, b_c, c_c, dtr_c, xs_l, b_l, c_l, dtr_l, dt_bias, a_log):
    bsz = xs_l.shape[0]
    init = jnp.zeros((bsz, C_GROUPS, C_HEADS // C_GROUPS, C_HEAD_DIM, C_STATE), jnp.float32)
    y_ctx, y_lat = [], []
    for direction in range(2):
        if direction == 0:
            flip = lambda u: u
        else:
            flip = lambda u: jnp.flip(u, axis=1)
        A = -jnp.exp(a_log[direction].astype(jnp.float32))
        dt_c = jax.nn.softplus(dtr_c[..., direction, :].astype(jnp.float32) + dt_bias[direction].astype(jnp.float32))
        dt_l = jax.nn.softplus(dtr_l[..., direction, :].astype(jnp.float32) + dt_bias[direction].astype(jnp.float32))
        yc, h_ctx = ssd_scan(flip(xs_c), flip(dt_c), A, flip(b_c), flip(c_c), init)
        yl, _ = ssd_scan(flip(xs_l), flip(dt_l), A, flip(b_l), flip(c_l), h_ctx)
        y_ctx.append(flip(yc))
        y_lat.append(flip(yl))
    return y_ctx[0] + y_ctx[1], y_lat[0] + y_lat[1]


def ssm_output(y, xs, z, d_skip, gnorm):
    b, t = y.shape[:2]
    y = (y + d_skip[:, None].astype(y.dtype) * xs).reshape(b, t, C_INNER) * jax.nn.silu(z)
    y = rms_norm(y.reshape(b, t, C_GROUPS, C_INNER // C_GROUPS), gnorm.reshape(C_GROUPS, C_INNER // C_GROUPS))
    return y.reshape(b, t, C_INNER)


def hybrid_mixer(h_ctx, h_lat, cos_a, sin_a, cos_b, sin_b, w_in, qn_a, kn_a, lam, lam_init, subln_a,
                 qn_b, kn_b, sink_b, conv_w, conv_b, dt_bias, a_log, d_skip, gnorm_c, with_ctx_out):
    p_ctx = h_ctx @ w_in
    p_lat = h_lat @ w_in
    o1, o2 = A_COLS, A_COLS + B_COLS
    qa_c, ka_c, va_c = diff_heads(p_ctx[..., :o1], qn_a, kn_a)
    qa_l, ka_l, va_l = diff_heads(p_lat[..., :o1], qn_a, kn_a)
    qa_l = axial_rope(qa_l, cos_a, sin_a)
    ka_l = axial_rope(ka_l, cos_a, sin_a)
    ka_all = jnp.concatenate([ka_c, ka_l], axis=1)
    va_all = jnp.concatenate([va_c, va_l], axis=1)
    oa_l = diff_attend_latent(qa_l, ka_all, va_all, lam, lam_init, subln_a)
    qb_c, kb_c, vb_c = gqa_heads(p_ctx[..., o1:o2], qn_b, kn_b)
    qb_l, kb_l, vb_l = gqa_heads(p_lat[..., o1:o2], qn_b, kn_b)
    qb_l = axial_rope(qb_l, cos_b, sin_b)
    kb_l = axial_rope(kb_l, cos_b, sin_b)
    ob_l = window_attend_latent(qb_l, kb_l, vb_l, kb_c, vb_c, sink_b)
    z_c, xs_c, b_c, c_c, dtr_c = ssm_inputs(p_ctx[..., o2:], conv_w, conv_b)
    z_l, xs_l, b_l, c_l, dtr_l = ssm_inputs(p_lat[..., o2:], conv_w, conv_b)
    y_c, y_l = bidir_ssd(xs_c, b_c, c_c, dtr_c, xs_l, b_l, c_l, dtr_l, dt_bias, a_log)
    oc_l = ssm_output(y_l, xs_l, z_l, d_skip, gnorm_c)
    out_lat = jnp.concatenate([oa_l, ob_l, oc_l], axis=-1)
    if not with_ctx_out:
        return None, out_lat
    bsz, tc = h_ctx.shape[:2]
    oa_c = diff_attend(qa_c, ka_c, va_c, lam, lam_init, subln_a).reshape(bsz, tc, A_HEADS * A_V)
    ob_c = sink_attend_ctx(qb_c, kb_c, vb_c, sink_b)
    oc_c = ssm_output(y_c, xs_c, z_c, d_skip, gnorm_c)
    out_ctx = jnp.concatenate([oa_c, ob_c, oc_c], axis=-1)
    return out_ctx, out_lat


def setup_inputs(seed: int = 0) -> dict:
    key = jax.random.key(seed)
    ks = iter(jax.random.split(key, 40))
    L = DEPTH

    def nrm(shape, scale):
        return jax.random.normal(next(ks), shape, jnp.float32) * scale

    def gain(shape):
        return 1.0 + 0.02 * jax.random.normal(next(ks), shape, jnp.float32)

    x = nrm((BATCH, SEQ, D_MODEL), 1.0)
    c = nrm((BATCH, D_MODEL), 1.0)
    ctx = nrm((BATCH, CTX_LEN, D_MODEL), 1.0)
    c_ctx = nrm((D_MODEL,), 1.0)
    w_mod = nrm((L, D_MODEL, N_MOD * D_MODEL), 0.5 * D_MODEL ** -0.5)
    b_mod = nrm((L, N_MOD * D_MODEL), 0.02)
    norm_ffn1 = gain((L, D_MODEL))
    ffn1_w13 = nrm((L, D_MODEL, 2 * D_FF), D_MODEL ** -0.5)
    ffn1_w2 = nrm((L, D_FF, D_MODEL), D_FF ** -0.5)
    norm_mix = gain((L, D_MODEL))
    w_in = nrm((L, D_MODEL, IN_COLS), D_MODEL ** -0.5)
    w_out = nrm((L, MIX_WIDTH, D_MODEL), MIX_WIDTH ** -0.5)
    qn_a = gain((L, A_QK))
    kn_a = gain((L, A_QK))
    lam_q1 = nrm((L, A_QK), 0.1)
    lam_k1 = nrm((L, A_QK), 0.1)
    lam_q2 = nrm((L, A_QK), 0.1)
    lam_k2 = nrm((L, A_QK), 0.1)
    subln_a = gain((L, A_V))
    qn_b = gain((L, B_DIM))
    kn_b = gain((L, B_DIM))
    sink_b = nrm((L, B_HEADS), 0.5)
    conv_w = nrm((L, C_CONV, C_XBC), C_CONV ** -0.5)
    conv_b = nrm((L, C_XBC), 0.01)
    dt0 = jnp.exp(jax.random.uniform(next(ks), (L, 2, C_HEADS), jnp.float32, math.log(1e-3), math.log(1e-1)))
    dt_bias = dt0 + jnp.log(-jnp.expm1(-dt0))
    a_log = jnp.log(jax.random.uniform(next(ks), (L, 2, C_HEADS), jnp.float32, 1.0, 16.0))
    d_skip = gain((L, C_HEADS))
    gnorm_c = gain((L, C_INNER))
    norm_ffn2 = gain((L, D_MODEL))
    ffn2_w13 = nrm((L, D_MODEL, 2 * D_FF), D_MODEL ** -0.5)
    ffn2_w2 = nrm((L, D_FF, D_MODEL), D_FF ** -0.5)
    return {'x': x, 'c': c, 'ctx': ctx, 'c_ctx': c_ctx, 'w_mod': w_mod, 'b_mod': b_mod,
            'norm_ffn1': norm_ffn1, 'ffn1_w13': ffn1_w13, 'ffn1_w2': ffn1_w2,
            'norm_mix': norm_mix, 'w_in': w_in, 'w_out': w_out,
            'qn_a': qn_a, 'kn_a': kn_a, 'lam_q1': lam_q1, 'lam_k1': lam_k1, 'lam_q2': lam_q2, 'lam_k2': lam_k2,
            'subln_a': subln_a, 'qn_b': qn_b, 'kn_b': kn_b, 'sink_b': sink_b,
            'conv_w': conv_w, 'conv_b': conv_b, 'dt_bias': dt_bias, 'a_log': a_log, 'd_skip': d_skip,
            'gnorm_c': gnorm_c, 'norm_ffn2': norm_ffn2, 'ffn2_w13': ffn2_w13, 'ffn2_w2': ffn2_w2}


def reference(x, c, ctx, c_ctx, w_mod, b_mod, norm_ffn1, ffn1_w13, ffn1_w2, norm_mix, w_in, w_out,
              qn_a, kn_a, lam_q1, lam_k1, lam_q2, lam_k2, subln_a, qn_b, kn_b, sink_b,
              conv_w, conv_b, dt_bias, a_log, d_skip, gnorm_c, norm_ffn2, ffn2_w13, ffn2_w2):
    s = x.shape[1]
    cos_a, sin_a = axial_rope_tables(s, A_QK)
    cos_b, sin_b = axial_rope_tables(s, B_DIM)
    sc_lat = jax.nn.silu(c)
    sc_ctx = jax.nn.silu(c_ctx)
    xl, xc = x, ctx
    for l in range(DEPTH):
        last = l == DEPTH - 1
        ml = adaln_chunks(sc_lat @ w_mod[l] + b_mod[l])
        mc = adaln_chunks(sc_ctx @ w_mod[l] + b_mod[l])
        xl = xl + 0.5 * ml[2] * swiglu(modulate(rms_norm(xl, norm_ffn1[l]), ml[0], ml[1]), ffn1_w13[l], ffn1_w2[l])
        xc = xc + 0.5 * mc[2] * swiglu(modulate(rms_norm(xc, norm_ffn1[l]), mc[0], mc[1]), ffn1_w13[l], ffn1_w2[l])
        lam_init = 0.8 - 0.6 * math.exp(-0.3 * l)
        lam = (jnp.exp(jnp.sum(lam_q1[l].astype(jnp.float32) * lam_k1[l].astype(jnp.float32)))
               - jnp.exp(jnp.sum(lam_q2[l].astype(jnp.float32) * lam_k2[l].astype(jnp.float32))) + lam_init)
        h_ctx = modulate(rms_norm(xc, norm_mix[l]), mc[3], mc[4])
        h_lat = modulate(rms_norm(xl, norm_mix[l]), ml[3], ml[4])
        mix_c, mix_l = hybrid_mixer(h_ctx, h_lat, cos_a, sin_a, cos_b, sin_b, w_in[l], qn_a[l], kn_a[l], lam,
                                    lam_init, subln_a[l], qn_b[l], kn_b[l], sink_b[l], conv_w[l], conv_b[l],
                                    dt_bias[l], a_log[l], d_skip[l], gnorm_c[l], not last)
        xl = xl + ml[5] * (mix_l @ w_out[l])
        xl = xl + 0.5 * ml[8] * swiglu(modulate(rms_norm(xl, norm_ffn2[l]), ml[6], ml[7]), ffn2_w13[l], ffn2_w2[l])
        if not last:
            xc = xc + mc[5] * (mix_c @ w_out[l])
            xc = xc + 0.5 * mc[8] * swiglu(modulate(rms_norm(xc, norm_ffn2[l]), mc[6], mc[7]), ffn2_w13[l], ffn2_w2[l])
    return xl
```

```python
import functools
import math

import jax
import jax.numpy as jnp
from jax import lax
from jax.experimental import pallas as pl
from jax.experimental.pallas import tpu as pltpu

F32 = jnp.float32
BF16 = jnp.bfloat16
HIGHEST = lax.Precision.HIGHEST
NEG_INF = float("-inf")

EPS = 1e-6
ROPE_BASE = 10000.0
GRID_W = 64
N_MOD = 9

A_HEADS, A_QK, A_V = 4, 32, 64
A_QCOLS = A_HEADS * 2 * A_QK
A_SCALE = A_QK ** -0.5
B_HEADS, B_KV_HEADS, B_DIM = 4, 2, 64
B_QCOLS = B_HEADS * B_DIM
B_KCOLS = B_KV_HEADS * B_DIM
B_SCALE = B_DIM ** -0.5
B_BLOCK = 128
C_HEADS, C_HEAD_DIM, C_GROUPS, C_STATE, C_CONV = 8, 64, 2, 64, 5
C_INNER = C_HEADS * C_HEAD_DIM
C_GN = C_GROUPS * C_STATE
C_XBC = C_INNER + 2 * C_GN
C_CHUNK = 128
C_HPG = C_HEADS // C_GROUPS
DT_LANES = 128
HALO = 8

V7X_VMEM_LIMIT = 56 * 1024 * 1024


def _params(sem, vmem=V7X_VMEM_LIMIT):
    return pltpu.CompilerParams(dimension_semantics=sem, vmem_limit_bytes=vmem)


def _rms_mod(x, nw, shift, scale):
    ms = jnp.mean(x * x, axis=-1, keepdims=True)
    y = x * lax.rsqrt(ms + EPS) * nw
    return y * (1.0 + scale) + shift


def _silu(v):
    return v * jax.nn.sigmoid(v)


def _mod_kernel(c_ref, w_ref, b_ref, o_ref):
    sc = _silu(c_ref[...])
    o_ref[...] = jnp.dot(sc, w_ref[...], precision=HIGHEST, preferred_element_type=F32) + b_ref[...]


def _mod_call(cvec, w_mod, b_mod, tn=1152):
    nl, d, n = w_mod.shape
    return pl.pallas_call(
        _mod_kernel,
        grid=(nl, n // tn),
        in_specs=[pl.BlockSpec((8, d), lambda l, j: (0, 0)),
                  pl.BlockSpec((None, d, tn), lambda l, j: (l, 0, j)),
                  pl.BlockSpec((None, 1, tn), lambda l, j: (l, 0, j))],
        out_specs=pl.BlockSpec((None, 8, tn), lambda l, j: (l, 0, j)),
        out_shape=jax.ShapeDtypeStruct((nl, 8, n), F32),
        compiler_params=_params(("parallel", "parallel")),
        name="adaln_mod",
    )(cvec, w_mod, b_mod.reshape(nl, 1, n))


def _ffn_kernel(x_ref, mod_ref, nw_ref, w13_ref, w2_ref, o_ref, a_scr, *, si, fc, nchunks):
    x = x_ref[...]
    m = mod_ref[...]
    h = _rms_mod(x, nw_ref[...], m[si:si + 1], m[si + 1:si + 2]).astype(BF16)
    for c in range(nchunks):
        gu = jnp.dot(h, w13_ref[c], preferred_element_type=F32)
        a_scr[:, c * fc:(c + 1) * fc] = (_silu(gu[:, :fc]) * gu[:, fc:]).astype(BF16)
    y = jnp.dot(a_scr[...], w2_ref[...], preferred_element_type=F32)
    o_ref[...] = x + (0.5 * m[si + 2:si + 3]) * y


def _ffn_call(xs, mod, nw, w13r, w2, *, si, tm, ctx_tiles, tile0, nbatch):
    b, t, d = xs.shape
    nchunks, _, fc2 = w13r.shape
    fc = fc2 // 2
    ntiles = t // tm - tile0
    mod_row = lambda bi, i: (jnp.where(i + tile0 < ctx_tiles, nbatch, bi), 0, 0)
    return pl.pallas_call(
        functools.partial(_ffn_kernel, si=si, fc=fc, nchunks=nchunks),
        grid=(b, ntiles),
        in_specs=[pl.BlockSpec((None, tm, d), lambda bi, i: (bi, i + tile0, 0)),
                  pl.BlockSpec((None, N_MOD, d), mod_row),
                  pl.BlockSpec((1, d), lambda bi, i: (0, 0)),
                  pl.BlockSpec(w13r.shape, lambda bi, i: (0, 0, 0), pipeline_mode=pl.Buffered(1)),
                  pl.BlockSpec(w2.shape, lambda bi, i: (0, 0), pipeline_mode=pl.Buffered(1))],
        out_specs=pl.BlockSpec((None, tm, d), lambda bi, i: (bi, i, 0)),
        out_shape=jax.ShapeDtypeStruct((b, ntiles * tm, d), F32),
        scratch_shapes=[pltpu.VMEM((tm, nchunks * fc), BF16)],
        compiler_params=_params(("parallel", "parallel")),
        name="swiglu_half",
    )(xs, mod, nw.reshape(1, d), w13r, w2)


def _rope(v, cos, sin_signed, quarter):
    n = v.shape[-1]
    lane = lax.broadcasted_iota(jnp.int32, v.shape, 1)
    first = (lane & (2 * quarter - 1)) < quarter
    vr = jnp.where(first, pltpu.roll(v, n - quarter, 1), pltpu.roll(v, quarter, 1))
    return v * cos + vr * sin_signed


def _group_norm(v, gmat, inv_n, w):
    ms = jnp.dot(v * v, gmat, precision=HIGHEST, preferred_element_type=F32) * inv_n
    return v * lax.rsqrt(ms + EPS) * w


def _inproj_kernel(x_ref, mod_ref, nw_ref, w_ref, ga_ref, gb_ref, hn_ref, cosa_ref, sina_ref, cosb_ref, sinb_ref,
                   qa_o, ka_o, va_o, qb_o, kb_o, vb_o, z_o, xbc_o, dt_o):
    m = mod_ref[...]
    h = _rms_mod(x_ref[...], nw_ref[...], m[3:4], m[4:5]).astype(BF16)
    p = jnp.dot(h, w_ref[...], preferred_element_type=F32)
    hn = hn_ref[...]
    ga, gb = ga_ref[...], gb_ref[...]
    cosa, sina, cosb, sinb = cosa_ref[...], sina_ref[...], cosb_ref[...], sinb_ref[...]
    o = 0
    qa = _group_norm(p[:, o:o + A_QCOLS], ga, 1.0 / A_QK, hn[0:1]); o += A_QCOLS
    ka = _group_norm(p[:, o:o + A_QCOLS], ga, 1.0 / A_QK, hn[1:2]); o += A_QCOLS
    qa_o[...] = (_rope(qa, cosa, sina, A_QK // 4) * A_SCALE).astype(BF16)
    ka_o[...] = _rope(ka, cosa, sina, A_QK // 4).astype(BF16)
    va_o[...] = p[:, o:o + A_HEADS * A_V].astype(BF16); o += A_HEADS * A_V
    qb = _group_norm(p[:, o:o + B_QCOLS], gb, 1.0 / B_DIM, hn[2:3]); o += B_QCOLS
    kb = _group_norm(p[:, o:o + B_KCOLS], gb[:B_KCOLS, :B_KCOLS], 1.0 / B_DIM, hn[3:4, :B_KCOLS]); o += B_KCOLS
    qb_o[...] = (_rope(qb, cosb, sinb, B_DIM // 4) * B_SCALE).astype(BF16)
    kb_o[...] = _rope(kb, cosb[:, :B_KCOLS], sinb[:, :B_KCOLS], B_DIM // 4).astype(BF16)
    vb_o[...] = p[:, o:o + B_KCOLS].astype(BF16); o += B_KCOLS
    z_o[...] = p[:, o:o + C_INNER]; o += C_INNER
    xbc_o[...] = p[:, o:o + C_XBC]; o += C_XBC
    dt_o[...] = p[:, o:o + DT_LANES]


def _inproj_call(xs, mod, nw, w_in_p, ga, gb, hn, cosa, sina, cosb, sinb, *, tm, ctx_tiles, nbatch):
    b, t, d = xs.shape
    ntiles = t // tm
    tok = lambda w: pl.BlockSpec((None, tm, w), lambda bi, i: (bi, i, 0))
    tab = lambda w: pl.BlockSpec((tm, w), lambda bi, i: (i, 0))
    full = lambda a: pl.BlockSpec(a.shape, lambda bi, i: (0,) * a.ndim)
    mod_row = lambda bi, i: (jnp.where(i < ctx_tiles, nbatch, bi), 0, 0)
    widths = (A_QCOLS, A_QCOLS, A_HEADS * A_V, B_QCOLS, B_KCOLS, B_KCOLS, C_INNER, C_XBC, DT_LANES)
    dtypes = (BF16,) * 6 + (F32,) * 3
    return pl.pallas_call(
        _inproj_kernel,
        grid=(b, ntiles),
        in_specs=[tok(d), pl.BlockSpec((None, N_MOD, d), mod_row), pl.BlockSpec((1, d), lambda bi, i: (0, 0)),
                  pl.BlockSpec(w_in_p.shape, lambda bi, i: (0, 0), pipeline_mode=pl.Buffered(1)),
                  full(ga), full(gb), full(hn), tab(A_QCOLS), tab(A_QCOLS), tab(B_QCOLS), tab(B_QCOLS)],
        out_specs=[tok(w) for w in widths],
        out_shape=[jax.ShapeDtypeStruct((b, t, w), dt) for w, dt in zip(widths, dtypes)],
        compiler_params=_params(("parallel", "parallel")),
        name="in_proj_heads",
    )(xs, mod, nw.reshape(1, d), w_in_p, ga, gb, hn, cosa, sina, cosb, sinb)


def _attn_a_kernel(lamv_ref, subln_ref, q_ref, kt_ref, v_ref, o_ref, m_scr, acc_scr, *,
                   tq, tk, ctx, nk, q_tile0, lam_init):
    qi = pl.program_id(2) + q_tile0
    ki = pl.program_id(3)
    ctx_q = qi * tq < ctx

    @pl.when(ki == 0)
    def _():
        m_scr[...] = jnp.full(m_scr.shape, NEG_INF, F32)
        acc_scr[...] = jnp.zeros(acc_scr.shape, F32)

    def step(masked):
        v = v_ref[...]
        for c in range(2):
            s = jnp.dot(q_ref[c], kt_ref[c], preferred_element_type=F32)
            if masked:
                col = ki * tk + lax.broadcasted_iota(jnp.int32, s.shape, 1)
                s = jnp.where(col < ctx, s, NEG_INF)
            m_prev = m_scr[c]
            m_new = jnp.maximum(m_prev, jnp.max(s, axis=1, keepdims=True))
            alpha = jnp.exp(m_prev - m_new)
            p = jnp.exp(s - m_new).astype(BF16)
            acc_scr[c] = alpha * acc_scr[c] + jnp.dot(p, v, preferred_element_type=F32)
            m_scr[c] = m_new

    pl.when(jnp.logical_and(ctx_q, ki * tk < ctx))(lambda: step(True))
    pl.when(jnp.logical_not(ctx_q))(lambda: step(False))

    @pl.when(ki == nk - 1)
    def _():
        a0, a1 = acc_scr[0], acc_scr[1]
        o0 = a0[:, :A_V] / a0[:, A_V:]
        o1 = a1[:, :A_V] / a1[:, A_V:]
        lv = lamv_ref[...]
        lam = (jnp.exp(jnp.sum(lv[0:1] * lv[1:2], axis=1, keepdims=True))
               - jnp.exp(jnp.sum(lv[2:3] * lv[3:4], axis=1, keepdims=True)) + lam_init)
        o = o0 - lam * o1
        ms = jnp.mean(o * o, axis=1, keepdims=True)
        o_ref[...] = o * lax.rsqrt(ms + EPS) * subln_ref[...] * (1.0 - lam_init)


def _attn_a_call(lamv, subln, qa5, kt5, ve, *, tq, tk, ctx, q_tile0, lam_init):
    b, h, _, t, dq = qa5.shape
    nq = t // tq - q_tile0
    nk = t // tk
    return pl.pallas_call(
        functools.partial(_attn_a_kernel, tq=tq, tk=tk, ctx=ctx, nk=nk, q_tile0=q_tile0, lam_init=lam_init),
        grid=(b, h, nq, nk),
        in_specs=[pl.BlockSpec(lamv.shape, lambda bi, hi, qi, ki: (0, 0)),
                  pl.BlockSpec(subln.shape, lambda bi, hi, qi, ki: (0, 0)),
                  pl.BlockSpec((None, None, 2, tq, dq), lambda bi, hi, qi, ki: (bi, hi, 0, qi + q_tile0, 0)),
                  pl.BlockSpec((None, None, 2, dq, tk), lambda bi, hi, qi, ki: (bi, hi, 0, 0, ki)),
                  pl.BlockSpec((None, None, tk, 2 * A_V), lambda bi, hi, qi, ki: (bi, hi, ki, 0))],
        out_specs=pl.BlockSpec((None, None, tq, A_V), lambda bi, hi, qi, ki: (bi, hi, qi + q_tile0, 0)),
        out_shape=jax.ShapeDtypeStruct((b, h, t, A_V), F32),
        scratch_shapes=[pltpu.VMEM((2, tq, 1), F32), pltpu.VMEM((2, tq, 2 * A_V), F32)],
        compiler_params=_params(("parallel", "parallel", "parallel", "arbitrary")),
        name="diff_attention",
    )(lamv, subln, qa5, kt5, ve)


def _attn_b_kernel(sink_ref, q_ref, kp_ref, kc_ref, kn_ref, kx_ref, vp_ref, vc_ref, vn_ref, vx_ref, o_ref, *,
                   blk0, ctx_blocks, nblk):
    n = pl.program_id(1) + blk0
    q = q_ref[...]
    k_all = jnp.concatenate([kp_ref[...], kc_ref[...], kn_ref[...], kx_ref[...]], axis=0)
    v_all = jnp.concatenate([vp_ref[...], vc_ref[...], vn_ref[...], vx_ref[...]], axis=0)
    nkeys = k_all.shape[0]
    i = lax.broadcasted_iota(jnp.int32, (B_BLOCK, nkeys), 0)
    j = lax.broadcasted_iota(jnp.int32, (B_BLOCK, nkeys), 1)
    lat = n >= ctx_blocks
    pen_prev = jnp.where(n - 1 >= ctx_blocks, 0.0, NEG_INF)
    pen_cur = jnp.where(lat, 0.0, NEG_INF)
    pen_next = jnp.where(jnp.logical_and(lat, n + 1 <= nblk - 1), 0.0, NEG_INF)
    bias = jnp.where(j < B_BLOCK, jnp.where(j >= i, pen_prev, NEG_INF),
                     jnp.where(j < 2 * B_BLOCK, pen_cur,
                               jnp.where(j < 3 * B_BLOCK, jnp.where(j - 2 * B_BLOCK <= i, pen_next, NEG_INF), 0.0)))
    outs = []
    for hq in range(B_HEADS):
        g = hq // (B_HEADS // B_KV_HEADS)
        s = lax.dot_general(q[:, hq * B_DIM:(hq + 1) * B_DIM], k_all[:, g * B_DIM:(g + 1) * B_DIM],
                            (((1,), (1,)), ((), ())), preferred_element_type=F32)
        s = s + bias
        sk = sink_ref[hq]
        m = jnp.maximum(jnp.max(s, axis=1, keepdims=True), sk)
        p = jnp.exp(s - m)
        den = jnp.sum(p, axis=1, keepdims=True) + jnp.exp(sk - m)
        pv = jnp.dot(p.astype(BF16), v_all[:, g * B_DIM:(g + 1) * B_DIM], preferred_element_type=F32)
        outs.append(pv / den)
    o_ref[...] = jnp.concatenate(outs, axis=1)


def _attn_b_call(sink, qb, kb, vb, *, ctx, blk0):
    b, t, _ = qb.shape
    nblk = t // B_BLOCK
    ctx_blocks = ctx // B_BLOCK
    cur = lambda bi, i: (bi, i + blk0, 0)
    prev = lambda bi, i: (bi, jnp.maximum(i + blk0 - 1, 0), 0)
    nxt = lambda bi, i: (bi, jnp.minimum(i + blk0 + 1, nblk - 1), 0)
    kv = lambda im: pl.BlockSpec((None, B_BLOCK, B_KCOLS), im)
    cx = pl.BlockSpec((None, ctx, B_KCOLS), lambda bi, i: (bi, 0, 0))
    return pl.pallas_call(
        functools.partial(_attn_b_kernel, blk0=blk0, ctx_blocks=ctx_blocks, nblk=nblk),
        grid=(b, nblk - blk0),
        in_specs=[pl.BlockSpec(memory_space=pltpu.SMEM),
                  pl.BlockSpec((None, B_BLOCK, B_QCOLS), cur),
                  kv(prev), kv(cur), kv(nxt), cx, kv(prev), kv(cur), kv(nxt), cx],
        out_specs=pl.BlockSpec((None, B_BLOCK, B_QCOLS), cur),
        out_shape=jax.ShapeDtypeStruct((b, t, B_QCOLS), F32),
        compiler_params=_params(("parallel", "parallel")),
        name="window_attention",
    )(sink, qb, kb, kb, kb, kb, vb, vb, vb, vb)


def _conv_kernel(u_ref, up_ref, un_ref, w_ref, b_ref, o_ref, *, tm, ctx_tiles, ntiles):
    i = pl.program_id(1)
    u = u_ref[...]
    has_prev = jnp.logical_and(i != 0, i != ctx_tiles)
    has_next = jnp.logical_and(i != ctx_tiles - 1, i != ntiles - 1)
    up = jnp.where(has_prev, up_ref[...], 0.0)
    un = jnp.where(has_next, un_ref[...], 0.0)
    full = jnp.concatenate([up, u, un], axis=0)
    w = w_ref[...]
    acc = b_ref[...] + w[0:1] * full[HALO - 2:HALO - 2 + tm]
    for k in range(1, C_CONV):
        acc = acc + w[k:k + 1] * full[HALO - 2 + k:HALO - 2 + k + tm]
    o_ref[...] = _silu(acc)


def _conv_call(xbc_raw, conv_w, conv_b, *, tm, ctx):
    b, t, ch = xbc_raw.shape
    ntiles = t // tm
    hb = tm // HALO
    nh = t // HALO
    w8 = jnp.zeros((8, ch), F32).at[:C_CONV].set(conv_w)
    return pl.pallas_call(
        functools.partial(_conv_kernel, tm=tm, ctx_tiles=ctx // tm, ntiles=ntiles),
        grid=(b, ntiles),
        in_specs=[pl.BlockSpec((None, tm, ch), lambda bi, i: (bi, i, 0)),
                  pl.BlockSpec((None, HALO, ch), lambda bi, i: (bi, jnp.maximum(i * hb - 1, 0), 0)),
                  pl.BlockSpec((None, HALO, ch), lambda bi, i: (bi, jnp.minimum((i + 1) * hb, nh - 1), 0)),
                  pl.BlockSpec((8, ch), lambda bi, i: (0, 0)),
                  pl.BlockSpec((1, ch), lambda bi, i: (0, 0))],
        out_specs=pl.BlockSpec((None, tm, ch), lambda bi, i: (bi, i, 0)),
        out_shape=jax.ShapeDtypeStruct((b, t, ch), F32),
        compiler_params=_params(("parallel", "parallel")),
        name="ssm_conv",
    )(xbc_raw, xbc_raw, xbc_raw, w8, conv_b.reshape(1, ch))


def _softplus(v):
    return jnp.maximum(v, 0.0) + jnp.log1p(jnp.exp(-jnp.abs(v)))


def _ssd_direction(x_ref, dt_ref, par, y_ref, h_scr, *, lane0, backward):
    xbc = x_ref[...]
    xs = xbc[:, :C_INNER]
    bm = xbc[:, C_INNER:C_INNER + C_GN]
    cm = xbc[:, C_INNER + C_GN:].astype(BF16)
    dt = _softplus(dt_ref[...] + par[0:1])
    a = dt * par[1:2]
    li = lax.broadcasted_iota(jnp.int32, (C_CHUNK, C_CHUNK), 0)
    si = lax.broadcasted_iota(jnp.int32, (C_CHUNK, C_CHUNK), 1)
    tri = (si <= li).astype(F32)
    cs = jnp.dot(tri, a, precision=HIGHEST, preferred_element_type=F32)
    tot = cs[C_CHUNK - 1:C_CHUNK, :]
    key = cs - a if backward else cs
    key_t = key.T
    bm_t = bm.T
    bm16 = bm.astype(BF16)
    mask = (si >= li) if backward else (si <= li)
    ys = []
    for g in range(C_GROUPS):
        gs = slice(g * C_STATE, (g + 1) * C_STATE)
        cb = lax.dot_general(cm[:, gs], bm16[:, gs], (((1,), (1,)), ((), ())), preferred_element_type=F32)
        for r in range(C_HPG):
            h = g * C_HPG + r
            hl = lane0 + h
            col = key[:, hl:hl + 1]
            row = key_t[hl:hl + 1, :]
            tot_h = tot[:, hl:hl + 1]
            d = (row - col) if backward else (col - row)
            decay = jnp.exp(jnp.where(mask, d, NEG_INF))
            xdt = (xs[:, h * C_HEAD_DIM:(h + 1) * C_HEAD_DIM] * dt[:, hl:hl + 1]).astype(BF16)
            y_diag = jnp.dot((cb * decay).astype(BF16), xdt, preferred_element_type=F32)
            h_prev = h_scr[h]
            off_scale = jnp.exp(tot_h - col) if backward else jnp.exp(col)
            y_off = jnp.dot(cm[:, gs], h_prev.astype(BF16), preferred_element_type=F32) * off_scale
            ys.append(y_diag + y_off)
            w_row = jnp.exp(row) if backward else jnp.exp(tot_h - row)
            bw_t = (bm_t[gs, :] * w_row).astype(BF16)
            h_scr[h] = jnp.exp(tot_h) * h_prev + jnp.dot(bw_t, xdt, preferred_element_type=F32)
    y_ref[...] = jnp.concatenate(ys, axis=1)


def _ssd_kernel(par_ref, xf_ref, dtf_ref, xb_ref, dtb_ref, yf_ref, yb_ref, hf_scr, hb_scr):
    @pl.when(pl.program_id(1) == 0)
    def _():
        hf_scr[...] = jnp.zeros(hf_scr.shape, F32)
        hb_scr[...] = jnp.zeros(hb_scr.shape, F32)

    par = par_ref[...]
    _ssd_direction(xf_ref, dtf_ref, par, yf_ref, hf_scr, lane0=0, backward=False)
    _ssd_direction(xb_ref, dtb_ref, par, yb_ref, hb_scr, lane0=C_HEADS, backward=True)


def _ssd_call(par, xbc, dt, *, ctx):
    b, t, ch = xbc.shape
    nc = t // C_CHUNK
    ncc = ctx // C_CHUNK
    fwd = lambda bi, j: (bi, j, 0)
    bwd = lambda bi, j: (bi, jnp.where(j < ncc, ncc - 1 - j, nc - 1 + ncc - j), 0)
    return pl.pallas_call(
        _ssd_kernel,
        grid=(b, nc),
        in_specs=[pl.BlockSpec((8, DT_LANES), lambda bi, j: (0, 0)),
                  pl.BlockSpec((None, C_CHUNK, ch), fwd), pl.BlockSpec((None, C_CHUNK, DT_LANES), fwd),
                  pl.BlockSpec((None, C_CHUNK, ch), bwd), pl.BlockSpec((None, C_CHUNK, DT_LANES), bwd)],
        out_specs=[pl.BlockSpec((None, C_CHUNK, C_INNER), fwd), pl.BlockSpec((None, C_CHUNK, C_INNER), bwd)],
        out_shape=[jax.ShapeDtypeStruct((b, t, C_INNER), F32)] * 2,
        scratch_shapes=[pltpu.VMEM((C_HEADS, C_STATE, C_HEAD_DIM), F32)] * 2,
        compiler_params=_params(("parallel", "arbitrary")),
        name="ssd_scan",
    )(par, xbc, dt, xbc, dt)


def _outproj_kernel(x_ref, mod_ref, oa_ref, ob_ref, yf_ref, yb_ref, xs_ref, z_ref, dg_ref, w_ref, o_ref):
    m = mod_ref[...]
    dg = dg_ref[...]
    y = (yf_ref[...] + yb_ref[...] + dg[0:1] * xs_ref[...]) * _silu(z_ref[...])
    gw = C_INNER // C_GROUPS
    oc = []
    for g in range(C_GROUPS):
        yg = y[:, g * gw:(g + 1) * gw]
        ms = jnp.mean(yg * yg, axis=1, keepdims=True)
        oc.append(yg * lax.rsqrt(ms + EPS) * dg[1:2, g * gw:(g + 1) * gw])
    mix = jnp.concatenate([oa_ref[...], ob_ref[...]] + oc, axis=1).astype(BF16)
    o_ref[...] = x_ref[...] + m[5:6] * jnp.dot(mix, w_ref[...], preferred_element_type=F32)


def _outproj_call(xs, mod, oa, ob, yf, yb, xbc, z, dg, w_out, *, tm, ctx_tiles, tile0, nbatch):
    b, t, d = xs.shape
    ntiles = t // tm - tile0
    tok = lambda w: pl.BlockSpec((None, tm, w), lambda bi, i: (bi, i + tile0, 0))
    mod_row = lambda bi, i: (jnp.where(i + tile0 < ctx_tiles, nbatch, bi), 0, 0)
    return pl.pallas_call(
        _outproj_kernel,
        grid=(b, ntiles),
        in_specs=[tok(d), pl.BlockSpec((None, N_MOD, d), mod_row),
                  tok(A_HEADS * A_V), tok(B_QCOLS), tok(C_INNER), tok(C_INNER), tok(C_INNER), tok(C_INNER),
                  pl.BlockSpec(dg.shape, lambda bi, i: (0, 0)),
                  pl.BlockSpec(w_out.shape, lambda bi, i: (0, 0), pipeline_mode=pl.Buffered(1))],
        out_specs=pl.BlockSpec((None, tm, d), lambda bi, i: (bi, i + tile0, 0)),
        out_shape=jax.ShapeDtypeStruct((b, t, d), F32),
        compiler_params=_params(("parallel", "parallel")),
        name="mix_out_proj",
    )(xs, mod, oa, ob, yf, yb, xbc, z, dg, w_out)


def _rope_tables(seq, ctx, dim, reps):
    rows = seq // GRID_W
    row = jnp.repeat(jnp.arange(rows, dtype=F32), GRID_W)
    col = jnp.tile(jnp.arange(GRID_W, dtype=F32), rows)
    quarter = dim // 4
    inv_freq = ROPE_BASE ** (-jnp.arange(quarter, dtype=F32) / quarter)
    ar = row[:, None] * inv_freq
    ac = col[:, None] * inv_freq
    ang = jnp.concatenate([ar, ar, ac, ac], axis=-1)
    sign = jnp.where((jnp.arange(dim) % (dim // 2)) < quarter, -1.0, 1.0).astype(F32)
    cos = jnp.concatenate([jnp.ones((ctx, dim), F32), jnp.cos(ang)], axis=0)
    sin = jnp.concatenate([jnp.zeros((ctx, dim), F32), jnp.sin(ang) * sign], axis=0)
    return jnp.tile(cos, (1, reps)), jnp.tile(sin, (1, reps))


def _group_matrix(n, group):
    idx = jnp.arange(n) // group
    return (idx[:, None] == idx[None, :]).astype(F32)


def _split_ffn_weights(w13, w2, fc):
    d, two_ff = w13.shape
    ff = two_ff // 2
    nch = ff // fc
    w1 = w13[:, :ff].reshape(d, nch, fc)
    w3 = w13[:, ff:].reshape(d, nch, fc)
    w13r = jnp.concatenate([w1, w3], axis=2).transpose(1, 0, 2).astype(BF16)
    return w13r, w2.astype(BF16)


def kernel(x, c, ctx, c_ctx, w_mod, b_mod, norm_ffn1, ffn1_w13, ffn1_w2, norm_mix, w_in, w_out, qn_a, kn_a, lam_q1, lam_k1, lam_q2, lam_k2, subln_a, qn_b, kn_b, sink_b, conv_w, conv_b, dt_bias, a_log, d_skip, gnorm_c, norm_ffn2, ffn2_w13, ffn2_w2):
    nb, seq, d = x.shape
    nctx = ctx.shape[1]
    t = nctx + seq
    depth = w_mod.shape[0]
    tm = 256
    fc = 256
    tq, tk = 256, (768 if t % 768 == 0 else 256)
    assert nctx % tm == 0 and t % tm == 0 and nctx % C_CHUNK == 0 and seq % GRID_W == 0 and nb < 8
    ctx_tiles = nctx // tm

    xs = jnp.concatenate([ctx, x], axis=1)
    cvec = jnp.zeros((8, d), F32).at[:nb].set(c).at[nb].set(c_ctx)
    mod = _mod_call(cvec, w_mod, b_mod).reshape(depth, 8, N_MOD, d)

    cosa, sina = _rope_tables(seq, nctx, A_QK, A_QCOLS // A_QK)
    cosb, sinb = _rope_tables(seq, nctx, B_DIM, B_QCOLS // B_DIM)
    ga = _group_matrix(A_QCOLS, A_QK)
    gb = _group_matrix(B_QCOLS, B_DIM)
    in_cols = w_in.shape[2]
    in_pad = (-in_cols) % DT_LANES

    for l in range(depth):
        last = l == depth - 1
        tile0 = ctx_tiles if last else 0
        lam_init = 0.8 - 0.6 * math.exp(-0.3 * l)
        w13r_1, w2_1 = _split_ffn_weights(ffn1_w13[l], ffn1_w2[l], fc)
        w13r_2, w2_2 = _split_ffn_weights(ffn2_w13[l], ffn2_w2[l], fc)
        w_in_p = jnp.pad(w_in[l], ((0, 0), (0, in_pad))).astype(BF16)
        hn = jnp.stack([jnp.tile(qn_a[l], A_QCOLS // A_QK), jnp.tile(kn_a[l], A_QCOLS // A_QK),
                        jnp.tile(qn_b[l], B_QCOLS // B_DIM), jnp.tile(kn_b[l], B_QCOLS // B_DIM)]).astype(F32)
        hn = jnp.zeros((8, A_QCOLS), F32).at[:4].set(hn)

        xs = _ffn_call(xs, mod[l], norm_ffn1[l], w13r_1, w2_1, si=0, tm=tm, ctx_tiles=ctx_tiles, tile0=0, nbatch=nb)

        qa, ka, va, qb, kb, vb, z, xbc_raw, dt = _inproj_call(
            xs, mod[l], norm_mix[l], w_in_p, ga, gb, hn, cosa, sina, cosb, sinb,
            tm=tm, ctx_tiles=ctx_tiles, nbatch=nb)

        qa5 = qa.reshape(nb, t, A_HEADS, 2, A_QK).transpose(0, 2, 3, 1, 4)
        kt5 = ka.reshape(nb, t, A_HEADS, 2, A_QK).transpose(0, 2, 3, 4, 1)
        v4 = va.reshape(nb, t, A_HEADS, A_V).transpose(0, 2, 1, 3)
        ve = jnp.concatenate([v4, jnp.ones_like(v4)], axis=-1)
        lamv = jnp.zeros((8, A_QK), F32).at[:4].set(jnp.stack([lam_q1[l], lam_k1[l], lam_q2[l], lam_k2[l]]))
        oa4 = _attn_a_call(lamv, subln_a[l].reshape(1, A_V), qa5, kt5, ve, tq=tq, tk=tk, ctx=nctx,
                           q_tile0=(nctx // tq if last else 0), lam_init=lam_init)
        oa = oa4.transpose(0, 2, 1, 3).reshape(nb, t, A_HEADS * A_V)

        ob = _attn_b_call(sink_b[l].astype(F32), qb, kb, vb, ctx=nctx, blk0=(nctx // B_BLOCK if last else 0))

        xbc = _conv_call(xbc_raw, conv_w[l], conv_b[l], tm=tm, ctx=nctx)
        par = (jnp.zeros((8, DT_LANES), F32)
               .at[0, :2 * C_HEADS].set(dt_bias[l].reshape(-1))
               .at[1, :2 * C_HEADS].set(-jnp.exp(a_log[l].astype(F32)).reshape(-1)))
        yf, yb = _ssd_call(par, xbc, dt, ctx=nctx)

        dg = (jnp.zeros((8, C_INNER), F32)
              .at[0].set(jnp.repeat(d_skip[l], C_HEAD_DIM)).at[1].set(gnorm_c[l]))
        xs = _outproj_call(xs, mod[l], oa, ob, yf, yb, xbc, z, dg, w_out[l].astype(BF16),
                           tm=tm, ctx_tiles=ctx_tiles, tile0=tile0, nbatch=nb)
        xs = _ffn_call(xs, mod[l], norm_ffn2[l], w13r_2, w2_2, si=6, tm=tm, ctx_tiles=ctx_tiles, tile0=tile0,
                       nbatch=nb)
    return xs
```

```python
import functools
import math

import jax
import jax.numpy as jnp
from jax import lax
from jax.experimental import pallas as pl
from jax.experimental.pallas import tpu as pltpu

F32 = jnp.float32
BF16 = jnp.bfloat16
HIGHEST = lax.Precision.HIGHEST
NEG_INF = float("-inf")

EPS = 1e-6
ROPE_BASE = 10000.0
GRID_W = 64
N_MOD = 9

A_HEADS, A_QK, A_V = 4, 32, 64
A_QCOLS = A_HEADS * 2 * A_QK
A_SCALE = A_QK ** -0.5
LOG2E = math.log2(math.e)
A_SHIFT_ROWS = 16
A_SHIFT_CAP = 48.0
B_HEADS, B_KV_HEADS, B_DIM = 4, 2, 64
B_QCOLS = B_HEADS * B_DIM
B_KCOLS = B_KV_HEADS * B_DIM
B_SCALE = B_DIM ** -0.5
B_BLOCK = 128
C_HEADS, C_HEAD_DIM, C_GROUPS, C_STATE, C_CONV = 8, 64, 2, 64, 5
C_INNER = C_HEADS * C_HEAD_DIM
C_GN = C_GROUPS * C_STATE
C_XBC = C_INNER + 2 * C_GN
C_CHUNK = 128
C_HPG = C_HEADS // C_GROUPS
DT_LANES = 128
HALO = 8

V7X_VMEM_LIMIT = 56 * 1024 * 1024


def _params(sem, vmem=V7X_VMEM_LIMIT):
    return pltpu.CompilerParams(dimension_semantics=sem, vmem_limit_bytes=vmem)


def _rms_mod(x, nw, shift, scale):
    ms = jnp.mean(x * x, axis=-1, keepdims=True)
    y = x * lax.rsqrt(ms + EPS) * nw
    return y * (1.0 + scale) + shift


def _silu(v):
    return v * jax.nn.sigmoid(v)


def _mod_kernel(c_ref, w_ref, b_ref, o_ref):
    sc = _silu(c_ref[...])
    o_ref[...] = jnp.dot(sc, w_ref[...], precision=HIGHEST, preferred_element_type=F32) + b_ref[...]


def _mod_call(cvec, w_mod, b_mod, tn=1152):
    nl, d, n = w_mod.shape
    return pl.pallas_call(
        _mod_kernel,
        grid=(nl, n // tn),
        in_specs=[pl.BlockSpec((8, d), lambda l, j: (0, 0)),
                  pl.BlockSpec((None, d, tn), lambda l, j: (l, 0, j)),
                  pl.BlockSpec((None, 1, tn), lambda l, j: (l, 0, j))],
        out_specs=pl.BlockSpec((None, 8, tn), lambda l, j: (l, 0, j)),
        out_shape=jax.ShapeDtypeStruct((nl, 8, n), F32),
        compiler_params=_params(("parallel", "parallel")),
        name="adaln_mod",
    )(cvec, w_mod, b_mod.reshape(nl, 1, n))


def _ffn_kernel(x_ref, mod_ref, nw_ref, w13_ref, w2_ref, o_ref, a_scr, *, si, fc, nchunks):
    x = x_ref[...]
    m = mod_ref[...]
    h = _rms_mod(x, nw_ref[...], m[si:si + 1], m[si + 1:si + 2]).astype(BF16)
    for c in range(nchunks):
        gu = jnp.dot(h, w13_ref[c], preferred_element_type=F32)
        a_scr[:, c * fc:(c + 1) * fc] = (_silu(gu[:, :fc]) * gu[:, fc:]).astype(BF16)
    y = jnp.dot(a_scr[...], w2_ref[...], preferred_element_type=F32)
    o_ref[...] = x + (0.5 * m[si + 2:si + 3]) * y


def _ffn_call(xs, mod, nw, w13r, w2, *, si, tm, ctx_tiles, tile0, nbatch):
    b, t, d = xs.shape
    nchunks, _, fc2 = w13r.shape
    fc = fc2 // 2
    ntiles = t // tm - tile0
    mod_row = lambda bi, i: (jnp.where(i + tile0 < ctx_tiles, nbatch, bi), 0, 0)
    return pl.pallas_call(
        functools.partial(_ffn_kernel, si=si, fc=fc, nchunks=nchunks),
        grid=(b, ntiles),
        in_specs=[pl.BlockSpec((None, tm, d), lambda bi, i: (bi, i + tile0, 0)),
                  pl.BlockSpec((None, N_MOD, d), mod_row),
                  pl.BlockSpec((1, d), lambda bi, i: (0, 0)),
                  pl.BlockSpec(w13r.shape, lambda bi, i: (0, 0, 0), pipeline_mode=pl.Buffered(1)),
                  pl.BlockSpec(w2.shape, lambda bi, i: (0, 0), pipeline_mode=pl.Buffered(1))],
        out_specs=pl.BlockSpec((None, tm, d), lambda bi, i: (bi, i, 0)),
        out_shape=jax.ShapeDtypeStruct((b, ntiles * tm, d), F32),
        scratch_shapes=[pltpu.VMEM((tm, nchunks * fc), BF16)],
        compiler_params=_params(("parallel", "parallel")),
        name="swiglu_half",
    )(xs, mod, nw.reshape(1, d), w13r, w2)


def _rope(v, cos, sin_signed, quarter):
    n = v.shape[-1]
    lane = lax.broadcasted_iota(jnp.int32, v.shape, 1)
    first = (lane & (2 * quarter - 1)) < quarter
    vr = jnp.where(first, pltpu.roll(v, n - quarter, 1), pltpu.roll(v, quarter, 1))
    return v * cos + vr * sin_signed


def _group_norm(v, gmat, inv_n, w):
    ms = jnp.dot(v * v, gmat, precision=HIGHEST, preferred_element_type=F32) * inv_n
    return v * lax.rsqrt(ms + EPS) * w


def _inproj_kernel(x_ref, mod_ref, nw_ref, w_ref, ga_ref, gb_ref, hn_ref, cosa_ref, sina_ref, cosb_ref, sinb_ref,
                   qa_o, ka_o, va_o, qb_o, kb_o, vb_o, z_o, xbc_o, dt_o):
    m = mod_ref[...]
    h = _rms_mod(x_ref[...], nw_ref[...], m[3:4], m[4:5]).astype(BF16)
    p = jnp.dot(h, w_ref[...], preferred_element_type=F32)
    hn = hn_ref[...]
    ga, gb = ga_ref[...], gb_ref[...]
    cosa, sina, cosb, sinb = cosa_ref[...], sina_ref[...], cosb_ref[...], sinb_ref[...]
    o = 0
    qa = _group_norm(p[:, o:o + A_QCOLS], ga, 1.0 / A_QK, hn[0:1]); o += A_QCOLS
    ka = _group_norm(p[:, o:o + A_QCOLS], ga, 1.0 / A_QK, hn[1:2]); o += A_QCOLS
    qa_o[...] = (_rope(qa, cosa, sina, A_QK // 4) * (A_SCALE * LOG2E)).astype(BF16)
    ka_o[...] = _rope(ka, cosa, sina, A_QK // 4).T.astype(BF16)
    va_o[...] = p[:, o:o + A_HEADS * A_V].astype(BF16); o += A_HEADS * A_V
    qb = _group_norm(p[:, o:o + B_QCOLS], gb, 1.0 / B_DIM, hn[2:3]); o += B_QCOLS
    kb = _group_norm(p[:, o:o + B_KCOLS], gb[:B_KCOLS, :B_KCOLS], 1.0 / B_DIM, hn[3:4, :B_KCOLS]); o += B_KCOLS
    qb_o[...] = (_rope(qb, cosb, sinb, B_DIM // 4) * B_SCALE).astype(BF16)
    kb_o[...] = _rope(kb, cosb[:, :B_KCOLS], sinb[:, :B_KCOLS], B_DIM // 4).astype(BF16)
    vb_o[...] = p[:, o:o + B_KCOLS].astype(BF16); o += B_KCOLS
    z_o[...] = p[:, o:o + C_INNER]; o += C_INNER
    xbc_o[...] = p[:, o:o + C_XBC]; o += C_XBC
    dt_o[...] = p[:, o:o + DT_LANES]


def _inproj_call(xs, mod, nw, w_in_p, ga, gb, hn, cosa, sina, cosb, sinb, *, tm, ctx_tiles, nbatch):
    b, t, d = xs.shape
    ntiles = t // tm
    tok = lambda w: pl.BlockSpec((None, tm, w), lambda bi, i: (bi, i, 0))
    tab = lambda w: pl.BlockSpec((tm, w), lambda bi, i: (i, 0))
    full = lambda a: pl.BlockSpec(a.shape, lambda bi, i: (0,) * a.ndim)
    mod_row = lambda bi, i: (jnp.where(i < ctx_tiles, nbatch, bi), 0, 0)
    widths = (A_QCOLS, A_QCOLS, A_HEADS * A_V, B_QCOLS, B_KCOLS, B_KCOLS, C_INNER, C_XBC, DT_LANES)
    dtypes = (BF16,) * 6 + (F32,) * 3
    return pl.pallas_call(
        _inproj_kernel,
        grid=(b, ntiles),
        in_specs=[tok(d), pl.BlockSpec((None, N_MOD, d), mod_row), pl.BlockSpec((1, d), lambda bi, i: (0, 0)),
                  pl.BlockSpec(w_in_p.shape, lambda bi, i: (0, 0), pipeline_mode=pl.Buffered(1)),
                  full(ga), full(gb), full(hn), tab(A_QCOLS), tab(A_QCOLS), tab(B_QCOLS), tab(B_QCOLS)],
        out_specs=[pl.BlockSpec((None, A_QCOLS, tm), lambda bi, i: (bi, 0, i)) if k == 1 else tok(w)
                   for k, w in enumerate(widths)],
        out_shape=[jax.ShapeDtypeStruct((b, w, t) if k == 1 else (b, t, w), dt)
                   for k, (w, dt) in enumerate(zip(widths, dtypes))],
        compiler_params=_params(("parallel", "parallel")),
        name="in_proj_heads",
    )(xs, mod, nw.reshape(1, d), w_in_p, ga, gb, hn, cosa, sina, cosb, sinb)


def _attn_a_kernel(lamv_ref, subln_ref, q_ref, kt_ref, v_ref, o_ref, ve_scr, kmax_scr, *,
                   tq, tk, ctx, nk, q_tile0, lam_init):
    step = pl.program_id(1)
    t = kt_ref.shape[1]

    @pl.when(step == 0)
    def _():
        ones = jnp.ones((t, A_V), BF16)
        for h in range(A_HEADS):
            ve_scr[h] = jnp.concatenate([v_ref[:, h * A_V:(h + 1) * A_V], ones], axis=1)
        for hc in range(2 * A_HEADS):
            kk = kt_ref[hc * A_QK:(hc + 1) * A_QK, :].astype(F32)
            ksq = jnp.sum(kk * kk, axis=0, keepdims=True)
            kmax = jnp.sqrt(jnp.max(ksq, axis=1, keepdims=True))
            kmax_scr[hc:hc + 1, :] = jnp.broadcast_to(kmax, (1, kmax_scr.shape[1]))

    lv = lamv_ref[...]
    lam = (jnp.exp(jnp.sum(lv[0:1] * lv[1:2], axis=1, keepdims=True))
           - jnp.exp(jnp.sum(lv[2:3] * lv[3:4], axis=1, keepdims=True)) + lam_init)
    q = q_ref[...].astype(F32)
    qcs = [q[:, hc * A_QK:(hc + 1) * A_QK] for hc in range(2 * A_HEADS)]
    shifts = [jnp.sqrt(jnp.sum(qc * qc, axis=1, keepdims=True)) * kmax_scr[hc:hc + 1, 0:1]
              for hc, qc in enumerate(qcs)]
    shift_max = jnp.max(functools.reduce(jnp.maximum, shifts))

    def q_ext(hc, shift):
        return jnp.concatenate([qcs[hc], jnp.broadcast_to(-shift, (tq, A_SHIFT_ROWS))], axis=1).astype(BF16)

    def k_ext(hc, start, size):
        shift_row = (lax.broadcasted_iota(jnp.int32, (A_SHIFT_ROWS, size), 0) == 0).astype(BF16)
        return jnp.concatenate([kt_ref[hc * A_QK:(hc + 1) * A_QK, pl.ds(start, size)], shift_row], axis=0)

    def scores(qx, hc, start, size):
        return jnp.dot(qx, k_ext(hc, start, size), preferred_element_type=F32)

    def finish(accs):
        outs = []
        for acc in accs:
            a0, a1 = acc[:tq], acc[tq:]
            o = a0[:, :A_V] / a0[:, A_V:] - lam * (a1[:, :A_V] / a1[:, A_V:])
            ms = jnp.mean(o * o, axis=1, keepdims=True)
            outs.append(o * lax.rsqrt(ms + EPS) * subln_ref[...] * (1.0 - lam_init))
        o_ref[...] = jnp.concatenate(outs, axis=1)

    def attend(shifted):
        accs = []
        for h in range(A_HEADS):
            zero = jnp.zeros((tq, 1), F32)
            qx = [q_ext(2 * h + c, shifts[2 * h + c] if shifted else zero) for c in range(2)]

            def body(j, carry, h=h, qx=qx):
                start = pl.multiple_of(j * tk, tk)
                s = [scores(qx[c], 2 * h + c, start, tk) for c in range(2)]
                ve = ve_scr[h, pl.ds(start, tk), :]
                if shifted:
                    p = jnp.concatenate([jnp.exp2(s[0]), jnp.exp2(s[1])], axis=0).astype(BF16)
                    return carry + jnp.dot(p, ve, preferred_element_type=F32)
                m_run, acc = carry
                s = jnp.concatenate(s, axis=0)
                m_new = jnp.maximum(m_run, jnp.max(s, axis=1, keepdims=True))
                p = jnp.exp2(s - m_new).astype(BF16)
                return m_new, jnp.exp2(m_run - m_new) * acc + jnp.dot(p, ve, preferred_element_type=F32)

            acc0 = jnp.zeros((2 * tq, 2 * A_V), F32)
            if shifted:
                accs.append(lax.fori_loop(0, nk, body, acc0))
            else:
                accs.append(lax.fori_loop(0, nk, body, (jnp.full((2 * tq, 1), NEG_INF, F32), acc0))[1])
        finish(accs)

    def attend_ctx():
        accs = []
        for h in range(A_HEADS):
            zero = jnp.zeros((tq, 1), F32)
            s = jnp.concatenate([scores(q_ext(2 * h + c, zero), 2 * h + c, 0, ctx) for c in range(2)], axis=0)
            p = jnp.exp2(s - jnp.max(s, axis=1, keepdims=True)).astype(BF16)
            accs.append(jnp.dot(p, ve_scr[h, 0:ctx, :], preferred_element_type=F32))
        finish(accs)

    safe = shift_max < A_SHIFT_CAP
    if q_tile0 * tq < ctx:
        is_ctx = (step + q_tile0) * tq < ctx
        pl.when(is_ctx)(attend_ctx)
        pl.when(jnp.logical_and(jnp.logical_not(is_ctx), safe))(lambda: attend(True))
        pl.when(jnp.logical_and(jnp.logical_not(is_ctx), jnp.logical_not(safe)))(lambda: attend(False))
    else:
        pl.when(safe)(lambda: attend(True))
        pl.when(jnp.logical_not(safe))(lambda: attend(False))


def _attn_a_call(lamv, subln, qa, ka_t, va, *, tq, tk, ctx, q_tile0, lam_init):
    b, t, w = qa.shape
    nq = t // tq - q_tile0
    return pl.pallas_call(
        functools.partial(_attn_a_kernel, tq=tq, tk=tk, ctx=ctx, nk=t // tk, q_tile0=q_tile0, lam_init=lam_init),
        grid=(b, nq),
        in_specs=[pl.BlockSpec(lamv.shape, lambda bi, qi: (0, 0)),
                  pl.BlockSpec(subln.shape, lambda bi, qi: (0, 0)),
                  pl.BlockSpec((None, tq, w), lambda bi, qi: (bi, qi + q_tile0, 0)),
                  pl.BlockSpec((None, w, t), lambda bi, qi: (bi, 0, 0)),
                  pl.BlockSpec((None, t, w), lambda bi, qi: (bi, 0, 0))],
        out_specs=pl.BlockSpec((None, tq, w), lambda bi, qi: (bi, qi + q_tile0, 0)),
        out_shape=jax.ShapeDtypeStruct((b, t, w), F32),
        scratch_shapes=[pltpu.VMEM((A_HEADS, t, 2 * A_V), BF16), pltpu.VMEM((2 * A_HEADS, 128), F32)],
        compiler_params=_params(("parallel", "arbitrary")),
        name="diff_attention",
    )(lamv, subln, qa, ka_t, va)


def _attn_b_kernel(sink_ref, q_ref, kp_ref, kc_ref, kn_ref, kx_ref, vp_ref, vc_ref, vn_ref, vx_ref, o_ref, *,
                   blk0, ctx_blocks, nblk):
    n = pl.program_id(1) + blk0
    q = q_ref[...]
    k_all = jnp.concatenate([kp_ref[...], kc_ref[...], kn_ref[...], kx_ref[...]], axis=0)
    v_all = jnp.concatenate([vp_ref[...], vc_ref[...], vn_ref[...], vx_ref[...]], axis=0)
    nkeys = k_all.shape[0]
    i = lax.broadcasted_iota(jnp.int32, (B_BLOCK, nkeys), 0)
    j = lax.broadcasted_iota(jnp.int32, (B_BLOCK, nkeys), 1)
    lat = n >= ctx_blocks
    pen_prev = jnp.where(n - 1 >= ctx_blocks, 0.0, NEG_INF)
    pen_cur = jnp.where(lat, 0.0, NEG_INF)
    pen_next = jnp.where(jnp.logical_and(lat, n + 1 <= nblk - 1), 0.0, NEG_INF)
    bias = jnp.where(j < B_BLOCK, jnp.where(j >= i, pen_prev, NEG_INF),
                     jnp.where(j < 2 * B_BLOCK, pen_cur,
                               jnp.where(j < 3 * B_BLOCK, jnp.where(j - 2 * B_BLOCK <= i, pen_next, NEG_INF), 0.0)))
    outs = []
    for hq in range(B_HEADS):
        g = hq // (B_HEADS // B_KV_HEADS)
        s = lax.dot_general(q[:, hq * B_DIM:(hq + 1) * B_DIM], k_all[:, g * B_DIM:(g + 1) * B_DIM],
                            (((1,), (1,)), ((), ())), preferred_element_type=F32)
        s = s + bias
        sk = sink_ref[hq]
        m = jnp.maximum(jnp.max(s, axis=1, keepdims=True), sk)
        p = jnp.exp(s - m)
        den = jnp.sum(p, axis=1, keepdims=True) + jnp.exp(sk - m)
        pv = jnp.dot(p.astype(BF16), v_all[:, g * B_DIM:(g + 1) * B_DIM], preferred_element_type=F32)
        outs.append(pv / den)
    o_ref[...] = jnp.concatenate(outs, axis=1)


def _attn_b_call(sink, qb, kb, vb, *, ctx, blk0):
    b, t, _ = qb.shape
    nblk = t // B_BLOCK
    ctx_blocks = ctx // B_BLOCK
    cur = lambda bi, i: (bi, i + blk0, 0)
    prev = lambda bi, i: (bi, jnp.maximum(i + blk0 - 1, 0), 0)
    nxt = lambda bi, i: (bi, jnp.minimum(i + blk0 + 1, nblk - 1), 0)
    kv = lambda im: pl.BlockSpec((None, B_BLOCK, B_KCOLS), im)
    cx = pl.BlockSpec((None, ctx, B_KCOLS), lambda bi, i: (bi, 0, 0))
    return pl.pallas_call(
        functools.partial(_attn_b_kernel, blk0=blk0, ctx_blocks=ctx_blocks, nblk=nblk),
        grid=(b, nblk - blk0),
        in_specs=[pl.BlockSpec(memory_space=pltpu.SMEM),
                  pl.BlockSpec((None, B_BLOCK, B_QCOLS), cur),
                  kv(prev), kv(cur), kv(nxt), cx, kv(prev), kv(cur), kv(nxt), cx],
        out_specs=pl.BlockSpec((None, B_BLOCK, B_QCOLS), cur),
        out_shape=jax.ShapeDtypeStruct((b, t, B_QCOLS), F32),
        compiler_params=_params(("parallel", "parallel")),
        name="window_attention",
    )(sink, qb, kb, kb, kb, kb, vb, vb, vb, vb)


def _conv_kernel(u_ref, up_ref, un_ref, w_ref, b_ref, o_ref, *, tm, ctx_tiles, ntiles):
    i = pl.program_id(1)
    u = u_ref[...]
    has_prev = jnp.logical_and(i != 0, i != ctx_tiles)
    has_next = jnp.logical_and(i != ctx_tiles - 1, i != ntiles - 1)
    up = jnp.where(has_prev, up_ref[...], 0.0)
    un = jnp.where(has_next, un_ref[...], 0.0)
    full = jnp.concatenate([up, u, un], axis=0)
    w = w_ref[...]
    acc = b_ref[...] + w[0:1] * full[HALO - 2:HALO - 2 + tm]
    for k in range(1, C_CONV):
        acc = acc + w[k:k + 1] * full[HALO - 2 + k:HALO - 2 + k + tm]
    o_ref[...] = _silu(acc)


def _conv_call(xbc_raw, conv_w, conv_b, *, tm, ctx):
    b, t, ch = xbc_raw.shape
    ntiles = t // tm
    hb = tm // HALO
    nh = t // HALO
    w8 = jnp.zeros((8, ch), F32).at[:C_CONV].set(conv_w)
    return pl.pallas_call(
        functools.partial(_conv_kernel, tm=tm, ctx_tiles=ctx // tm, ntiles=ntiles),
        grid=(b, ntiles),
        in_specs=[pl.BlockSpec((None, tm, ch), lambda bi, i: (bi, i, 0)),
                  pl.BlockSpec((None, HALO, ch), lambda bi, i: (bi, jnp.maximum(i * hb - 1, 0), 0)),
                  pl.BlockSpec((None, HALO, ch), lambda bi, i: (bi, jnp.minimum((i + 1) * hb, nh - 1), 0)),
                  pl.BlockSpec((8, ch), lambda bi, i: (0, 0)),
                  pl.BlockSpec((1, ch), lambda bi, i: (0, 0))],
        out_specs=pl.BlockSpec((None, tm, ch), lambda bi, i: (bi, i, 0)),
        out_shape=jax.ShapeDtypeStruct((b, t, ch), F32),
        compiler_params=_params(("parallel", "parallel")),
        name="ssm_conv",
    )(xbc_raw, xbc_raw, xbc_raw, w8, conv_b.reshape(1, ch))


def _softplus(v):
    return jnp.maximum(v, 0.0) + jnp.log1p(jnp.exp(-jnp.abs(v)))


def _ssd_direction(x_ref, dt_ref, par, y_ref, h_scr, *, lane0, backward):
    xbc = x_ref[...]
    xs = xbc[:, :C_INNER]
    bm = xbc[:, C_INNER:C_INNER + C_GN]
    cm = xbc[:, C_INNER + C_GN:].astype(BF16)
    dt = _softplus(dt_ref[...] + par[0:1])
    a = dt * par[1:2]
    li = lax.broadcasted_iota(jnp.int32, (C_CHUNK, C_CHUNK), 0)
    si = lax.broadcasted_iota(jnp.int32, (C_CHUNK, C_CHUNK), 1)
    tri = (si <= li).astype(F32)
    cs = jnp.dot(tri, a, precision=HIGHEST, preferred_element_type=F32)
    tot = cs[C_CHUNK - 1:C_CHUNK, :]
    key = cs - a if backward else cs
    key_t = key.T
    bm_t = bm.T
    bm16 = bm.astype(BF16)
    mask = (si >= li) if backward else (si <= li)
    ys = []
    for g in range(C_GROUPS):
        gs = slice(g * C_STATE, (g + 1) * C_STATE)
        cb = lax.dot_general(cm[:, gs], bm16[:, gs], (((1,), (1,)), ((), ())), preferred_element_type=F32)
        for r in range(C_HPG):
            h = g * C_HPG + r
            hl = lane0 + h
            col = key[:, hl:hl + 1]
            row = key_t[hl:hl + 1, :]
            tot_h = tot[:, hl:hl + 1]
            d = (row - col) if backward else (col - row)
            decay = jnp.exp(jnp.where(mask, d, NEG_INF))
            xdt = (xs[:, h * C_HEAD_DIM:(h + 1) * C_HEAD_DIM] * dt[:, hl:hl + 1]).astype(BF16)
            y_diag = jnp.dot((cb * decay).astype(BF16), xdt, preferred_element_type=F32)
            h_prev = h_scr[h]
            off_scale = jnp.exp(tot_h - col) if backward else jnp.exp(col)
            y_off = jnp.dot(cm[:, gs], h_prev.astype(BF16), preferred_element_type=F32) * off_scale
            ys.append(y_diag + y_off)
            w_row = jnp.exp(row) if backward else jnp.exp(tot_h - row)
            bw_t = (bm_t[gs, :] * w_row).astype(BF16)
            h_scr[h] = jnp.exp(tot_h) * h_prev + jnp.dot(bw_t, xdt, preferred_element_type=F32)
    y_ref[...] = jnp.concatenate(ys, axis=1)


def _ssd_kernel(par_ref, xf_ref, dtf_ref, xb_ref, dtb_ref, yf_ref, yb_ref, hf_scr, hb_scr):
    @pl.when(pl.program_id(1) == 0)
    def _():
        hf_scr[...] = jnp.zeros(hf_scr.shape, F32)
        hb_scr[...] = jnp.zeros(hb_scr.shape, F32)

    par = par_ref[...]
    _ssd_direction(xf_ref, dtf_ref, par, yf_ref, hf_scr, lane0=0, backward=False)
    _ssd_direction(xb_ref, dtb_ref, par, yb_ref, hb_scr, lane0=C_HEADS, backward=True)


def _ssd_call(par, xbc, dt, *, ctx):
    b, t, ch = xbc.shape
    nc = t // C_CHUNK
    ncc = ctx // C_CHUNK
    fwd = lambda bi, j: (bi, j, 0)
    bwd = lambda bi, j: (bi, jnp.where(j < ncc, ncc - 1 - j, nc - 1 + ncc - j), 0)
    return pl.pallas_call(
        _ssd_kernel,
        grid=(b, nc),
        in_specs=[pl.BlockSpec((8, DT_LANES), lambda bi, j: (0, 0)),
                  pl.BlockSpec((None, C_CHUNK, ch), fwd), pl.BlockSpec((None, C_CHUNK, DT_LANES), fwd),
                  pl.BlockSpec((None, C_CHUNK, ch), bwd), pl.BlockSpec((None, C_CHUNK, DT_LANES), bwd)],
        out_specs=[pl.BlockSpec((None, C_CHUNK, C_INNER), fwd), pl.BlockSpec((None, C_CHUNK, C_INNER), bwd)],
        out_shape=[jax.ShapeDtypeStruct((b, t, C_INNER), F32)] * 2,
        scratch_shapes=[pltpu.VMEM((C_HEADS, C_STATE, C_HEAD_DIM), F32)] * 2,
        compiler_params=_params(("parallel", "arbitrary")),
        name="ssd_scan",
    )(par, xbc, dt, xbc, dt)


def _outproj_kernel(x_ref, mod_ref, oa_ref, ob_ref, yf_ref, yb_ref, xs_ref, z_ref, dg_ref, w_ref, o_ref):
    m = mod_ref[...]
    dg = dg_ref[...]
    y = (yf_ref[...] + yb_ref[...] + dg[0:1] * xs_ref[...]) * _silu(z_ref[...])
    gw = C_INNER // C_GROUPS
    oc = []
    for g in range(C_GROUPS):
        yg = y[:, g * gw:(g + 1) * gw]
        ms = jnp.mean(yg * yg, axis=1, keepdims=True)
        oc.append(yg * lax.rsqrt(ms + EPS) * dg[1:2, g * gw:(g + 1) * gw])
    mix = jnp.concatenate([oa_ref[...], ob_ref[...]] + oc, axis=1).astype(BF16)
    o_ref[...] = x_ref[...] + m[5:6] * jnp.dot(mix, w_ref[...], preferred_element_type=F32)


def _outproj_call(xs, mod, oa, ob, yf, yb, xbc, z, dg, w_out, *, tm, ctx_tiles, tile0, nbatch):
    b, t, d = xs.shape
    ntiles = t // tm - tile0
    tok = lambda w: pl.BlockSpec((None, tm, w), lambda bi, i: (bi, i + tile0, 0))
    mod_row = lambda bi, i: (jnp.where(i + tile0 < ctx_tiles, nbatch, bi), 0, 0)
    return pl.pallas_call(
        _outproj_kernel,
        grid=(b, ntiles),
        in_specs=[tok(d), pl.BlockSpec((None, N_MOD, d), mod_row),
                  tok(A_HEADS * A_V), tok(B_QCOLS), tok(C_INNER), tok(C_INNER), tok(C_INNER), tok(C_INNER),
                  pl.BlockSpec(dg.shape, lambda bi, i: (0, 0)),
                  pl.BlockSpec(w_out.shape, lambda bi, i: (0, 0), pipeline_mode=pl.Buffered(1))],
        out_specs=pl.BlockSpec((None, tm, d), lambda bi, i: (bi, i + tile0, 0)),
        out_shape=jax.ShapeDtypeStruct((b, t, d), F32),
        compiler_params=_params(("parallel", "parallel")),
        name="mix_out_proj",
    )(xs, mod, oa, ob, yf, yb, xbc, z, dg, w_out)


def _rope_tables(seq, ctx, dim, reps):
    rows = seq // GRID_W
    row = jnp.repeat(jnp.arange(rows, dtype=F32), GRID_W)
    col = jnp.tile(jnp.arange(GRID_W, dtype=F32), rows)
    quarter = dim // 4
    inv_freq = ROPE_BASE ** (-jnp.arange(quarter, dtype=F32) / quarter)
    ar = row[:, None] * inv_freq
    ac = col[:, None] * inv_freq
    ang = jnp.concatenate([ar, ar, ac, ac], axis=-1)
    sign = jnp.where((jnp.arange(dim) % (dim // 2)) < quarter, -1.0, 1.0).astype(F32)
    cos = jnp.concatenate([jnp.ones((ctx, dim), F32), jnp.cos(ang)], axis=0)
    sin = jnp.concatenate([jnp.zeros((ctx, dim), F32), jnp.sin(ang) * sign], axis=0)
    return jnp.tile(cos, (1, reps)), jnp.tile(sin, (1, reps))


def _group_matrix(n, group):
    idx = jnp.arange(n) // group
    return (idx[:, None] == idx[None, :]).astype(F32)


def _split_ffn_weights(w13, w2, fc):
    d, two_ff = w13.shape
    ff = two_ff // 2
    nch = ff // fc
    w1 = w13[:, :ff].reshape(d, nch, fc)
    w3 = w13[:, ff:].reshape(d, nch, fc)
    w13r = jnp.concatenate([w1, w3], axis=2).transpose(1, 0, 2).astype(BF16)
    return w13r, w2.astype(BF16)


def kernel(x, c, ctx, c_ctx, w_mod, b_mod, norm_ffn1, ffn1_w13, ffn1_w2, norm_mix, w_in, w_out, qn_a, kn_a, lam_q1, lam_k1, lam_q2, lam_k2, subln_a, qn_b, kn_b, sink_b, conv_w, conv_b, dt_bias, a_log, d_skip, gnorm_c, norm_ffn2, ffn2_w13, ffn2_w2):
    nb, seq, d = x.shape
    nctx = ctx.shape[1]
    t = nctx + seq
    depth = w_mod.shape[0]
    tm = 256
    fc = 256
    tq, tk = 256, (768 if t % 768 == 0 and t > 768 else 256)
    assert nctx % tm == 0 and t % tm == 0 and nctx % C_CHUNK == 0 and seq % GRID_W == 0 and nb < 8
    ctx_tiles = nctx // tm

    xs = jnp.concatenate([ctx, x], axis=1)
    cvec = jnp.zeros((8, d), F32).at[:nb].set(c).at[nb].set(c_ctx)
    mod = _mod_call(cvec, w_mod, b_mod).reshape(depth, 8, N_MOD, d)

    cosa, sina = _rope_tables(seq, nctx, A_QK, A_QCOLS // A_QK)
    cosb, sinb = _rope_tables(seq, nctx, B_DIM, B_QCOLS // B_DIM)
    ga = _group_matrix(A_QCOLS, A_QK)
    gb = _group_matrix(B_QCOLS, B_DIM)
    in_cols = w_in.shape[2]
    in_pad = (-in_cols) % DT_LANES

    for l in range(depth):
        last = l == depth - 1
        tile0 = ctx_tiles if last else 0
        lam_init = 0.8 - 0.6 * math.exp(-0.3 * l)
        w13r_1, w2_1 = _split_ffn_weights(ffn1_w13[l], ffn1_w2[l], fc)
        w13r_2, w2_2 = _split_ffn_weights(ffn2_w13[l], ffn2_w2[l], fc)
        w_in_p = jnp.pad(w_in[l], ((0, 0), (0, in_pad))).astype(BF16)
        hn = jnp.stack([jnp.tile(qn_a[l], A_QCOLS // A_QK), jnp.tile(kn_a[l], A_QCOLS // A_QK),
                        jnp.tile(qn_b[l], B_QCOLS // B_DIM), jnp.tile(kn_b[l], B_QCOLS // B_DIM)]).astype(F32)
        hn = jnp.zeros((8, A_QCOLS), F32).at[:4].set(hn)

        xs = _ffn_call(xs, mod[l], norm_ffn1[l], w13r_1, w2_1, si=0, tm=tm, ctx_tiles=ctx_tiles, tile0=0, nbatch=nb)

        qa, ka_t, va, qb, kb, vb, z, xbc_raw, dt = _inproj_call(
            xs, mod[l], norm_mix[l], w_in_p, ga, gb, hn, cosa, sina, cosb, sinb,
            tm=tm, ctx_tiles=ctx_tiles, nbatch=nb)

        lamv = jnp.zeros((8, A_QK), F32).at[:4].set(jnp.stack([lam_q1[l], lam_k1[l], lam_q2[l], lam_k2[l]]))
        oa = _attn_a_call(lamv, subln_a[l].reshape(1, A_V), qa, ka_t, va, tq=tq, tk=tk, ctx=nctx,
                          q_tile0=(nctx // tq if last else 0), lam_init=lam_init)

        ob = _attn_b_call(sink_b[l].astype(F32), qb, kb, vb, ctx=nctx, blk0=(nctx // B_BLOCK if last else 0))

        xbc = _conv_call(xbc_raw, conv_w[l], conv_b[l], tm=tm, ctx=nctx)
        par = (jnp.zeros((8, DT_LANES), F32)
               .at[0, :2 * C_HEADS].set(dt_bias[l].reshape(-1))
               .at[1, :2 * C_HEADS].set(-jnp.exp(a_log[l].astype(F32)).reshape(-1)))
        yf, yb = _ssd_call(par, xbc, dt, ctx=nctx)

        dg = (jnp.zeros((8, C_INNER), F32)
              .at[0].set(jnp.repeat(d_skip[l], C_HEAD_DIM)).at[1].set(gnorm_c[l]))
        xs = _outproj_call(xs, mod[l], oa, ob, yf, yb, xbc, z, dg, w_out[l].astype(BF16),
                           tm=tm, ctx_tiles=ctx_tiles, tile0=tile0, nbatch=nb)
        xs = _ffn_call(xs, mod[l], norm_ffn2[l], w13r_2, w2_2, si=6, tm=tm, ctx_tiles=ctx_tiles, tile0=tile0,
                       nbatch=nb)
    return xs
```

```python
import functools
import math

import jax
import jax.numpy as jnp
from jax import lax
from jax.experimental import pallas as pl
from jax.experimental.pallas import tpu as pltpu

F32 = jnp.float32
BF16 = jnp.bfloat16
HIGHEST = lax.Precision.HIGHEST
NEG_INF = float("-inf")

EPS = 1e-6
ROPE_BASE = 10000.0
GRID_W = 64
ROPE_LANES = 128
N_MOD = 9

A_HEADS, A_QK, A_V = 4, 32, 64
A_QCOLS = A_HEADS * 2 * A_QK
A_SCALE = A_QK ** -0.5
LOG2E = math.log2(math.e)
A_SHIFT_ROWS = 16
A_SHIFT_CAP = 48.0
B_HEADS, B_KV_HEADS, B_DIM = 4, 2, 64
B_QCOLS = B_HEADS * B_DIM
B_KCOLS = B_KV_HEADS * B_DIM
B_SCALE = B_DIM ** -0.5
B_BLOCK = 128
C_HEADS, C_HEAD_DIM, C_GROUPS, C_STATE, C_CONV = 8, 64, 2, 64, 5
C_INNER = C_HEADS * C_HEAD_DIM
C_GN = C_GROUPS * C_STATE
C_XBC = C_INNER + 2 * C_GN
C_CHUNK = 128
C_HPG = C_HEADS // C_GROUPS
DT_LANES = 128
HALO = 8

V7X_VMEM_LIMIT = 56 * 1024 * 1024


def _params(sem, vmem=V7X_VMEM_LIMIT):
    return pltpu.CompilerParams(dimension_semantics=sem, vmem_limit_bytes=vmem)


def _rms_mod(x, nw, shift, scale):
    ms = jnp.mean(x * x, axis=-1, keepdims=True)
    y = x * lax.rsqrt(ms + EPS) * nw
    return y * (1.0 + scale) + shift


def _silu(v):
    return v * jax.nn.sigmoid(v)


def _mod_kernel(c_ref, w_ref, b_ref, o_ref):
    sc = _silu(c_ref[...])
    o_ref[...] = jnp.dot(sc, w_ref[...], precision=HIGHEST, preferred_element_type=F32) + b_ref[...]


def _mod_call(cvec, w_mod, b_mod, tn=1152):
    nl, d, n = w_mod.shape
    return pl.pallas_call(
        _mod_kernel,
        grid=(nl, n // tn),
        in_specs=[pl.BlockSpec((8, d), lambda l, j: (0, 0)),
                  pl.BlockSpec((None, d, tn), lambda l, j: (l, 0, j)),
                  pl.BlockSpec((None, 1, tn), lambda l, j: (l, 0, j))],
        out_specs=pl.BlockSpec((None, 8, tn), lambda l, j: (l, 0, j)),
        out_shape=jax.ShapeDtypeStruct((nl, 8, n), F32),
        compiler_params=_params(("parallel", "parallel")),
        name="adaln_mod",
    )(cvec, w_mod, b_mod.reshape(nl, 1, n))


def _ffn_kernel(*refs, si, fc, ff, ctx_tiles, split_input):
    if split_input:
        c_ref, x_ref, mod_ref, nw_ref, w13_ref, w2_ref, o_ref, a_scr = refs
        x = jnp.where(pl.program_id(1) < ctx_tiles, c_ref[...], x_ref[...])
    else:
        x_ref, mod_ref, nw_ref, w13_ref, w2_ref, o_ref, a_scr = refs
        x = x_ref[...]
    m = mod_ref[...]
    h = _rms_mod(x, nw_ref[...], m[si:si + 1], m[si + 1:si + 2]).astype(BF16)
    for c in range(ff // fc):
        g = jnp.dot(h, w13_ref[:, c * fc:(c + 1) * fc], preferred_element_type=F32)
        u = jnp.dot(h, w13_ref[:, ff + c * fc:ff + (c + 1) * fc], preferred_element_type=F32)
        a_scr[:, c * fc:(c + 1) * fc] = (_silu(g) * u).astype(BF16)
    y = jnp.dot(a_scr[...], w2_ref[...], preferred_element_type=F32)
    o_ref[...] = x + (0.5 * m[si + 2:si + 3]) * y


def _ffn_call(xs, mod, nw, w13, w2, *, si, tm, fc, ctx_tiles, nbatch, ctx_in=None):
    b, t, d = xs.shape
    ff = w2.shape[0]
    split = ctx_in is not None
    ntiles = t // tm + (ctx_tiles if split else 0)
    mod_row = lambda bi, i: (jnp.where(i < ctx_tiles, nbatch, bi), 0, 0)
    if split:
        tok_specs = [pl.BlockSpec((None, tm, d), lambda bi, i: (bi, jnp.minimum(i, ctx_tiles - 1), 0)),
                     pl.BlockSpec((None, tm, d), lambda bi, i: (bi, jnp.maximum(i - ctx_tiles, 0), 0))]
        toks = (ctx_in, xs)
    else:
        tok_specs = [pl.BlockSpec((None, tm, d), lambda bi, i: (bi, i, 0))]
        toks = (xs,)
    return pl.pallas_call(
        functools.partial(_ffn_kernel, si=si, fc=fc, ff=ff, ctx_tiles=ctx_tiles, split_input=split),
        grid=(b, ntiles),
        in_specs=tok_specs + [pl.BlockSpec((None, N_MOD, d), mod_row),
                              pl.BlockSpec((1, d), lambda bi, i: (0, 0)),
                              pl.BlockSpec(w13.shape, lambda bi, i: (0, 0), pipeline_mode=pl.Buffered(1)),
                              pl.BlockSpec(w2.shape, lambda bi, i: (0, 0), pipeline_mode=pl.Buffered(1))],
        out_specs=pl.BlockSpec((None, tm, d), lambda bi, i: (bi, i, 0)),
        out_shape=jax.ShapeDtypeStruct((b, ntiles * tm, d), F32),
        scratch_shapes=[pltpu.VMEM((tm, ff), BF16)],
        compiler_params=_params(("parallel", "parallel")),
        name="swiglu_half",
    )(*toks, mod, nw.reshape(1, d), w13, w2)


def _rope(v, cos, sin_signed, quarter):
    n = v.shape[-1]
    lane = lax.broadcasted_iota(jnp.int32, v.shape, 1)
    first = (lane & (2 * quarter - 1)) < quarter
    vr = jnp.where(first, pltpu.roll(v, n - quarter, 1), pltpu.roll(v, quarter, 1))
    return v * cos + vr * sin_signed


def _group_norm(v, gmat, inv_n, w):
    sq = v * v
    hi = sq.astype(BF16)
    lo = (sq - hi.astype(F32)).astype(BF16)
    ms = (jnp.dot(hi, gmat, preferred_element_type=F32) + jnp.dot(lo, gmat, preferred_element_type=F32)) * inv_n
    return v * lax.rsqrt(ms + EPS) * w


def _inproj_kernel(x_ref, mod_ref, nw_ref, w_ref, ga_ref, gb_ref, hn_ref, cosa_ref, sina_ref, cosb_ref, sinb_ref,
                   qa_o, ka_o, va_o, qb_o, kb_o, vb_o, z_o, xbc_o, dt_o):
    m = mod_ref[...]
    h = _rms_mod(x_ref[...], nw_ref[...], m[3:4], m[4:5]).astype(BF16)
    p = jnp.dot(h, w_ref[...], preferred_element_type=F32)
    hn = hn_ref[...]
    ga, gb = ga_ref[...], gb_ref[...]
    twice = lambda a: jnp.concatenate([a, a], axis=1)
    cosa, sina, cosb, sinb = twice(cosa_ref[...]), twice(sina_ref[...]), twice(cosb_ref[...]), twice(sinb_ref[...])
    o = 0
    qa = _group_norm(p[:, o:o + A_QCOLS], ga, 1.0 / A_QK, hn[0:1]); o += A_QCOLS
    ka = _group_norm(p[:, o:o + A_QCOLS], ga, 1.0 / A_QK, hn[1:2]); o += A_QCOLS
    qa_o[...] = (_rope(qa, cosa, sina, A_QK // 4) * (A_SCALE * LOG2E)).astype(BF16)
    ka_o[...] = _rope(ka, cosa, sina, A_QK // 4).T.astype(BF16)
    va_o[...] = p[:, o:o + A_HEADS * A_V].astype(BF16); o += A_HEADS * A_V
    qb = _group_norm(p[:, o:o + B_QCOLS], gb, 1.0 / B_DIM, hn[2:3]); o += B_QCOLS
    kb = _group_norm(p[:, o:o + B_KCOLS], gb[:B_KCOLS, :B_KCOLS], 1.0 / B_DIM, hn[3:4, :B_KCOLS]); o += B_KCOLS
    qb_o[...] = (_rope(qb, cosb, sinb, B_DIM // 4) * B_SCALE).astype(BF16)
    kb_o[...] = _rope(kb, cosb[:, :B_KCOLS], sinb[:, :B_KCOLS], B_DIM // 4).astype(BF16)
    vb_o[...] = p[:, o:o + B_KCOLS].astype(BF16); o += B_KCOLS
    z_o[...] = p[:, o:o + C_INNER]; o += C_INNER
    xbc_o[...] = p[:, o:o + C_XBC]; o += C_XBC
    dt_o[...] = p[:, o:o + DT_LANES]


def _inproj_call(xs, mod, nw, w_in_p, ga, gb, hn, cosa, sina, cosb, sinb, *, tm, ctx_tiles, nbatch):
    b, t, d = xs.shape
    ntiles = t // tm
    tok = lambda w: pl.BlockSpec((None, tm, w), lambda bi, i: (bi, i, 0))
    tab = lambda w: pl.BlockSpec((tm, w), lambda bi, i: (i, 0))
    full = lambda a: pl.BlockSpec(a.shape, lambda bi, i: (0,) * a.ndim)
    mod_row = lambda bi, i: (jnp.where(i < ctx_tiles, nbatch, bi), 0, 0)
    widths = (A_QCOLS, A_QCOLS, A_HEADS * A_V, B_QCOLS, B_KCOLS, B_KCOLS, C_INNER, C_XBC, DT_LANES)
    dtypes = (BF16,) * 6 + (F32,) * 3
    return pl.pallas_call(
        _inproj_kernel,
        grid=(b, ntiles),
        in_specs=[tok(d), pl.BlockSpec((None, N_MOD, d), mod_row), pl.BlockSpec((1, d), lambda bi, i: (0, 0)),
                  pl.BlockSpec(w_in_p.shape, lambda bi, i: (0, 0), pipeline_mode=pl.Buffered(1)),
                  full(ga), full(gb), full(hn)] + [tab(ROPE_LANES)] * 4,
        out_specs=[pl.BlockSpec((None, A_QCOLS, tm), lambda bi, i: (bi, 0, i)) if k == 1 else tok(w)
                   for k, w in enumerate(widths)],
        out_shape=[jax.ShapeDtypeStruct((b, w, t) if k == 1 else (b, t, w), dt)
                   for k, (w, dt) in enumerate(zip(widths, dtypes))],
        compiler_params=_params(("parallel", "parallel")),
        name="in_proj_heads",
    )(xs, mod, nw.reshape(1, d), w_in_p, ga, gb, hn, cosa, sina, cosb, sinb)


def _attn_a_kernel(lamv_ref, subln_ref, q_ref, kt_ref, v_ref, o_ref, ve_scr, kmax_scr, *,
                   tq, tk, ctx, nk, q_tile0, lam_init):
    step = pl.program_id(1)
    t = kt_ref.shape[1]

    @pl.when(step == 0)
    def _():
        ones = jnp.ones((t, A_V), BF16)
        for h in range(A_HEADS):
            ve_scr[h] = jnp.concatenate([v_ref[:, h * A_V:(h + 1) * A_V], ones], axis=1)
        for hc in range(2 * A_HEADS):
            kk = kt_ref[hc * A_QK:(hc + 1) * A_QK, :].astype(F32)
            ksq = jnp.sum(kk * kk, axis=0, keepdims=True)
            kmax = jnp.sqrt(jnp.max(ksq, axis=1, keepdims=True))
            kmax_scr[hc:hc + 1, :] = jnp.broadcast_to(kmax, (1, kmax_scr.shape[1]))

    lv = lamv_ref[...]
    lam = (jnp.exp(jnp.sum(lv[0:1] * lv[1:2], axis=1, keepdims=True))
           - jnp.exp(jnp.sum(lv[2:3] * lv[3:4], axis=1, keepdims=True)) + lam_init)
    q = q_ref[...].astype(F32)
    qcs = [q[:, hc * A_QK:(hc + 1) * A_QK] for hc in range(2 * A_HEADS)]
    shifts = [jnp.sqrt(jnp.sum(qc * qc, axis=1, keepdims=True)) * kmax_scr[hc:hc + 1, 0:1]
              for hc, qc in enumerate(qcs)]
    shift_max = jnp.max(functools.reduce(jnp.maximum, shifts))

    def q_ext(hc, shift):
        return jnp.concatenate([qcs[hc], jnp.broadcast_to(-shift, (tq, A_SHIFT_ROWS))], axis=1).astype(BF16)

    def k_ext(hc, start, size):
        shift_row = (lax.broadcasted_iota(jnp.int32, (A_SHIFT_ROWS, size), 0) == 0).astype(BF16)
        return jnp.concatenate([kt_ref[hc * A_QK:(hc + 1) * A_QK, pl.ds(start, size)], shift_row], axis=0)

    def scores(qx, hc, start, size):
        return jnp.dot(qx, k_ext(hc, start, size), preferred_element_type=F32)

    def finish(accs):
        outs = []
        for acc in accs:
            a0, a1 = acc[:tq], acc[tq:]
            o = a0[:, :A_V] / a0[:, A_V:] - lam * (a1[:, :A_V] / a1[:, A_V:])
            ms = jnp.mean(o * o, axis=1, keepdims=True)
            outs.append(o * lax.rsqrt(ms + EPS) * subln_ref[...] * (1.0 - lam_init))
        o_ref[...] = jnp.concatenate(outs, axis=1)

    def attend(shifted):
        zero = jnp.zeros((tq, 1), F32)
        qx = [q_ext(hc, shifts[hc] if shifted else zero) for hc in range(2 * A_HEADS)]

        def head_step(h, start, carry):
            s = [scores(qx[2 * h + c], 2 * h + c, start, tk) for c in range(2)]
            ve = ve_scr[h, pl.ds(start, tk), :]
            if shifted:
                p = jnp.concatenate([jnp.exp2(s[0]), jnp.exp2(s[1])], axis=0).astype(BF16)
                return carry + jnp.dot(p, ve, preferred_element_type=F32)
            m_run, acc = carry
            s = jnp.concatenate(s, axis=0)
            m_new = jnp.maximum(m_run, jnp.max(s, axis=1, keepdims=True))
            p = jnp.exp2(s - m_new).astype(BF16)
            return m_new, jnp.exp2(m_run - m_new) * acc + jnp.dot(p, ve, preferred_element_type=F32)

        def body(j, carries):
            start = pl.multiple_of(j * tk, tk)
            return tuple(head_step(h, start, carries[h]) for h in range(A_HEADS))

        acc0 = jnp.zeros((2 * tq, 2 * A_V), F32)
        if shifted:
            accs = lax.fori_loop(0, nk, body, (acc0,) * A_HEADS)
        else:
            init = (jnp.full((2 * tq, 1), NEG_INF, F32), acc0)
            accs = [c[1] for c in lax.fori_loop(0, nk, body, (init,) * A_HEADS)]
        finish(accs)

    def attend_ctx():
        accs = []
        for h in range(A_HEADS):
            zero = jnp.zeros((tq, 1), F32)
            s = jnp.concatenate([scores(q_ext(2 * h + c, zero), 2 * h + c, 0, ctx) for c in range(2)], axis=0)
            p = jnp.exp2(s - jnp.max(s, axis=1, keepdims=True)).astype(BF16)
            accs.append(jnp.dot(p, ve_scr[h, 0:ctx, :], preferred_element_type=F32))
        finish(accs)

    safe = shift_max < A_SHIFT_CAP
    if q_tile0 * tq < ctx:
        is_ctx = (step + q_tile0) * tq < ctx
        pl.when(is_ctx)(attend_ctx)
        pl.when(jnp.logical_and(jnp.logical_not(is_ctx), safe))(lambda: attend(True))
        pl.when(jnp.logical_and(jnp.logical_not(is_ctx), jnp.logical_not(safe)))(lambda: attend(False))
    else:
        pl.when(safe)(lambda: attend(True))
        pl.when(jnp.logical_not(safe))(lambda: attend(False))


def _attn_a_call(lamv, subln, qa, ka_t, va, *, tq, tk, ctx, q_tile0, lam_init):
    b, t, w = qa.shape
    nq = t // tq - q_tile0
    return pl.pallas_call(
        functools.partial(_attn_a_kernel, tq=tq, tk=tk, ctx=ctx, nk=t // tk, q_tile0=q_tile0, lam_init=lam_init),
        grid=(b, nq),
        in_specs=[pl.BlockSpec(lamv.shape, lambda bi, qi: (0, 0)),
                  pl.BlockSpec(subln.shape, lambda bi, qi: (0, 0)),
                  pl.BlockSpec((None, tq, w), lambda bi, qi: (bi, qi + q_tile0, 0)),
                  pl.BlockSpec((None, w, t), lambda bi, qi: (bi, 0, 0)),
                  pl.BlockSpec((None, t, w), lambda bi, qi: (bi, 0, 0))],
        out_specs=pl.BlockSpec((None, tq, w), lambda bi, qi: (bi, qi, 0)),
        out_shape=jax.ShapeDtypeStruct((b, nq * tq, w), F32),
        scratch_shapes=[pltpu.VMEM((A_HEADS, t, 2 * A_V), BF16), pltpu.VMEM((2 * A_HEADS, 128), F32)],
        compiler_params=_params(("parallel", "arbitrary")),
        name="diff_attention",
    )(lamv, subln, qa, ka_t, va)


def _attn_b_kernel(sink_ref, q_ref, kp_ref, kc_ref, kn_ref, kx_ref, vp_ref, vc_ref, vn_ref, vx_ref, o_ref, *,
                   blk0, ctx_blocks, nblk):
    n = pl.program_id(1) + blk0
    q = q_ref[...]
    k_all = jnp.concatenate([kp_ref[...], kc_ref[...], kn_ref[...], kx_ref[...]], axis=0)
    v_all = jnp.concatenate([vp_ref[...], vc_ref[...], vn_ref[...], vx_ref[...]], axis=0)
    nkeys = k_all.shape[0]
    i = lax.broadcasted_iota(jnp.int32, (B_BLOCK, nkeys), 0)
    j = lax.broadcasted_iota(jnp.int32, (B_BLOCK, nkeys), 1)
    lat = n >= ctx_blocks
    pen_prev = jnp.where(n - 1 >= ctx_blocks, 0.0, NEG_INF)
    pen_cur = jnp.where(lat, 0.0, NEG_INF)
    pen_next = jnp.where(jnp.logical_and(lat, n + 1 <= nblk - 1), 0.0, NEG_INF)
    bias = jnp.where(j < B_BLOCK, jnp.where(j >= i, pen_prev, NEG_INF),
                     jnp.where(j < 2 * B_BLOCK, pen_cur,
                               jnp.where(j < 3 * B_BLOCK, jnp.where(j - 2 * B_BLOCK <= i, pen_next, NEG_INF), 0.0)))
    outs = []
    for hq in range(B_HEADS):
        g = hq // (B_HEADS // B_KV_HEADS)
        s = lax.dot_general(q[:, hq * B_DIM:(hq + 1) * B_DIM], k_all[:, g * B_DIM:(g + 1) * B_DIM],
                            (((1,), (1,)), ((), ())), preferred_element_type=F32)
        s = s + bias
        sk = sink_ref[hq]
        m = jnp.maximum(jnp.max(s, axis=1, keepdims=True), sk)
        p = jnp.exp(s - m)
        den = jnp.sum(p, axis=1, keepdims=True) + jnp.exp(sk - m)
        pv = jnp.dot(p.astype(BF16), v_all[:, g * B_DIM:(g + 1) * B_DIM], preferred_element_type=F32)
        outs.append(pv / den)
    o_ref[...] = jnp.concatenate(outs, axis=1)


def _attn_b_call(sink, qb, kb, vb, *, ctx, blk0):
    b, t, _ = qb.shape
    nblk = t // B_BLOCK
    ctx_blocks = ctx // B_BLOCK
    cur = lambda bi, i: (bi, i + blk0, 0)
    prev = lambda bi, i: (bi, jnp.maximum(i + blk0 - 1, 0), 0)
    nxt = lambda bi, i: (bi, jnp.minimum(i + blk0 + 1, nblk - 1), 0)
    kv = lambda im: pl.BlockSpec((None, B_BLOCK, B_KCOLS), im)
    cx = pl.BlockSpec((None, ctx, B_KCOLS), lambda bi, i: (bi, 0, 0))
    return pl.pallas_call(
        functools.partial(_attn_b_kernel, blk0=blk0, ctx_blocks=ctx_blocks, nblk=nblk),
        grid=(b, nblk - blk0),
        in_specs=[pl.BlockSpec(memory_space=pltpu.SMEM),
                  pl.BlockSpec((None, B_BLOCK, B_QCOLS), cur),
                  kv(prev), kv(cur), kv(nxt), cx, kv(prev), kv(cur), kv(nxt), cx],
        out_specs=pl.BlockSpec((None, B_BLOCK, B_QCOLS), lambda bi, i: (bi, i, 0)),
        out_shape=jax.ShapeDtypeStruct((b, (nblk - blk0) * B_BLOCK, B_QCOLS), F32),
        compiler_params=_params(("parallel", "parallel")),
        name="window_attention",
    )(sink, qb, kb, kb, kb, kb, vb, vb, vb, vb)


def _conv_kernel(u_ref, up_ref, un_ref, w_ref, b_ref, o_ref, *, tm, ctx_tiles, ntiles):
    i = pl.program_id(1)
    u = u_ref[...]
    has_prev = jnp.logical_and(i != 0, i != ctx_tiles)
    has_next = jnp.logical_and(i != ctx_tiles - 1, i != ntiles - 1)
    up = jnp.where(has_prev, up_ref[...], 0.0)
    un = jnp.where(has_next, un_ref[...], 0.0)
    full = jnp.concatenate([up, u, un], axis=0)
    w = w_ref[...]
    acc = b_ref[...] + w[0:1] * full[HALO - 2:HALO - 2 + tm]
    for k in range(1, C_CONV):
        acc = acc + w[k:k + 1] * full[HALO - 2 + k:HALO - 2 + k + tm]
    o_ref[...] = _silu(acc)


def _conv_call(xbc_raw, conv_w, conv_b, *, tm, ctx):
    b, t, ch = xbc_raw.shape
    ntiles = t // tm
    hb = tm // HALO
    nh = t // HALO
    w8 = jnp.zeros((8, ch), F32).at[:C_CONV].set(conv_w)
    return pl.pallas_call(
        functools.partial(_conv_kernel, tm=tm, ctx_tiles=ctx // tm, ntiles=ntiles),
        grid=(b, ntiles),
        in_specs=[pl.BlockSpec((None, tm, ch), lambda bi, i: (bi, i, 0)),
                  pl.BlockSpec((None, HALO, ch), lambda bi, i: (bi, jnp.maximum(i * hb - 1, 0), 0)),
                  pl.BlockSpec((None, HALO, ch), lambda bi, i: (bi, jnp.minimum((i + 1) * hb, nh - 1), 0)),
                  pl.BlockSpec((8, ch), lambda bi, i: (0, 0)),
                  pl.BlockSpec((1, ch), lambda bi, i: (0, 0))],
        out_specs=pl.BlockSpec((None, tm, ch), lambda bi, i: (bi, i, 0)),
        out_shape=jax.ShapeDtypeStruct((b, t, ch), F32),
        compiler_params=_params(("parallel", "parallel")),
        name="ssm_conv",
    )(xbc_raw, xbc_raw, xbc_raw, w8, conv_b.reshape(1, ch))


def _softplus(v):
    return jnp.maximum(v, 0.0) + jnp.log1p(jnp.exp(-jnp.abs(v)))


def _ssd_direction(x_ref, dt_ref, par, y_ref, h_scr, *, lane0, backward):
    xbc = x_ref[...]
    xs = xbc[:, :C_INNER]
    bm = xbc[:, C_INNER:C_INNER + C_GN]
    cm = xbc[:, C_INNER + C_GN:].astype(BF16)
    dt = _softplus(dt_ref[...] + par[0:1])
    a = dt * par[1:2]
    li = lax.broadcasted_iota(jnp.int32, (C_CHUNK, C_CHUNK), 0)
    si = lax.broadcasted_iota(jnp.int32, (C_CHUNK, C_CHUNK), 1)
    tri = (si <= li).astype(F32)
    cs = jnp.dot(tri, a, precision=HIGHEST, preferred_element_type=F32)
    tot = cs[C_CHUNK - 1:C_CHUNK, :]
    key = cs - a if backward else cs
    key_t = key.T
    bm_t = bm.T
    bm16 = bm.astype(BF16)
    mask = (si >= li) if backward else (si <= li)
    ys = []
    for g in range(C_GROUPS):
        gs = slice(g * C_STATE, (g + 1) * C_STATE)
        cb = lax.dot_general(cm[:, gs], bm16[:, gs], (((1,), (1,)), ((), ())), preferred_element_type=F32)
        for r in range(C_HPG):
            h = g * C_HPG + r
            hl = lane0 + h
            col = key[:, hl:hl + 1]
            row = key_t[hl:hl + 1, :]
            tot_h = tot[:, hl:hl + 1]
            d = (row - col) if backward else (col - row)
            decay = jnp.exp(jnp.where(mask, d, NEG_INF))
            xdt = (xs[:, h * C_HEAD_DIM:(h + 1) * C_HEAD_DIM] * dt[:, hl:hl + 1]).astype(BF16)
            y_diag = jnp.dot((cb * decay).astype(BF16), xdt, preferred_element_type=F32)
            h_prev = h_scr[h]
            off_scale = jnp.exp(tot_h - col) if backward else jnp.exp(col)
            y_off = jnp.dot(cm[:, gs], h_prev.astype(BF16), preferred_element_type=F32) * off_scale
            ys.append(y_diag + y_off)
            w_row = jnp.exp(row) if backward else jnp.exp(tot_h - row)
            bw_t = (bm_t[gs, :] * w_row).astype(BF16)
            h_scr[h] = jnp.exp(tot_h) * h_prev + jnp.dot(bw_t, xdt, preferred_element_type=F32)
    y_ref[...] = jnp.concatenate(ys, axis=1)


def _ssd_kernel(par_ref, xf_ref, dtf_ref, xb_ref, dtb_ref, yf_ref, yb_ref, hf_scr, hb_scr):
    @pl.when(pl.program_id(1) == 0)
    def _():
        hf_scr[...] = jnp.zeros(hf_scr.shape, F32)
        hb_scr[...] = jnp.zeros(hb_scr.shape, F32)

    par = par_ref[...]
    _ssd_direction(xf_ref, dtf_ref, par, yf_ref, hf_scr, lane0=0, backward=False)
    _ssd_direction(xb_ref, dtb_ref, par, yb_ref, hb_scr, lane0=C_HEADS, backward=True)


def _ssd_call(par, xbc, dt, *, ctx):
    b, t, ch = xbc.shape
    nc = t // C_CHUNK
    ncc = ctx // C_CHUNK
    fwd = lambda bi, j: (bi, j, 0)
    bwd = lambda bi, j: (bi, jnp.where(j < ncc, ncc - 1 - j, nc - 1 + ncc - j), 0)
    return pl.pallas_call(
        _ssd_kernel,
        grid=(b, nc),
        in_specs=[pl.BlockSpec((8, DT_LANES), lambda bi, j: (0, 0)),
                  pl.BlockSpec((None, C_CHUNK, ch), fwd), pl.BlockSpec((None, C_CHUNK, DT_LANES), fwd),
                  pl.BlockSpec((None, C_CHUNK, ch), bwd), pl.BlockSpec((None, C_CHUNK, DT_LANES), bwd)],
        out_specs=[pl.BlockSpec((None, C_CHUNK, C_INNER), fwd), pl.BlockSpec((None, C_CHUNK, C_INNER), bwd)],
        out_shape=[jax.ShapeDtypeStruct((b, t, C_INNER), F32)] * 2,
        scratch_shapes=[pltpu.VMEM((C_HEADS, C_STATE, C_HEAD_DIM), F32)] * 2,
        compiler_params=_params(("parallel", "arbitrary")),
        name="ssd_scan",
    )(par, xbc, dt, xbc, dt)


def _outproj_kernel(x_ref, mod_ref, oa_ref, ob_ref, yf_ref, yb_ref, xs_ref, z_ref, dg_ref, w_ref, o_ref):
    m = mod_ref[...]
    dg = dg_ref[...]
    y = (yf_ref[...] + yb_ref[...] + dg[0:1] * xs_ref[...]) * _silu(z_ref[...])
    gw = C_INNER // C_GROUPS
    oc = []
    for g in range(C_GROUPS):
        yg = y[:, g * gw:(g + 1) * gw]
        ms = jnp.mean(yg * yg, axis=1, keepdims=True)
        oc.append(yg * lax.rsqrt(ms + EPS) * dg[1:2, g * gw:(g + 1) * gw])
    mix = jnp.concatenate([oa_ref[...], ob_ref[...]] + oc, axis=1).astype(BF16)
    o_ref[...] = x_ref[...] + m[5:6] * jnp.dot(mix, w_ref[...], preferred_element_type=F32)


def _outproj_call(xs, mod, oa, ob, yf, yb, xbc, z, dg, w_out, *, tm, ctx_tiles, tile0, nbatch):
    b, t, d = xs.shape
    ntiles = t // tm - tile0
    tok = lambda w: pl.BlockSpec((None, tm, w), lambda bi, i: (bi, i + tile0, 0))
    tok0 = lambda w: pl.BlockSpec((None, tm, w), lambda bi, i: (bi, i, 0))
    mod_row = lambda bi, i: (jnp.where(i + tile0 < ctx_tiles, nbatch, bi), 0, 0)
    return pl.pallas_call(
        _outproj_kernel,
        grid=(b, ntiles),
        in_specs=[tok(d), pl.BlockSpec((None, N_MOD, d), mod_row),
                  tok0(A_HEADS * A_V), tok0(B_QCOLS), tok(C_INNER), tok(C_INNER), tok(C_INNER), tok(C_INNER),
                  pl.BlockSpec(dg.shape, lambda bi, i: (0, 0)),
                  pl.BlockSpec(w_out.shape, lambda bi, i: (0, 0), pipeline_mode=pl.Buffered(1))],
        out_specs=tok0(d),
        out_shape=jax.ShapeDtypeStruct((b, ntiles * tm, d), F32),
        compiler_params=_params(("parallel", "parallel")),
        name="mix_out_proj",
    )(xs, mod, oa, ob, yf, yb, xbc, z, dg, w_out)


def _rope_tables(seq, ctx, dim, reps):
    rows = seq // GRID_W
    row = jnp.repeat(jnp.arange(rows, dtype=F32), GRID_W)
    col = jnp.tile(jnp.arange(GRID_W, dtype=F32), rows)
    quarter = dim // 4
    inv_freq = ROPE_BASE ** (-jnp.arange(quarter, dtype=F32) / quarter)
    ar = row[:, None] * inv_freq
    ac = col[:, None] * inv_freq
    ang = jnp.concatenate([ar, ar, ac, ac], axis=-1)
    sign = jnp.where((jnp.arange(dim) % (dim // 2)) < quarter, -1.0, 1.0).astype(F32)
    cos = jnp.concatenate([jnp.ones((ctx, dim), F32), jnp.cos(ang)], axis=0)
    sin = jnp.concatenate([jnp.zeros((ctx, dim), F32), jnp.sin(ang) * sign], axis=0)
    return jnp.tile(cos, (1, reps)), jnp.tile(sin, (1, reps))


def _group_matrix(n, group):
    idx = jnp.arange(n) // group
    return (idx[:, None] == idx[None, :]).astype(BF16)


def kernel(x, c, ctx, c_ctx, w_mod, b_mod, norm_ffn1, ffn1_w13, ffn1_w2, norm_mix, w_in, w_out, qn_a, kn_a, lam_q1, lam_k1, lam_q2, lam_k2, subln_a, qn_b, kn_b, sink_b, conv_w, conv_b, dt_bias, a_log, d_skip, gnorm_c, norm_ffn2, ffn2_w13, ffn2_w2):
    nb, seq, d = x.shape
    nctx = ctx.shape[1]
    t = nctx + seq
    depth = w_mod.shape[0]
    tm = 256
    fc = 256
    tq, tk = 256, (768 if t % 768 == 0 and t > 768 else 256)
    assert nctx % tm == 0 and t % tm == 0 and nctx % C_CHUNK == 0 and seq % GRID_W == 0 and nb < 8
    ctx_tiles = nctx // tm

    xs = x
    cvec = jnp.zeros((8, d), F32).at[:nb].set(c).at[nb].set(c_ctx)
    mod = _mod_call(cvec, w_mod, b_mod).reshape(depth, 8, N_MOD, d)

    cosa, sina = _rope_tables(seq, nctx, A_QK, ROPE_LANES // A_QK)
    cosb, sinb = _rope_tables(seq, nctx, B_DIM, ROPE_LANES // B_DIM)
    ga = _group_matrix(A_QCOLS, A_QK)
    gb = _group_matrix(B_QCOLS, B_DIM)
    in_cols = w_in.shape[2]
    in_pad = (-in_cols) % DT_LANES

    for l in range(depth):
        last = l == depth - 1
        tile0 = ctx_tiles if last else 0
        lam_init = 0.8 - 0.6 * math.exp(-0.3 * l)
        w_in_p = jnp.pad(w_in[l], ((0, 0), (0, in_pad))).astype(BF16)
        hn = jnp.stack([jnp.tile(qn_a[l], A_QCOLS // A_QK), jnp.tile(kn_a[l], A_QCOLS // A_QK),
                        jnp.tile(qn_b[l], B_QCOLS // B_DIM), jnp.tile(kn_b[l], B_QCOLS // B_DIM)]).astype(F32)
        hn = jnp.zeros((8, A_QCOLS), F32).at[:4].set(hn)

        xs = _ffn_call(xs, mod[l], norm_ffn1[l], ffn1_w13[l].astype(BF16), ffn1_w2[l].astype(BF16), si=0, tm=tm, fc=fc,
                       ctx_tiles=ctx_tiles, nbatch=nb, ctx_in=(ctx if l == 0 else None))

        qa, ka_t, va, qb, kb, vb, z, xbc_raw, dt = _inproj_call(
            xs, mod[l], norm_mix[l], w_in_p, ga, gb, hn, cosa, sina, cosb, sinb,
            tm=tm, ctx_tiles=ctx_tiles, nbatch=nb)

        lamv = jnp.zeros((8, A_QK), F32).at[:4].set(jnp.stack([lam_q1[l], lam_k1[l], lam_q2[l], lam_k2[l]]))
        oa = _attn_a_call(lamv, subln_a[l].reshape(1, A_V), qa, ka_t, va, tq=tq, tk=tk, ctx=nctx,
                          q_tile0=(nctx // tq if last else 0), lam_init=lam_init)

        ob = _attn_b_call(sink_b[l].astype(F32), qb, kb, vb, ctx=nctx, blk0=(nctx // B_BLOCK if last else 0))

        xbc = _conv_call(xbc_raw, conv_w[l], conv_b[l], tm=tm, ctx=nctx)
        par = (jnp.zeros((8, DT_LANES), F32)
               .at[0, :2 * C_HEADS].set(dt_bias[l].reshape(-1))
               .at[1, :2 * C_HEADS].set(-jnp.exp(a_log[l].astype(F32)).reshape(-1)))
        yf, yb = _ssd_call(par, xbc, dt, ctx=nctx)

        dg = (jnp.zeros((8, C_INNER), F32)
              .at[0].set(jnp.repeat(d_skip[l], C_HEAD_DIM)).at[1].set(gnorm_c[l]))
        xs = _outproj_call(xs, mod[l], oa, ob, yf, yb, xbc, z, dg, w_out[l].astype(BF16),
                           tm=tm, ctx_tiles=ctx_tiles, tile0=tile0, nbatch=nb)
        xs = _ffn_call(xs, mod[l], norm_ffn2[l], ffn2_w13[l].astype(BF16), ffn2_w2[l].astype(BF16), si=6, tm=tm, fc=fc,
                       ctx_tiles=(0 if last else ctx_tiles), nbatch=nb)
    return xs
```

```python
import functools
import math

import jax
import jax.numpy as jnp
from jax import lax
from jax.experimental import pallas as pl
from jax.experimental.pallas import tpu as pltpu

F32 = jnp.float32
BF16 = jnp.bfloat16
HIGHEST = lax.Precision.HIGHEST
NEG_INF = float("-inf")

EPS = 1e-6
ROPE_BASE = 10000.0
GRID_W = 64
ROPE_LANES = 128
N_MOD = 9

A_HEADS, A_QK, A_V = 4, 32, 64
A_QCOLS = A_HEADS * 2 * A_QK
A_SCALE = A_QK ** -0.5
LOG2E = math.log2(math.e)
A_HLANES = 128
A_VROWS = A_V + 16
A_SHIFT_CAP = 48.0
B_HEADS, B_KV_HEADS, B_DIM = 4, 2, 64
B_QCOLS = B_HEADS * B_DIM
B_KCOLS = B_KV_HEADS * B_DIM
B_SCALE = B_DIM ** -0.5
B_BLOCK = 128
C_HEADS, C_HEAD_DIM, C_GROUPS, C_STATE, C_CONV = 8, 64, 2, 64, 5
C_INNER = C_HEADS * C_HEAD_DIM
C_GN = C_GROUPS * C_STATE
C_XBC = C_INNER + 2 * C_GN
C_CHUNK = 128
C_HPG = C_HEADS // C_GROUPS
DT_LANES = 128
HALO = 8

V7X_VMEM_LIMIT = 56 * 1024 * 1024


def _params(sem, vmem=V7X_VMEM_LIMIT):
    return pltpu.CompilerParams(dimension_semantics=sem, vmem_limit_bytes=vmem)


def _rms_mod(x, nw, shift, scale):
    ms = jnp.mean(x * x, axis=-1, keepdims=True)
    y = x * lax.rsqrt(ms + EPS) * nw
    return y * (1.0 + scale) + shift


def _silu(v):
    return v * jax.nn.sigmoid(v)


def _mod_kernel(c_ref, w_ref, b_ref, o_ref):
    sc = _silu(c_ref[...])
    o_ref[...] = jnp.dot(sc, w_ref[...], precision=HIGHEST, preferred_element_type=F32) + b_ref[...]


def _mod_call(cvec, w_mod, b_mod, tn=1152):
    nl, d, n = w_mod.shape
    return pl.pallas_call(
        _mod_kernel,
        grid=(nl, n // tn),
        in_specs=[pl.BlockSpec((8, d), lambda l, j: (0, 0)),
                  pl.BlockSpec((None, d, tn), lambda l, j: (l, 0, j)),
                  pl.BlockSpec((None, 1, tn), lambda l, j: (l, 0, j))],
        out_specs=pl.BlockSpec((None, 8, tn), lambda l, j: (l, 0, j)),
        out_shape=jax.ShapeDtypeStruct((nl, 8, n), F32),
        compiler_params=_params(("parallel", "parallel")),
        name="adaln_mod",
    )(cvec, w_mod, b_mod.reshape(nl, 1, n))


def _ffn_kernel(*refs, si, fc, ff, ctx_tiles, split_input):
    if split_input:
        c_ref, x_ref, mod_ref, nw_ref, w13_ref, w2_ref, o_ref, a_scr = refs
        x = jnp.where(pl.program_id(1) < ctx_tiles, c_ref[...], x_ref[...])
    else:
        x_ref, mod_ref, nw_ref, w13_ref, w2_ref, o_ref, a_scr = refs
        x = x_ref[...]
    m = mod_ref[...]
    h = _rms_mod(x, nw_ref[...], m[si:si + 1], m[si + 1:si + 2]).astype(BF16)
    for c in range(ff // fc):
        g = jnp.dot(h, w13_ref[:, c * fc:(c + 1) * fc], preferred_element_type=F32)
        u = jnp.dot(h, w13_ref[:, ff + c * fc:ff + (c + 1) * fc], preferred_element_type=F32)
        a_scr[:, c * fc:(c + 1) * fc] = (_silu(g) * u).astype(BF16)
    y = jnp.dot(a_scr[...], w2_ref[...], preferred_element_type=F32)
    o_ref[...] = x + (0.5 * m[si + 2:si + 3]) * y


def _ffn_call(xs, mod, nw, w13, w2, *, si, tm, fc, ctx_tiles, nbatch, ctx_in=None):
    b, t, d = xs.shape
    ff = w2.shape[0]
    split = ctx_in is not None
    ntiles = t // tm + (ctx_tiles if split else 0)
    mod_row = lambda bi, i: (jnp.where(i < ctx_tiles, nbatch, bi), 0, 0)
    if split:
        tok_specs = [pl.BlockSpec((None, tm, d), lambda bi, i: (bi, jnp.minimum(i, ctx_tiles - 1), 0)),
                     pl.BlockSpec((None, tm, d), lambda bi, i: (bi, jnp.maximum(i - ctx_tiles, 0), 0))]
        toks = (ctx_in, xs)
    else:
        tok_specs = [pl.BlockSpec((None, tm, d), lambda bi, i: (bi, i, 0))]
        toks = (xs,)
    return pl.pallas_call(
        functools.partial(_ffn_kernel, si=si, fc=fc, ff=ff, ctx_tiles=ctx_tiles, split_input=split),
        grid=(b, ntiles),
        in_specs=tok_specs + [pl.BlockSpec((None, N_MOD, d), mod_row),
                              pl.BlockSpec((1, d), lambda bi, i: (0, 0)),
                              pl.BlockSpec(w13.shape, lambda bi, i: (0, 0), pipeline_mode=pl.Buffered(1)),
                              pl.BlockSpec(w2.shape, lambda bi, i: (0, 0), pipeline_mode=pl.Buffered(1))],
        out_specs=pl.BlockSpec((None, tm, d), lambda bi, i: (bi, i, 0)),
        out_shape=jax.ShapeDtypeStruct((b, ntiles * tm, d), F32),
        scratch_shapes=[pltpu.VMEM((tm, ff), BF16)],
        compiler_params=_params(("parallel", "parallel")),
        name="swiglu_half",
    )(*toks, mod, nw.reshape(1, d), w13, w2)


def _rope(v, cos, sin_signed, quarter):
    n = v.shape[-1]
    lane = lax.broadcasted_iota(jnp.int32, v.shape, 1)
    first = (lane & (2 * quarter - 1)) < quarter
    vr = jnp.where(first, pltpu.roll(v, n - quarter, 1), pltpu.roll(v, quarter, 1))
    return v * cos + vr * sin_signed


def _group_norm(v, gmat, inv_n, w):
    sq = v * v
    hi = sq.astype(BF16)
    lo = (sq - hi.astype(F32)).astype(BF16)
    ms = (jnp.dot(hi, gmat, preferred_element_type=F32) + jnp.dot(lo, gmat, preferred_element_type=F32)) * inv_n
    return v * lax.rsqrt(ms + EPS) * w


def _inproj_kernel(x_ref, mod_ref, nw_ref, w_ref, ga_ref, gb_ref, hn_ref, kpl_ref, kone_ref,
                   cosa_ref, sina_ref, cosb_ref, sinb_ref, qa_o, ka_o, va_o, qb_o, kb_o, vb_o, z_o, xbc_o, dt_o):
    m = mod_ref[...]
    h = _rms_mod(x_ref[...], nw_ref[...], m[3:4], m[4:5]).astype(BF16)
    p = jnp.dot(h, w_ref[...], preferred_element_type=F32)
    hn = hn_ref[...]
    ga, gb = ga_ref[...], gb_ref[...]
    twice = lambda a: jnp.concatenate([a, a], axis=1)
    cosa, sina, cosb, sinb = twice(cosa_ref[...]), twice(sina_ref[...]), twice(cosb_ref[...]), twice(sinb_ref[...])
    o = 0
    qa = _group_norm(p[:, o:o + A_QCOLS], ga, 1.0 / A_QK, hn[0:1]); o += A_QCOLS
    ka = _group_norm(p[:, o:o + A_QCOLS], ga, 1.0 / A_QK, hn[1:2]); o += A_QCOLS
    qa_o[...] = (_rope(qa, cosa, sina, A_QK // 4) * (A_SCALE * LOG2E)).T.astype(BF16)
    ka16 = _rope(ka, cosa, sina, A_QK // 4).astype(BF16)
    ka_o[...] = (jnp.dot(ka16, kpl_ref[...], preferred_element_type=F32) + kone_ref[...]).astype(BF16)
    va_o[...] = p[:, o:o + A_HEADS * A_V].T.astype(BF16); o += A_HEADS * A_V
    qb = _group_norm(p[:, o:o + B_QCOLS], gb, 1.0 / B_DIM, hn[2:3]); o += B_QCOLS
    kb = _group_norm(p[:, o:o + B_KCOLS], gb[:B_KCOLS, :B_KCOLS], 1.0 / B_DIM, hn[3:4, :B_KCOLS]); o += B_KCOLS
    qb_o[...] = (_rope(qb, cosb, sinb, B_DIM // 4) * B_SCALE).astype(BF16)
    kb_o[...] = _rope(kb, cosb[:, :B_KCOLS], sinb[:, :B_KCOLS], B_DIM // 4).astype(BF16)
    vb_o[...] = p[:, o:o + B_KCOLS].astype(BF16); o += B_KCOLS
    z_o[...] = p[:, o:o + C_INNER]; o += C_INNER
    xbc_o[...] = p[:, o:o + C_XBC]; o += C_XBC
    dt_o[...] = p[:, o:o + DT_LANES]


def _inproj_call(xs, mod, nw, w_in_p, ga, gb, hn, kpl, kone, cosa, sina, cosb, sinb, *, tm, ctx_tiles, nbatch):
    b, t, d = xs.shape
    ntiles = t // tm
    tok = lambda w: pl.BlockSpec((None, tm, w), lambda bi, i: (bi, i, 0))
    tab = lambda w: pl.BlockSpec((tm, w), lambda bi, i: (i, 0))
    full = lambda a: pl.BlockSpec(a.shape, lambda bi, i: (0,) * a.ndim)
    mod_row = lambda bi, i: (jnp.where(i < ctx_tiles, nbatch, bi), 0, 0)
    widths = (A_QCOLS, A_HEADS * A_HLANES, A_HEADS * A_V, B_QCOLS, B_KCOLS, B_KCOLS, C_INNER, C_XBC, DT_LANES)
    transposed = (0, 2)
    dtypes = (BF16,) * 6 + (F32,) * 3
    return pl.pallas_call(
        _inproj_kernel,
        grid=(b, ntiles),
        in_specs=[tok(d), pl.BlockSpec((None, N_MOD, d), mod_row), pl.BlockSpec((1, d), lambda bi, i: (0, 0)),
                  pl.BlockSpec(w_in_p.shape, lambda bi, i: (0, 0), pipeline_mode=pl.Buffered(1)),
                  full(ga), full(gb), full(hn), full(kpl), full(kone)] + [tab(ROPE_LANES)] * 4,
        out_specs=[pl.BlockSpec((None, w, tm), lambda bi, i: (bi, 0, i)) if k in transposed else tok(w)
                   for k, w in enumerate(widths)],
        out_shape=[jax.ShapeDtypeStruct((b, w, t) if k in transposed else (b, t, w), dt)
                   for k, (w, dt) in enumerate(zip(widths, dtypes))],
        compiler_params=_params(("parallel", "parallel")),
        name="in_proj_heads",
    )(xs, mod, nw.reshape(1, d), w_in_p, ga, gb, hn, kpl, kone, cosa, sina, cosb, sinb)


def _attn_a_kernel(lamv_ref, subln_ref, qt_ref, kx_ref, vt_ref, o_ref, kmax_scr, *,
                   tq, tk, ctx, nk, q_tile0, lam_init):
    step = pl.program_id(1)

    @pl.when(step == 0)
    def _():
        r = lax.broadcasted_iota(jnp.int32, (A_HLANES, A_HLANES), 0)
        c = lax.broadcasted_iota(jnp.int32, (A_HLANES, A_HLANES), 1)
        in_comp = jnp.logical_and(r >= c * A_QK, r < (c + 1) * A_QK)
        sel = jnp.where(jnp.logical_and(in_comp, c < 2), 1.0, 0.0).astype(BF16)
        for h in range(A_HEADS):
            kk = kx_ref[:, h * A_HLANES:(h + 1) * A_HLANES].astype(F32)
            ksq = jnp.dot((kk * kk).astype(BF16), sel, preferred_element_type=F32)
            kmax_scr[h:h + 1, :] = jnp.sqrt(jnp.max(ksq, axis=0, keepdims=True))

    lv = lamv_ref[...]
    lam = (jnp.exp(jnp.sum(lv[0:1] * lv[1:2], axis=1, keepdims=True))
           - jnp.exp(jnp.sum(lv[2:3] * lv[3:4], axis=1, keepdims=True)) + lam_init)
    qt = qt_ref[...].astype(F32)
    qcs = [qt[hc * A_QK:(hc + 1) * A_QK, :] for hc in range(2 * A_HEADS)]
    shifts = [jnp.sqrt(jnp.sum(qc * qc, axis=0, keepdims=True)) * kmax_scr[hc // 2:hc // 2 + 1, hc % 2:hc % 2 + 1]
              for hc, qc in enumerate(qcs)]
    shift_max = jnp.max(functools.reduce(jnp.maximum, shifts))

    def q_ext(h, shifted):
        z = jnp.zeros((A_QK, tq), F32)
        top = jnp.concatenate([qcs[2 * h], z], axis=1)
        mid = jnp.concatenate([z, qcs[2 * h + 1]], axis=1)
        row = lax.broadcasted_iota(jnp.int32, (A_HLANES - 2 * A_QK, 2 * tq), 0)
        if shifted:
            bot = jnp.where(row == 0, -jnp.concatenate([shifts[2 * h], shifts[2 * h + 1]], axis=1), 0.0)
        else:
            bot = jnp.zeros(row.shape, F32)
        return jnp.concatenate([top, mid, bot], axis=0).astype(BF16)

    def scores(qx, h, start, size):
        return jnp.dot(kx_ref[pl.ds(start, size), h * A_HLANES:(h + 1) * A_HLANES], qx, preferred_element_type=F32)

    def v_ext(h, start, size):
        ones = jnp.ones((A_VROWS - A_V, size), BF16)
        return jnp.concatenate([vt_ref[h * A_V:(h + 1) * A_V, pl.ds(start, size)], ones], axis=0)

    def finish(accs):
        rows = []
        for acc in accs:
            o = (acc[:A_V, :tq] / acc[A_V:A_V + 1, :tq]) - lam * (acc[:A_V, tq:] / acc[A_V:A_V + 1, tq:])
            ms = jnp.mean(o * o, axis=0, keepdims=True)
            rows.append(o * lax.rsqrt(ms + EPS) * subln_ref[...] * (1.0 - lam_init))
        o_ref[...] = jnp.concatenate(rows, axis=0).T

    def attend(shifted):
        qx = [q_ext(h, shifted) for h in range(A_HEADS)]

        def head_step(h, start, carry):
            s = scores(qx[h], h, start, tk)
            ve = v_ext(h, start, tk)
            if shifted:
                return carry + jnp.dot(ve, jnp.exp2(s).astype(BF16), preferred_element_type=F32)
            m_run, acc = carry
            m_new = jnp.maximum(m_run, jnp.max(s, axis=0, keepdims=True))
            p = jnp.exp2(s - m_new).astype(BF16)
            return m_new, jnp.exp2(m_run - m_new) * acc + jnp.dot(ve, p, preferred_element_type=F32)

        def body(j, carries):
            start = pl.multiple_of(j * tk, tk)
            return tuple(head_step(h, start, carries[h]) for h in range(A_HEADS))

        acc0 = jnp.zeros((A_VROWS, 2 * tq), F32)
        if shifted:
            accs = lax.fori_loop(0, nk, body, (acc0,) * A_HEADS)
        else:
            init = (jnp.full((1, 2 * tq), NEG_INF, F32), acc0)
            accs = [c[1] for c in lax.fori_loop(0, nk, body, (init,) * A_HEADS)]
        finish(accs)

    def attend_ctx():
        accs = []
        for h in range(A_HEADS):
            s = scores(q_ext(h, False), h, 0, ctx)
            p = jnp.exp2(s - jnp.max(s, axis=0, keepdims=True)).astype(BF16)
            accs.append(jnp.dot(v_ext(h, 0, ctx), p, preferred_element_type=F32))
        finish(accs)

    safe = shift_max < A_SHIFT_CAP
    if q_tile0 * tq < ctx:
        is_ctx = (step + q_tile0) * tq < ctx
        pl.when(is_ctx)(attend_ctx)
        pl.when(jnp.logical_and(jnp.logical_not(is_ctx), safe))(lambda: attend(True))
        pl.when(jnp.logical_and(jnp.logical_not(is_ctx), jnp.logical_not(safe)))(lambda: attend(False))
    else:
        pl.when(safe)(lambda: attend(True))
        pl.when(jnp.logical_not(safe))(lambda: attend(False))


def _attn_a_call(lamv, subln_col, qa_t, ka_x, va_t, *, tq, tk, ctx, q_tile0, lam_init):
    b, w, t = qa_t.shape
    nq = t // tq - q_tile0
    return pl.pallas_call(
        functools.partial(_attn_a_kernel, tq=tq, tk=tk, ctx=ctx, nk=t // tk, q_tile0=q_tile0, lam_init=lam_init),
        grid=(b, nq),
        in_specs=[pl.BlockSpec(lamv.shape, lambda bi, qi: (0, 0)),
                  pl.BlockSpec(subln_col.shape, lambda bi, qi: (0, 0)),
                  pl.BlockSpec((None, w, tq), lambda bi, qi: (bi, 0, qi + q_tile0)),
                  pl.BlockSpec((None, t, A_HEADS * A_HLANES), lambda bi, qi: (bi, 0, 0)),
                  pl.BlockSpec((None, w, t), lambda bi, qi: (bi, 0, 0))],
        out_specs=pl.BlockSpec((None, tq, w), lambda bi, qi: (bi, qi, 0)),
        out_shape=jax.ShapeDtypeStruct((b, nq * tq, w), F32),
        scratch_shapes=[pltpu.VMEM((8, A_HLANES), F32)],
        compiler_params=_params(("parallel", "arbitrary")),
        name="diff_attention",
    )(lamv, subln_col, qa_t, ka_x, va_t)


def _attn_b_kernel(sink_ref, q_ref, kp_ref, kc_ref, kn_ref, kx_ref, vp_ref, vc_ref, vn_ref, vx_ref, o_ref, *,
                   blk0, ctx_blocks, nblk):
    n = pl.program_id(1) + blk0
    q = q_ref[...]
    k_all = jnp.concatenate([kp_ref[...], kc_ref[...], kn_ref[...], kx_ref[...]], axis=0)
    v_all = jnp.concatenate([vp_ref[...], vc_ref[...], vn_ref[...], vx_ref[...]], axis=0)
    nkeys = k_all.shape[0]
    i = lax.broadcasted_iota(jnp.int32, (B_BLOCK, nkeys), 0)
    j = lax.broadcasted_iota(jnp.int32, (B_BLOCK, nkeys), 1)
    lat = n >= ctx_blocks
    pen_prev = jnp.where(n - 1 >= ctx_blocks, 0.0, NEG_INF)
    pen_cur = jnp.where(lat, 0.0, NEG_INF)
    pen_next = jnp.where(jnp.logical_and(lat, n + 1 <= nblk - 1), 0.0, NEG_INF)
    bias = jnp.where(j < B_BLOCK, jnp.where(j >= i, pen_prev, NEG_INF),
                     jnp.where(j < 2 * B_BLOCK, pen_cur,
                               jnp.where(j < 3 * B_BLOCK, jnp.where(j - 2 * B_BLOCK <= i, pen_next, NEG_INF), 0.0)))
    outs = []
    for hq in range(B_HEADS):
        g = hq // (B_HEADS // B_KV_HEADS)
        s = lax.dot_general(q[:, hq * B_DIM:(hq + 1) * B_DIM], k_all[:, g * B_DIM:(g + 1) * B_DIM],
                            (((1,), (1,)), ((), ())), preferred_element_type=F32)
        s = s + bias
        sk = sink_ref[hq]
        m = jnp.maximum(jnp.max(s, axis=1, keepdims=True), sk)
        p = jnp.exp(s - m)
        den = jnp.sum(p, axis=1, keepdims=True) + jnp.exp(sk - m)
        pv = jnp.dot(p.astype(BF16), v_all[:, g * B_DIM:(g + 1) * B_DIM], preferred_element_type=F32)
        outs.append(pv / den)
    o_ref[...] = jnp.concatenate(outs, axis=1)


def _attn_b_call(sink, qb, kb, vb, *, ctx, blk0):
    b, t, _ = qb.shape
    nblk = t // B_BLOCK
    ctx_blocks = ctx // B_BLOCK
    cur = lambda bi, i: (bi, i + blk0, 0)
    prev = lambda bi, i: (bi, jnp.maximum(i + blk0 - 1, 0), 0)
    nxt = lambda bi, i: (bi, jnp.minimum(i + blk0 + 1, nblk - 1), 0)
    kv = lambda im: pl.BlockSpec((None, B_BLOCK, B_KCOLS), im)
    cx = pl.BlockSpec((None, ctx, B_KCOLS), lambda bi, i: (bi, 0, 0))
    return pl.pallas_call(
        functools.partial(_attn_b_kernel, blk0=blk0, ctx_blocks=ctx_blocks, nblk=nblk),
        grid=(b, nblk - blk0),
        in_specs=[pl.BlockSpec(memory_space=pltpu.SMEM),
                  pl.BlockSpec((None, B_BLOCK, B_QCOLS), cur),
                  kv(prev), kv(cur), kv(nxt), cx, kv(prev), kv(cur), kv(nxt), cx],
        out_specs=pl.BlockSpec((None, B_BLOCK, B_QCOLS), lambda bi, i: (bi, i, 0)),
        out_shape=jax.ShapeDtypeStruct((b, (nblk - blk0) * B_BLOCK, B_QCOLS), F32),
        compiler_params=_params(("parallel", "parallel")),
        name="window_attention",
    )(sink, qb, kb, kb, kb, kb, vb, vb, vb, vb)


def _conv_kernel(u_ref, up_ref, un_ref, w_ref, b_ref, o_ref, *, tm, ctx_tiles, ntiles):
    i = pl.program_id(1)
    u = u_ref[...]
    has_prev = jnp.logical_and(i != 0, i != ctx_tiles)
    has_next = jnp.logical_and(i != ctx_tiles - 1, i != ntiles - 1)
    up = jnp.where(has_prev, up_ref[...], 0.0)
    un = jnp.where(has_next, un_ref[...], 0.0)
    full = jnp.concatenate([up, u, un], axis=0)
    w = w_ref[...]
    acc = b_ref[...] + w[0:1] * full[HALO - 2:HALO - 2 + tm]
    for k in range(1, C_CONV):
        acc = acc + w[k:k + 1] * full[HALO - 2 + k:HALO - 2 + k + tm]
    o_ref[...] = _silu(acc)


def _conv_call(xbc_raw, conv_w, conv_b, *, tm, ctx):
    b, t, ch = xbc_raw.shape
    ntiles = t // tm
    hb = tm // HALO
    nh = t // HALO
    w8 = jnp.zeros((8, ch), F32).at[:C_CONV].set(conv_w)
    return pl.pallas_call(
        functools.partial(_conv_kernel, tm=tm, ctx_tiles=ctx // tm, ntiles=ntiles),
        grid=(b, ntiles),
        in_specs=[pl.BlockSpec((None, tm, ch), lambda bi, i: (bi, i, 0)),
                  pl.BlockSpec((None, HALO, ch), lambda bi, i: (bi, jnp.maximum(i * hb - 1, 0), 0)),
                  pl.BlockSpec((None, HALO, ch), lambda bi, i: (bi, jnp.minimum((i + 1) * hb, nh - 1), 0)),
                  pl.BlockSpec((8, ch), lambda bi, i: (0, 0)),
                  pl.BlockSpec((1, ch), lambda bi, i: (0, 0))],
        out_specs=pl.BlockSpec((None, tm, ch), lambda bi, i: (bi, i, 0)),
        out_shape=jax.ShapeDtypeStruct((b, t, ch), F32),
        compiler_params=_params(("parallel", "parallel")),
        name="ssm_conv",
    )(xbc_raw, xbc_raw, xbc_raw, w8, conv_b.reshape(1, ch))


def _softplus(v):
    return jnp.maximum(v, 0.0) + jnp.log1p(jnp.exp(-jnp.abs(v)))


def _ssd_direction(x_ref, dt_ref, par, y_ref, h_scr, *, lane0, backward):
    xbc = x_ref[...]
    xs = xbc[:, :C_INNER]
    bm = xbc[:, C_INNER:C_INNER + C_GN]
    cm = xbc[:, C_INNER + C_GN:].astype(BF16)
    dt = _softplus(dt_ref[...] + par[0:1])
    a = dt * par[1:2]
    li = lax.broadcasted_iota(jnp.int32, (C_CHUNK, C_CHUNK), 0)
    si = lax.broadcasted_iota(jnp.int32, (C_CHUNK, C_CHUNK), 1)
    tri = (si <= li).astype(F32)
    cs = jnp.dot(tri, a, precision=HIGHEST, preferred_element_type=F32)
    tot = cs[C_CHUNK - 1:C_CHUNK, :]
    key = cs - a if backward else cs
    key_t = key.T
    bm_t = bm.T
    bm16 = bm.astype(BF16)
    mask = (si >= li) if backward else (si <= li)
    ys = []
    for g in range(C_GROUPS):
        gs = slice(g * C_STATE, (g + 1) * C_STATE)
        cb = lax.dot_general(cm[:, gs], bm16[:, gs], (((1,), (1,)), ((), ())), preferred_element_type=F32)
        for r in range(C_HPG):
            h = g * C_HPG + r
            hl = lane0 + h
            col = key[:, hl:hl + 1]
            row = key_t[hl:hl + 1, :]
            tot_h = tot[:, hl:hl + 1]
            d = (row - col) if backward else (col - row)
            decay = jnp.exp(jnp.where(mask, d, NEG_INF))
            xdt = (xs[:, h * C_HEAD_DIM:(h + 1) * C_HEAD_DIM] * dt[:, hl:hl + 1]).astype(BF16)
            y_diag = jnp.dot((cb * decay).astype(BF16), xdt, preferred_element_type=F32)
            h_prev = h_scr[h]
            off_scale = jnp.exp(tot_h - col) if backward else jnp.exp(col)
            y_off = jnp.dot(cm[:, gs], h_prev.astype(BF16), preferred_element_type=F32) * off_scale
            ys.append(y_diag + y_off)
            w_row = jnp.exp(row) if backward else jnp.exp(tot_h - row)
            bw_t = (bm_t[gs, :] * w_row).astype(BF16)
            h_scr[h] = jnp.exp(tot_h) * h_prev + jnp.dot(bw_t, xdt, preferred_element_type=F32)
    y_ref[...] = jnp.concatenate(ys, axis=1)


def _ssd_kernel(par_ref, xf_ref, dtf_ref, xb_ref, dtb_ref, yf_ref, yb_ref, hf_scr, hb_scr):
    @pl.when(pl.program_id(1) == 0)
    def _():
        hf_scr[...] = jnp.zeros(hf_scr.shape, F32)
        hb_scr[...] = jnp.zeros(hb_scr.shape, F32)

    par = par_ref[...]
    _ssd_direction(xf_ref, dtf_ref, par, yf_ref, hf_scr, lane0=0, backward=False)
    _ssd_direction(xb_ref, dtb_ref, par, yb_ref, hb_scr, lane0=C_HEADS, backward=True)


def _ssd_call(par, xbc, dt, *, ctx):
    b, t, ch = xbc.shape
    nc = t // C_CHUNK
    ncc = ctx // C_CHUNK
    fwd = lambda bi, j: (bi, j, 0)
    bwd = lambda bi, j: (bi, jnp.where(j < ncc, ncc - 1 - j, nc - 1 + ncc - j), 0)
    return pl.pallas_call(
        _ssd_kernel,
        grid=(b, nc),
        in_specs=[pl.BlockSpec((8, DT_LANES), lambda bi, j: (0, 0)),
                  pl.BlockSpec((None, C_CHUNK, ch), fwd), pl.BlockSpec((None, C_CHUNK, DT_LANES), fwd),
                  pl.BlockSpec((None, C_CHUNK, ch), bwd), pl.BlockSpec((None, C_CHUNK, DT_LANES), bwd)],
        out_specs=[pl.BlockSpec((None, C_CHUNK, C_INNER), fwd), pl.BlockSpec((None, C_CHUNK, C_INNER), bwd)],
        out_shape=[jax.ShapeDtypeStruct((b, t, C_INNER), F32)] * 2,
        scratch_shapes=[pltpu.VMEM((C_HEADS, C_STATE, C_HEAD_DIM), F32)] * 2,
        compiler_params=_params(("parallel", "arbitrary")),
        name="ssd_scan",
    )(par, xbc, dt, xbc, dt)


def _outproj_kernel(x_ref, mod_ref, oa_ref, ob_ref, yf_ref, yb_ref, xs_ref, z_ref, dg_ref, w_ref, o_ref):
    m = mod_ref[...]
    dg = dg_ref[...]
    y = (yf_ref[...] + yb_ref[...] + dg[0:1] * xs_ref[...]) * _silu(z_ref[...])
    gw = C_INNER // C_GROUPS
    oc = []
    for g in range(C_GROUPS):
        yg = y[:, g * gw:(g + 1) * gw]
        ms = jnp.mean(yg * yg, axis=1, keepdims=True)
        oc.append(yg * lax.rsqrt(ms + EPS) * dg[1:2, g * gw:(g + 1) * gw])
    mix = jnp.concatenate([oa_ref[...], ob_ref[...]] + oc, axis=1).astype(BF16)
    o_ref[...] = x_ref[...] + m[5:6] * jnp.dot(mix, w_ref[...], preferred_element_type=F32)


def _outproj_call(xs, mod, oa, ob, yf, yb, xbc, z, dg, w_out, *, tm, ctx_tiles, tile0, nbatch):
    b, t, d = xs.shape
    ntiles = t // tm - tile0
    tok = lambda w: pl.BlockSpec((None, tm, w), lambda bi, i: (bi, i + tile0, 0))
    tok0 = lambda w: pl.BlockSpec((None, tm, w), lambda bi, i: (bi, i, 0))
    mod_row = lambda bi, i: (jnp.where(i + tile0 < ctx_tiles, nbatch, bi), 0, 0)
    return pl.pallas_call(
        _outproj_kernel,
        grid=(b, ntiles),
        in_specs=[tok(d), pl.BlockSpec((None, N_MOD, d), mod_row),
                  tok0(A_HEADS * A_V), tok0(B_QCOLS), tok(C_INNER), tok(C_INNER), tok(C_INNER), tok(C_INNER),
                  pl.BlockSpec(dg.shape, lambda bi, i: (0, 0)),
                  pl.BlockSpec(w_out.shape, lambda bi, i: (0, 0), pipeline_mode=pl.Buffered(1))],
        out_specs=tok0(d),
        out_shape=jax.ShapeDtypeStruct((b, ntiles * tm, d), F32),
        compiler_params=_params(("parallel", "parallel")),
        name="mix_out_proj",
    )(xs, mod, oa, ob, yf, yb, xbc, z, dg, w_out)


def _rope_tables(seq, ctx, dim, reps):
    rows = seq // GRID_W
    row = jnp.repeat(jnp.arange(rows, dtype=F32), GRID_W)
    col = jnp.tile(jnp.arange(GRID_W, dtype=F32), rows)
    quarter = dim // 4
    inv_freq = ROPE_BASE ** (-jnp.arange(quarter, dtype=F32) / quarter)
    ar = row[:, None] * inv_freq
    ac = col[:, None] * inv_freq
    ang = jnp.concatenate([ar, ar, ac, ac], axis=-1)
    sign = jnp.where((jnp.arange(dim) % (dim // 2)) < quarter, -1.0, 1.0).astype(F32)
    cos = jnp.concatenate([jnp.ones((ctx, dim), F32), jnp.cos(ang)], axis=0)
    sin = jnp.concatenate([jnp.zeros((ctx, dim), F32), jnp.sin(ang) * sign], axis=0)
    return jnp.tile(cos, (1, reps)), jnp.tile(sin, (1, reps))


def _group_matrix(n, group):
    idx = jnp.arange(n) // group
    return (idx[:, None] == idx[None, :]).astype(BF16)


def kernel(x, c, ctx, c_ctx, w_mod, b_mod, norm_ffn1, ffn1_w13, ffn1_w2, norm_mix, w_in, w_out, qn_a, kn_a, lam_q1, lam_k1, lam_q2, lam_k2, subln_a, qn_b, kn_b, sink_b, conv_w, conv_b, dt_bias, a_log, d_skip, gnorm_c, norm_ffn2, ffn2_w13, ffn2_w2):
    nb, seq, d = x.shape
    nctx = ctx.shape[1]
    t = nctx + seq
    depth = w_mod.shape[0]
    tm = 256
    fc = 256
    tq, tk = 256, (768 if t % 768 == 0 and t > 768 else 256)
    assert nctx % tm == 0 and t % tm == 0 and nctx % C_CHUNK == 0 and seq % GRID_W == 0 and nb < 8
    ctx_tiles = nctx // tm

    xs = x
    cvec = jnp.zeros((8, d), F32).at[:nb].set(c).at[nb].set(c_ctx)
    mod = _mod_call(cvec, w_mod, b_mod).reshape(depth, 8, N_MOD, d)

    cosa, sina = _rope_tables(seq, nctx, A_QK, ROPE_LANES // A_QK)
    cosb, sinb = _rope_tables(seq, nctx, B_DIM, ROPE_LANES // B_DIM)
    ga = _group_matrix(A_QCOLS, A_QK)
    gb = _group_matrix(B_QCOLS, B_DIM)
    src = jnp.arange(A_QCOLS)
    dst = (src // (2 * A_QK)) * A_HLANES + src % (2 * A_QK)
    kpl = (dst[:, None] == jnp.arange(A_HEADS * A_HLANES)[None, :]).astype(BF16)
    kone = (jnp.arange(A_HEADS * A_HLANES) % A_HLANES == 2 * A_QK).astype(F32).reshape(1, -1)
    in_cols = w_in.shape[2]
    in_pad = (-in_cols) % DT_LANES

    for l in range(depth):
        last = l == depth - 1
        tile0 = ctx_tiles if last else 0
        lam_init = 0.8 - 0.6 * math.exp(-0.3 * l)
        w_in_p = jnp.pad(w_in[l], ((0, 0), (0, in_pad))).astype(BF16)
        hn = jnp.stack([jnp.tile(qn_a[l], A_QCOLS // A_QK), jnp.tile(kn_a[l], A_QCOLS // A_QK),
                        jnp.tile(qn_b[l], B_QCOLS // B_DIM), jnp.tile(kn_b[l], B_QCOLS // B_DIM)]).astype(F32)
        hn = jnp.zeros((8, A_QCOLS), F32).at[:4].set(hn)

        xs = _ffn_call(xs, mod[l], norm_ffn1[l], ffn1_w13[l].astype(BF16), ffn1_w2[l].astype(BF16), si=0, tm=tm, fc=fc,
                       ctx_tiles=ctx_tiles, nbatch=nb, ctx_in=(ctx if l == 0 else None))

        qa_t, ka_x, va_t, qb, kb, vb, z, xbc_raw, dt = _inproj_call(
            xs, mod[l], norm_mix[l], w_in_p, ga, gb, hn, kpl, kone, cosa, sina, cosb, sinb,
            tm=tm, ctx_tiles=ctx_tiles, nbatch=nb)

        lamv = jnp.zeros((8, A_QK), F32).at[:4].set(jnp.stack([lam_q1[l], lam_k1[l], lam_q2[l], lam_k2[l]]))
        oa = _attn_a_call(lamv, subln_a[l].reshape(A_V, 1), qa_t, ka_x, va_t, tq=tq, tk=tk, ctx=nctx,
                          q_tile0=(nctx // tq if last else 0), lam_init=lam_init)

        ob = _attn_b_call(sink_b[l].astype(F32), qb, kb, vb, ctx=nctx, blk0=(nctx // B_BLOCK if last else 0))

        xbc = _conv_call(xbc_raw, conv_w[l], conv_b[l], tm=tm, ctx=nctx)
        par = (jnp.zeros((8, DT_LANES), F32)
               .at[0, :2 * C_HEADS].set(dt_bias[l].reshape(-1))
               .at[1, :2 * C_HEADS].set(-jnp.exp(a_log[l].astype(F32)).reshape(-1)))
        yf, yb = _ssd_call(par, xbc, dt, ctx=nctx)

        dg = (jnp.zeros((8, C_INNER), F32)
              .at[0].set(jnp.repeat(d_skip[l], C_HEAD_DIM)).at[1].set(gnorm_c[l]))
        xs = _outproj_call(xs, mod[l], oa, ob, yf, yb, xbc, z, dg, w_out[l].astype(BF16),
                           tm=tm, ctx_tiles=ctx_tiles, tile0=tile0, nbatch=nb)
        xs = _ffn_call(xs, mod[l], norm_ffn2[l], ffn2_w13[l].astype(BF16), ffn2_w2[l].astype(BF16), si=6, tm=tm, fc=fc,
                       ctx_tiles=(0 if last else ctx_tiles), nbatch=nb)
    return xs
```

```python
import functools
import math

import jax
import jax.numpy as jnp
from jax import lax
from jax.experimental import pallas as pl
from jax.experimental.pallas import tpu as pltpu

F32 = jnp.float32
BF16 = jnp.bfloat16
HIGHEST = lax.Precision.HIGHEST
NEG_INF = float("-inf")

EPS = 1e-6
ROPE_BASE = 10000.0
GRID_W = 64
ROPE_LANES = 128
N_MOD = 9

A_HEADS, A_QK, A_V = 4, 32, 64
A_QCOLS = A_HEADS * 2 * A_QK
A_SCALE = A_QK ** -0.5
LOG2E = math.log2(math.e)
A_HLANES = 128
A_VROWS = A_V + 16
A_SHIFT_CAP = 48.0
B_HEADS, B_KV_HEADS, B_DIM = 4, 2, 64
B_QCOLS = B_HEADS * B_DIM
B_KCOLS = B_KV_HEADS * B_DIM
B_SCALE = B_DIM ** -0.5
B_BLOCK = 128
C_HEADS, C_HEAD_DIM, C_GROUPS, C_STATE, C_CONV = 8, 64, 2, 64, 5
C_INNER = C_HEADS * C_HEAD_DIM
C_GN = C_GROUPS * C_STATE
C_XBC = C_INNER + 2 * C_GN
C_CHUNK = 128
C_HPG = C_HEADS // C_GROUPS
DT_LANES = 128
HALO = 8

V7X_VMEM_LIMIT = 56 * 1024 * 1024


def _params(sem, vmem=V7X_VMEM_LIMIT):
    return pltpu.CompilerParams(dimension_semantics=sem, vmem_limit_bytes=vmem)


def _rms_mod(x, nw, shift, scale):
    ms = jnp.mean(x * x, axis=-1, keepdims=True)
    y = x * lax.rsqrt(ms + EPS) * nw
    return y * (1.0 + scale) + shift


def _silu(v):
    return v * jax.nn.sigmoid(v)


def _mod_kernel(c_ref, w_ref, b_ref, o_ref):
    sc = _silu(c_ref[...])
    o_ref[...] = jnp.dot(sc, w_ref[...], precision=HIGHEST, preferred_element_type=F32) + b_ref[...]


def _mod_call(cvec, w_mod, b_mod, tn=1152):
    nl, d, n = w_mod.shape
    return pl.pallas_call(
        _mod_kernel,
        grid=(nl, n // tn),
        in_specs=[pl.BlockSpec((8, d), lambda l, j: (0, 0)),
                  pl.BlockSpec((None, d, tn), lambda l, j: (l, 0, j)),
                  pl.BlockSpec((None, 1, tn), lambda l, j: (l, 0, j))],
        out_specs=pl.BlockSpec((None, 8, tn), lambda l, j: (l, 0, j)),
        out_shape=jax.ShapeDtypeStruct((nl, 8, n), F32),
        compiler_params=_params(("parallel", "parallel")),
        name="adaln_mod",
    )(cvec, w_mod, b_mod.reshape(nl, 1, n))


def _ffn_kernel(*refs, si, fc, ff, ctx_tiles, split_input):
    if split_input:
        c_ref, x_ref, mod_ref, nw_ref, w13_ref, w2_ref, o_ref, a_scr = refs
        x = jnp.where(pl.program_id(1) < ctx_tiles, c_ref[...], x_ref[...])
    else:
        x_ref, mod_ref, nw_ref, w13_ref, w2_ref, o_ref, a_scr = refs
        x = x_ref[...]
    m = mod_ref[...]
    h = _rms_mod(x, nw_ref[...], m[si:si + 1], m[si + 1:si + 2]).astype(BF16)
    for c in range(ff // fc):
        g = jnp.dot(h, w13_ref[:, c * fc:(c + 1) * fc], preferred_element_type=F32)
        u = jnp.dot(h, w13_ref[:, ff + c * fc:ff + (c + 1) * fc], preferred_element_type=F32)
        a_scr[:, c * fc:(c + 1) * fc] = (_silu(g) * u).astype(BF16)
    y = jnp.dot(a_scr[...], w2_ref[...], preferred_element_type=F32)
    o_ref[...] = x + (0.5 * m[si + 2:si + 3]) * y


def _ffn_call(xs, mod, nw, w13, w2, *, si, tm, fc, ctx_tiles, nbatch, ctx_in=None):
    b, t, d = xs.shape
    ff = w2.shape[0]
    split = ctx_in is not None
    ntiles = t // tm + (ctx_tiles if split else 0)
    mod_row = lambda bi, i: (jnp.where(i < ctx_tiles, nbatch, bi), 0, 0)
    if split:
        tok_specs = [pl.BlockSpec((None, tm, d), lambda bi, i: (bi, jnp.minimum(i, ctx_tiles - 1), 0)),
                     pl.BlockSpec((None, tm, d), lambda bi, i: (bi, jnp.maximum(i - ctx_tiles, 0), 0))]
        toks = (ctx_in, xs)
    else:
        tok_specs = [pl.BlockSpec((None, tm, d), lambda bi, i: (bi, i, 0))]
        toks = (xs,)
    return pl.pallas_call(
        functools.partial(_ffn_kernel, si=si, fc=fc, ff=ff, ctx_tiles=ctx_tiles, split_input=split),
        grid=(b, ntiles),
        in_specs=tok_specs + [pl.BlockSpec((None, N_MOD, d), mod_row),
                              pl.BlockSpec((1, d), lambda bi, i: (0, 0)),
                              pl.BlockSpec(w13.shape, lambda bi, i: (0, 0), pipeline_mode=pl.Buffered(1)),
                              pl.BlockSpec(w2.shape, lambda bi, i: (0, 0), pipeline_mode=pl.Buffered(1))],
        out_specs=pl.BlockSpec((None, tm, d), lambda bi, i: (bi, i, 0)),
        out_shape=jax.ShapeDtypeStruct((b, ntiles * tm, d), F32),
        scratch_shapes=[pltpu.VMEM((tm, ff), BF16)],
        compiler_params=_params(("parallel", "parallel")),
        name="swiglu_half",
    )(*toks, mod, nw.reshape(1, d), w13, w2)


def _rope(v, cos, sin_signed, quarter):
    n = v.shape[-1]
    lane = lax.broadcasted_iota(jnp.int32, v.shape, 1)
    first = (lane & (2 * quarter - 1)) < quarter
    vr = jnp.where(first, pltpu.roll(v, n - quarter, 1), pltpu.roll(v, quarter, 1))
    return v * cos + vr * sin_signed


def _group_norm(v, gmat, inv_n, w):
    sq = v * v
    hi = sq.astype(BF16)
    lo = (sq - hi.astype(F32)).astype(BF16)
    ms = (jnp.dot(hi, gmat, preferred_element_type=F32) + jnp.dot(lo, gmat, preferred_element_type=F32)) * inv_n
    return v * lax.rsqrt(ms + EPS) * w


def _inproj_kernel(x_ref, mod_ref, nw_ref, w_ref, ga_ref, gb_ref, hn_ref, kpl_ref, kone_ref,
                   cosa_ref, sina_ref, cosb_ref, sinb_ref, qa_o, ka_o, va_o, qb_o, kb_o, vb_o, z_o, xbc_o, dt_o):
    m = mod_ref[...]
    h = _rms_mod(x_ref[...], nw_ref[...], m[3:4], m[4:5]).astype(BF16)
    p = jnp.dot(h, w_ref[...], preferred_element_type=F32)
    hn = hn_ref[...]
    ga, gb = ga_ref[...], gb_ref[...]
    twice = lambda a: jnp.concatenate([a, a], axis=1)
    cosa, sina, cosb, sinb = twice(cosa_ref[...]), twice(sina_ref[...]), twice(cosb_ref[...]), twice(sinb_ref[...])
    o = 0
    qa = _group_norm(p[:, o:o + A_QCOLS], ga, 1.0 / A_QK, hn[0:1]); o += A_QCOLS
    ka = _group_norm(p[:, o:o + A_QCOLS], ga, 1.0 / A_QK, hn[1:2]); o += A_QCOLS
    qa_o[...] = (_rope(qa, cosa, sina, A_QK // 4) * (A_SCALE * LOG2E)).T.astype(BF16)
    ka16 = _rope(ka, cosa, sina, A_QK // 4).astype(BF16)
    ka_o[...] = (jnp.dot(ka16, kpl_ref[...], preferred_element_type=F32) + kone_ref[...]).astype(BF16)
    va_o[...] = p[:, o:o + A_HEADS * A_V].T.astype(BF16); o += A_HEADS * A_V
    qb = _group_norm(p[:, o:o + B_QCOLS], gb, 1.0 / B_DIM, hn[2:3]); o += B_QCOLS
    kb = _group_norm(p[:, o:o + B_KCOLS], gb[:B_KCOLS, :B_KCOLS], 1.0 / B_DIM, hn[3:4, :B_KCOLS]); o += B_KCOLS
    qb_o[...] = (_rope(qb, cosb, sinb, B_DIM // 4) * B_SCALE).astype(BF16)
    kb_o[...] = _rope(kb, cosb[:, :B_KCOLS], sinb[:, :B_KCOLS], B_DIM // 4).astype(BF16)
    vb_o[...] = p[:, o:o + B_KCOLS].astype(BF16); o += B_KCOLS
    z_o[...] = p[:, o:o + C_INNER]; o += C_INNER
    xbc_o[...] = p[:, o:o + C_XBC]; o += C_XBC
    dt_o[...] = p[:, o:o + DT_LANES]


def _inproj_call(xs, mod, nw, w_in_p, ga, gb, hn, kpl, kone, cosa, sina, cosb, sinb, *, tm, ctx_tiles, nbatch):
    b, t, d = xs.shape
    ntiles = t // tm
    tok = lambda w: pl.BlockSpec((None, tm, w), lambda bi, i: (bi, i, 0))
    tab = lambda w: pl.BlockSpec((tm, w), lambda bi, i: (i, 0))
    full = lambda a: pl.BlockSpec(a.shape, lambda bi, i: (0,) * a.ndim)
    mod_row = lambda bi, i: (jnp.where(i < ctx_tiles, nbatch, bi), 0, 0)
    widths = (A_QCOLS, A_HEADS * A_HLANES, A_HEADS * A_V, B_QCOLS, B_KCOLS, B_KCOLS, C_INNER, C_XBC, DT_LANES)
    transposed = (0, 2)
    dtypes = (BF16,) * 6 + (F32,) * 3
    return pl.pallas_call(
        _inproj_kernel,
        grid=(b, ntiles),
        in_specs=[tok(d), pl.BlockSpec((None, N_MOD, d), mod_row), pl.BlockSpec((1, d), lambda bi, i: (0, 0)),
                  pl.BlockSpec(w_in_p.shape, lambda bi, i: (0, 0), pipeline_mode=pl.Buffered(1)),
                  full(ga), full(gb), full(hn), full(kpl), full(kone)] + [tab(ROPE_LANES)] * 4,
        out_specs=[pl.BlockSpec((None, w, tm), lambda bi, i: (bi, 0, i)) if k in transposed else tok(w)
                   for k, w in enumerate(widths)],
        out_shape=[jax.ShapeDtypeStruct((b, w, t) if k in transposed else (b, t, w), dt)
                   for k, (w, dt) in enumerate(zip(widths, dtypes))],
        compiler_params=_params(("parallel", "parallel")),
        name="in_proj_heads",
    )(xs, mod, nw.reshape(1, d), w_in_p, ga, gb, hn, kpl, kone, cosa, sina, cosb, sinb)


def _attn_a_kernel(lamv_ref, subln_ref, qt_ref, kx_ref, vt_ref, o_ref, kmax_scr, *,
                   tq, tk, ctx, nk, q_tile0, lam_init):
    step = pl.program_id(1)

    @pl.when(step == 0)
    def _():
        r = lax.broadcasted_iota(jnp.int32, (A_HLANES, A_HLANES), 0)
        c = lax.broadcasted_iota(jnp.int32, (A_HLANES, A_HLANES), 1)
        in_comp = jnp.logical_and(r >= c * A_QK, r < (c + 1) * A_QK)
        sel = jnp.where(jnp.logical_and(in_comp, c < 2), 1.0, 0.0).astype(BF16)
        for h in range(A_HEADS):
            kk = kx_ref[:, h * A_HLANES:(h + 1) * A_HLANES].astype(F32)
            ksq = jnp.dot((kk * kk).astype(BF16), sel, preferred_element_type=F32)
            kmax_scr[h:h + 1, :] = jnp.sqrt(jnp.max(ksq, axis=0, keepdims=True))

    lv = lamv_ref[...]
    lam = (jnp.exp(jnp.sum(lv[0:1] * lv[1:2], axis=1, keepdims=True))
           - jnp.exp(jnp.sum(lv[2:3] * lv[3:4], axis=1, keepdims=True)) + lam_init)
    qt = qt_ref[...].astype(F32)
    qcs = [qt[hc * A_QK:(hc + 1) * A_QK, :] for hc in range(2 * A_HEADS)]
    shifts = [jnp.sqrt(jnp.sum(qc * qc, axis=0, keepdims=True)) * kmax_scr[hc // 2:hc // 2 + 1, hc % 2:hc % 2 + 1]
              for hc, qc in enumerate(qcs)]
    shift_max = jnp.max(functools.reduce(jnp.maximum, shifts))

    def q_ext(h, shifted):
        z = jnp.zeros((A_QK, tq), F32)
        top = jnp.concatenate([qcs[2 * h], z], axis=1)
        mid = jnp.concatenate([z, qcs[2 * h + 1]], axis=1)
        row = lax.broadcasted_iota(jnp.int32, (A_HLANES - 2 * A_QK, 2 * tq), 0)
        if shifted:
            bot = jnp.where(row == 0, -jnp.concatenate([shifts[2 * h], shifts[2 * h + 1]], axis=1), 0.0)
        else:
            bot = jnp.zeros(row.shape, F32)
        return jnp.concatenate([top, mid, bot], axis=0).astype(BF16)

    def scores(qx, h, start, size):
        return jnp.dot(kx_ref[pl.ds(start, size), h * A_HLANES:(h + 1) * A_HLANES], qx, preferred_element_type=F32)

    def v_ext(h, start, size):
        ones = jnp.ones((A_VROWS - A_V, size), BF16)
        return jnp.concatenate([vt_ref[h * A_V:(h + 1) * A_V, pl.ds(start, size)], ones], axis=0)

    def finish(accs):
        rows = []
        for acc in accs:
            o = (acc[:A_V, :tq] / acc[A_V:A_V + 1, :tq]) - lam * (acc[:A_V, tq:] / acc[A_V:A_V + 1, tq:])
            ms = jnp.mean(o * o, axis=0, keepdims=True)
            rows.append(o * lax.rsqrt(ms + EPS) * subln_ref[...] * (1.0 - lam_init))
        o_ref[...] = jnp.concatenate(rows, axis=0).T

    def attend(shifted):
        qx = [q_ext(h, shifted) for h in range(A_HEADS)]

        def head_step(h, start, carry):
            s = scores(qx[h], h, start, tk)
            ve = v_ext(h, start, tk)
            if shifted:
                return carry + jnp.dot(ve, jnp.exp2(s).astype(BF16), preferred_element_type=F32)
            m_run, acc = carry
            m_new = jnp.maximum(m_run, jnp.max(s, axis=0, keepdims=True))
            p = jnp.exp2(s - m_new).astype(BF16)
            return m_new, jnp.exp2(m_run - m_new) * acc + jnp.dot(ve, p, preferred_element_type=F32)

        def body(j, carries):
            start = pl.multiple_of(j * tk, tk)
            return tuple(head_step(h, start, carries[h]) for h in range(A_HEADS))

        acc0 = jnp.zeros((A_VROWS, 2 * tq), F32)
        if shifted:
            accs = lax.fori_loop(0, nk, body, (acc0,) * A_HEADS)
        else:
            init = (jnp.full((1, 2 * tq), NEG_INF, F32), acc0)
            accs = [c[1] for c in lax.fori_loop(0, nk, body, (init,) * A_HEADS)]
        finish(accs)

    def attend_ctx():
        accs = []
        for h in range(A_HEADS):
            s = scores(q_ext(h, False), h, 0, ctx)
            p = jnp.exp2(s - jnp.max(s, axis=0, keepdims=True)).astype(BF16)
            accs.append(jnp.dot(v_ext(h, 0, ctx), p, preferred_element_type=F32))
        finish(accs)

    safe = shift_max < A_SHIFT_CAP
    if q_tile0 * tq < ctx:
        is_ctx = (step + q_tile0) * tq < ctx
        pl.when(is_ctx)(attend_ctx)
        pl.when(jnp.logical_and(jnp.logical_not(is_ctx), safe))(lambda: attend(True))
        pl.when(jnp.logical_and(jnp.logical_not(is_ctx), jnp.logical_not(safe)))(lambda: attend(False))
    else:
        pl.when(safe)(lambda: attend(True))
        pl.when(jnp.logical_not(safe))(lambda: attend(False))


def _attn_a_call(lamv, subln_col, qa_t, ka_x, va_t, *, tq, tk, ctx, q_tile0, lam_init):
    b, w, t = qa_t.shape
    nq = t // tq - q_tile0
    return pl.pallas_call(
        functools.partial(_attn_a_kernel, tq=tq, tk=tk, ctx=ctx, nk=t // tk, q_tile0=q_tile0, lam_init=lam_init),
        grid=(b, nq),
        in_specs=[pl.BlockSpec(lamv.shape, lambda bi, qi: (0, 0)),
                  pl.BlockSpec(subln_col.shape, lambda bi, qi: (0, 0)),
                  pl.BlockSpec((None, w, tq), lambda bi, qi: (bi, 0, qi + q_tile0)),
                  pl.BlockSpec((None, t, A_HEADS * A_HLANES), lambda bi, qi: (bi, 0, 0)),
                  pl.BlockSpec((None, w, t), lambda bi, qi: (bi, 0, 0))],
        out_specs=pl.BlockSpec((None, tq, w), lambda bi, qi: (bi, qi, 0)),
        out_shape=jax.ShapeDtypeStruct((b, nq * tq, w), F32),
        scratch_shapes=[pltpu.VMEM((8, A_HLANES), F32)],
        compiler_params=_params(("parallel", "arbitrary")),
        name="diff_attention",
    )(lamv, subln_col, qa_t, ka_x, va_t)


def _attn_b_kernel(sink_ref, q_ref, kp_ref, kc_ref, kn_ref, kx_ref, vp_ref, vc_ref, vn_ref, vx_ref, o_ref, *,
                   blk0, ctx_blocks, nblk):
    n = pl.program_id(1) + blk0
    q = q_ref[...]
    k_all = jnp.concatenate([kp_ref[...], kc_ref[...], kn_ref[...], kx_ref[...]], axis=0)
    v_all = jnp.concatenate([vp_ref[...], vc_ref[...], vn_ref[...], vx_ref[...]], axis=0)
    nkeys = k_all.shape[0]
    i = lax.broadcasted_iota(jnp.int32, (B_BLOCK, nkeys), 0)
    j = lax.broadcasted_iota(jnp.int32, (B_BLOCK, nkeys), 1)
    lat = n >= ctx_blocks
    pen_prev = jnp.where(n - 1 >= ctx_blocks, 0.0, NEG_INF)
    pen_cur = jnp.where(lat, 0.0, NEG_INF)
    pen_next = jnp.where(jnp.logical_and(lat, n + 1 <= nblk - 1), 0.0, NEG_INF)
    bias = jnp.where(j < B_BLOCK, jnp.where(j >= i, pen_prev, NEG_INF),
                     jnp.where(j < 2 * B_BLOCK, pen_cur,
                               jnp.where(j < 3 * B_BLOCK, jnp.where(j - 2 * B_BLOCK <= i, pen_next, NEG_INF), 0.0)))
    outs = []
    for hq in range(B_HEADS):
        g = hq // (B_HEADS // B_KV_HEADS)
        s = lax.dot_general(q[:, hq * B_DIM:(hq + 1) * B_DIM], k_all[:, g * B_DIM:(g + 1) * B_DIM],
                            (((1,), (1,)), ((), ())), preferred_element_type=F32)
        s = s + bias
        sk = sink_ref[hq]
        m = jnp.maximum(jnp.max(s, axis=1, keepdims=True), sk)
        p = jnp.exp(s - m)
        den = jnp.sum(p, axis=1, keepdims=True) + jnp.exp(sk - m)
        pv = jnp.dot(p.astype(BF16), v_all[:, g * B_DIM:(g + 1) * B_DIM], preferred_element_type=F32)
        outs.append(pv / den)
    o_ref[...] = jnp.concatenate(outs, axis=1)


def _attn_b_call(sink, qb, kb, vb, *, ctx, blk0):
    b, t, _ = qb.shape
    nblk = t // B_BLOCK
    ctx_blocks = ctx // B_BLOCK
    cur = lambda bi, i: (bi, i + blk0, 0)
    prev = lambda bi, i: (bi, jnp.maximum(i + blk0 - 1, 0), 0)
    nxt = lambda bi, i: (bi, jnp.minimum(i + blk0 + 1, nblk - 1), 0)
    kv = lambda im: pl.BlockSpec((None, B_BLOCK, B_KCOLS), im)
    cx = pl.BlockSpec((None, ctx, B_KCOLS), lambda bi, i: (bi, 0, 0))
    return pl.pallas_call(
        functools.partial(_attn_b_kernel, blk0=blk0, ctx_blocks=ctx_blocks, nblk=nblk),
        grid=(b, nblk - blk0),
        in_specs=[pl.BlockSpec(memory_space=pltpu.SMEM),
                  pl.BlockSpec((None, B_BLOCK, B_QCOLS), cur),
                  kv(prev), kv(cur), kv(nxt), cx, kv(prev), kv(cur), kv(nxt), cx],
        out_specs=pl.BlockSpec((None, B_BLOCK, B_QCOLS), lambda bi, i: (bi, i, 0)),
        out_shape=jax.ShapeDtypeStruct((b, (nblk - blk0) * B_BLOCK, B_QCOLS), F32),
        compiler_params=_params(("parallel", "parallel")),
        name="window_attention",
    )(sink, qb, kb, kb, kb, kb, vb, vb, vb, vb)


def _conv_kernel(u_ref, up_ref, un_ref, w_ref, b_ref, xt_o, bn_o, *, tm, ctx_tiles, ntiles):
    i = pl.program_id(1)
    u = u_ref[...]
    has_prev = jnp.logical_and(i != 0, i != ctx_tiles)
    has_next = jnp.logical_and(i != ctx_tiles - 1, i != ntiles - 1)
    up = jnp.where(has_prev, up_ref[...], 0.0)
    un = jnp.where(has_next, un_ref[...], 0.0)
    full = jnp.concatenate([up, u, un], axis=0)
    w = w_ref[...]
    acc = b_ref[...] + w[0:1] * full[HALO - 2:HALO - 2 + tm]
    for k in range(1, C_CONV):
        acc = acc + w[k:k + 1] * full[HALO - 2 + k:HALO - 2 + k + tm]
    y = _silu(acc)
    xt_o[...] = y.T
    bn_o[...] = y[:, C_INNER:C_INNER + C_GN]


def _conv_call(xbc_raw, conv_w, conv_b, *, tm, ctx):
    b, t, ch = xbc_raw.shape
    ntiles = t // tm
    hb = tm // HALO
    nh = t // HALO
    w8 = jnp.zeros((8, ch), F32).at[:C_CONV].set(conv_w)
    return pl.pallas_call(
        functools.partial(_conv_kernel, tm=tm, ctx_tiles=ctx // tm, ntiles=ntiles),
        grid=(b, ntiles),
        in_specs=[pl.BlockSpec((None, tm, ch), lambda bi, i: (bi, i, 0)),
                  pl.BlockSpec((None, HALO, ch), lambda bi, i: (bi, jnp.maximum(i * hb - 1, 0), 0)),
                  pl.BlockSpec((None, HALO, ch), lambda bi, i: (bi, jnp.minimum((i + 1) * hb, nh - 1), 0)),
                  pl.BlockSpec((8, ch), lambda bi, i: (0, 0)),
                  pl.BlockSpec((1, ch), lambda bi, i: (0, 0))],
        out_specs=[pl.BlockSpec((None, ch, tm), lambda bi, i: (bi, 0, i)),
                   pl.BlockSpec((None, tm, C_GN), lambda bi, i: (bi, i, 0))],
        out_shape=[jax.ShapeDtypeStruct((b, ch, t), F32), jax.ShapeDtypeStruct((b, t, C_GN), F32)],
        compiler_params=_params(("parallel", "parallel")),
        name="ssm_conv",
    )(xbc_raw, xbc_raw, xbc_raw, w8, conv_b.reshape(1, ch))


def _softplus(v):
    return jnp.maximum(v, 0.0) + jnp.log1p(jnp.exp(-jnp.abs(v)))


def _split3(v):
    v1 = v.astype(BF16)
    r1 = v - v1.astype(F32)
    v2 = r1.astype(BF16)
    v3 = (r1 - v2.astype(F32)).astype(BF16)
    return v1, v2, v3


def _ssd_direction(xt_ref, bn_ref, dt_ref, par, dsk_scr, y_ref, h_scr, *, lane0, backward):
    xt_all = xt_ref[...]
    x_t = xt_all[:C_INNER]
    c_t = xt_all[C_INNER + C_GN:]
    bn = bn_ref[...]
    dt = _softplus(dt_ref[...] + par[0:1])
    a = dt * par[1:2]
    s_idx = lax.broadcasted_iota(jnp.int32, (C_CHUNK, C_CHUNK), 0)
    l_idx = lax.broadcasted_iota(jnp.int32, (C_CHUNK, C_CHUNK), 1)
    tri = jnp.where(l_idx <= s_idx, 1.0, 0.0).astype(BF16)
    cs = sum(jnp.dot(tri, part, preferred_element_type=F32) for part in _split3(a))
    key = cs - a if backward else cs
    dt_t = dt.T
    key_t = key.T
    k1, k2, k3 = _split3(key)
    parts = (k1.astype(F32) + pltpu.roll(k2.astype(F32), 2 * C_HEADS, 1)
             + pltpu.roll(k3.astype(F32), 4 * C_HEADS, 1)).astype(BF16)
    src = lax.broadcasted_iota(jnp.int32, (DT_LANES, C_HEADS * C_CHUNK), 0)
    dst = lax.broadcasted_iota(jnp.int32, (DT_LANES, C_HEADS * C_CHUNK), 1)
    pick = jnp.logical_and(src < 6 * C_HEADS, (src & (2 * C_HEADS - 1)) == lane0 + dst // C_CHUNK)
    colb_all = jnp.dot(parts, jnp.where(pick, 1.0, 0.0).astype(BF16), preferred_element_type=F32)
    mask = (s_idx >= l_idx) if backward else (s_idx <= l_idx)
    c16 = c_t.astype(BF16)
    lane_g = lax.broadcasted_iota(jnp.int32, bn.shape, 1) // C_STATE
    row_g = lax.broadcasted_iota(jnp.int32, c_t.shape, 0) // C_STATE
    y_rows = []
    for g in range(C_GROUPS):
        bn_g = jnp.where(lane_g == g, bn, 0.0).astype(BF16)
        ct_g = jnp.where(row_g == g, c_t, 0.0).astype(BF16)
        cb_t = jnp.dot(bn_g, c16, preferred_element_type=F32)
        hs = h_scr[g]
        y_off = jnp.dot(hs.astype(BF16), ct_g, preferred_element_type=F32)
        y_g, xw_g, hs_new = [], [], []
        for r in range(C_HPG):
            h = g * C_HPG + r
            hl = lane0 + h
            colb = colb_all[:, h * C_CHUNK:(h + 1) * C_CHUNK]
            row = key_t[hl:hl + 1, :]
            tot_h = cs[C_CHUNK - 1:C_CHUNK, hl:hl + 1]
            d = (colb - row) if backward else (row - colb)
            m_t = (cb_t * jnp.exp(jnp.where(mask, d, NEG_INF))).astype(BF16)
            xdt = x_t[h * C_HEAD_DIM:(h + 1) * C_HEAD_DIM] * dt_t[hl:hl + 1, :]
            y_diag = jnp.dot(xdt.astype(BF16), m_t, preferred_element_type=F32)
            carry = jnp.exp(tot_h - row) if backward else jnp.exp(row)
            y_g.append(y_diag + y_off[r * C_HEAD_DIM:(r + 1) * C_HEAD_DIM] * carry)
            w_row = jnp.exp(row) if backward else jnp.exp(tot_h - row)
            xw_g.append((xdt * w_row).astype(BF16))
            hs_new.append(jnp.exp(tot_h) * hs[r * C_HEAD_DIM:(r + 1) * C_HEAD_DIM])
        h_scr[g] = (jnp.concatenate(hs_new, axis=0)
                    + jnp.dot(jnp.concatenate(xw_g, axis=0), bn_g, preferred_element_type=F32))
        y_rows += y_g
    y_t = jnp.concatenate(y_rows, axis=0)
    if not backward:
        y_t = y_t + dsk_scr[...] * x_t
    y_ref[...] = y_t.T


def _ssd_kernel(par_ref, dsk_ref, xf_ref, bf_ref, dtf_ref, xb_ref, bb_ref, dtb_ref, yf_ref, yb_ref,
                hf_scr, hb_scr, dsk_scr):
    @pl.when(pl.program_id(1) == 0)
    def _():
        hf_scr[...] = jnp.zeros(hf_scr.shape, F32)
        hb_scr[...] = jnp.zeros(hb_scr.shape, F32)
        dsk_scr[...] = jnp.broadcast_to(dsk_ref[...], dsk_scr.shape)

    par = par_ref[...]
    _ssd_direction(xf_ref, bf_ref, dtf_ref, par, dsk_scr, yf_ref, hf_scr, lane0=0, backward=False)
    _ssd_direction(xb_ref, bb_ref, dtb_ref, par, dsk_scr, yb_ref, hb_scr, lane0=C_HEADS, backward=True)


def _ssd_call(par, dsk_col, xbc_t, bn, dt, *, ctx):
    b, ch, t = xbc_t.shape
    nc = t // C_CHUNK
    ncc = ctx // C_CHUNK
    bchunk = lambda j: jnp.where(j < ncc, ncc - 1 - j, nc - 1 + ncc - j)
    fwd = lambda bi, j: (bi, j, 0)
    bwd = lambda bi, j: (bi, bchunk(j), 0)
    fwd_t = lambda bi, j: (bi, 0, j)
    bwd_t = lambda bi, j: (bi, 0, bchunk(j))
    tspec = lambda im: pl.BlockSpec((None, ch, C_CHUNK), im)
    nspec = lambda w, im: pl.BlockSpec((None, C_CHUNK, w), im)
    return pl.pallas_call(
        _ssd_kernel,
        grid=(b, nc),
        in_specs=[pl.BlockSpec((8, DT_LANES), lambda bi, j: (0, 0)),
                  pl.BlockSpec(dsk_col.shape, lambda bi, j: (0, 0)),
                  tspec(fwd_t), nspec(C_GN, fwd), nspec(DT_LANES, fwd),
                  tspec(bwd_t), nspec(C_GN, bwd), nspec(DT_LANES, bwd)],
        out_specs=[nspec(C_INNER, fwd), nspec(C_INNER, bwd)],
        out_shape=[jax.ShapeDtypeStruct((b, t, C_INNER), F32)] * 2,
        scratch_shapes=[pltpu.VMEM((C_GROUPS, C_HPG * C_HEAD_DIM, C_GN), F32)] * 2
                       + [pltpu.VMEM((C_INNER, C_CHUNK), F32)],
        compiler_params=_params(("parallel", "arbitrary")),
        name="ssd_scan",
    )(par, dsk_col, xbc_t, bn, dt, xbc_t, bn, dt)


def _outproj_kernel(x_ref, mod_ref, oa_ref, ob_ref, yf_ref, yb_ref, z_ref, gn_ref, w_ref, o_ref):
    m = mod_ref[...]
    gn = gn_ref[...]
    y = (yf_ref[...] + yb_ref[...]) * _silu(z_ref[...])
    gw = C_INNER // C_GROUPS
    oc = []
    for g in range(C_GROUPS):
        yg = y[:, g * gw:(g + 1) * gw]
        ms = jnp.mean(yg * yg, axis=1, keepdims=True)
        oc.append(yg * lax.rsqrt(ms + EPS) * gn[:, g * gw:(g + 1) * gw])
    mix = jnp.concatenate([oa_ref[...], ob_ref[...]] + oc, axis=1).astype(BF16)
    o_ref[...] = x_ref[...] + m[5:6] * jnp.dot(mix, w_ref[...], preferred_element_type=F32)


def _outproj_call(xs, mod, oa, ob, yf, yb, z, gn, w_out, *, tm, ctx_tiles, tile0, nbatch):
    b, t, d = xs.shape
    ntiles = t // tm - tile0
    tok = lambda w: pl.BlockSpec((None, tm, w), lambda bi, i: (bi, i + tile0, 0))
    tok0 = lambda w: pl.BlockSpec((None, tm, w), lambda bi, i: (bi, i, 0))
    mod_row = lambda bi, i: (jnp.where(i + tile0 < ctx_tiles, nbatch, bi), 0, 0)
    return pl.pallas_call(
        _outproj_kernel,
        grid=(b, ntiles),
        in_specs=[tok(d), pl.BlockSpec((None, N_MOD, d), mod_row),
                  tok0(A_HEADS * A_V), tok0(B_QCOLS), tok(C_INNER), tok(C_INNER), tok(C_INNER),
                  pl.BlockSpec(gn.shape, lambda bi, i: (0, 0)),
                  pl.BlockSpec(w_out.shape, lambda bi, i: (0, 0), pipeline_mode=pl.Buffered(1))],
        out_specs=tok0(d),
        out_shape=jax.ShapeDtypeStruct((b, ntiles * tm, d), F32),
        compiler_params=_params(("parallel", "parallel")),
        name="mix_out_proj",
    )(xs, mod, oa, ob, yf, yb, z, gn, w_out)


def _rope_tables(seq, ctx, dim, reps):
    rows = seq // GRID_W
    row = jnp.repeat(jnp.arange(rows, dtype=F32), GRID_W)
    col = jnp.tile(jnp.arange(GRID_W, dtype=F32), rows)
    quarter = dim // 4
    inv_freq = ROPE_BASE ** (-jnp.arange(quarter, dtype=F32) / quarter)
    ar = row[:, None] * inv_freq
    ac = col[:, None] * inv_freq
    ang = jnp.concatenate([ar, ar, ac, ac], axis=-1)
    sign = jnp.where((jnp.arange(dim) % (dim // 2)) < quarter, -1.0, 1.0).astype(F32)
    cos = jnp.concatenate([jnp.ones((ctx, dim), F32), jnp.cos(ang)], axis=0)
    sin = jnp.concatenate([jnp.zeros((ctx, dim), F32), jnp.sin(ang) * sign], axis=0)
    return jnp.tile(cos, (1, reps)), jnp.tile(sin, (1, reps))


def _group_matrix(n, group):
    idx = jnp.arange(n) // group
    return (idx[:, None] == idx[None, :]).astype(BF16)


def kernel(x, c, ctx, c_ctx, w_mod, b_mod, norm_ffn1, ffn1_w13, ffn1_w2, norm_mix, w_in, w_out, qn_a, kn_a, lam_q1, lam_k1, lam_q2, lam_k2, subln_a, qn_b, kn_b, sink_b, conv_w, conv_b, dt_bias, a_log, d_skip, gnorm_c, norm_ffn2, ffn2_w13, ffn2_w2):
    nb, seq, d = x.shape
    nctx = ctx.shape[1]
    t = nctx + seq
    depth = w_mod.shape[0]
    tm = 256
    fc = 256
    tq, tk = 256, (768 if t % 768 == 0 and t > 768 else 256)
    assert nctx % tm == 0 and t % tm == 0 and nctx % C_CHUNK == 0 and seq % GRID_W == 0 and nb < 8
    ctx_tiles = nctx // tm

    xs = x
    cvec = jnp.zeros((8, d), F32).at[:nb].set(c).at[nb].set(c_ctx)
    mod = _mod_call(cvec, w_mod, b_mod).reshape(depth, 8, N_MOD, d)

    cosa, sina = _rope_tables(seq, nctx, A_QK, ROPE_LANES // A_QK)
    cosb, sinb = _rope_tables(seq, nctx, B_DIM, ROPE_LANES // B_DIM)
    ga = _group_matrix(A_QCOLS, A_QK)
    gb = _group_matrix(B_QCOLS, B_DIM)
    src = jnp.arange(A_QCOLS)
    dst = (src // (2 * A_QK)) * A_HLANES + src % (2 * A_QK)
    kpl = (dst[:, None] == jnp.arange(A_HEADS * A_HLANES)[None, :]).astype(BF16)
    kone = (jnp.arange(A_HEADS * A_HLANES) % A_HLANES == 2 * A_QK).astype(F32).reshape(1, -1)
    in_cols = w_in.shape[2]
    in_pad = (-in_cols) % DT_LANES

    for l in range(depth):
        last = l == depth - 1
        tile0 = ctx_tiles if last else 0
        lam_init = 0.8 - 0.6 * math.exp(-0.3 * l)
        w_in_p = jnp.pad(w_in[l], ((0, 0), (0, in_pad))).astype(BF16)
        hn = jnp.stack([jnp.tile(qn_a[l], A_QCOLS // A_QK), jnp.tile(kn_a[l], A_QCOLS // A_QK),
                        jnp.tile(qn_b[l], B_QCOLS // B_DIM), jnp.tile(kn_b[l], B_QCOLS // B_DIM)]).astype(F32)
        hn = jnp.zeros((8, A_QCOLS), F32).at[:4].set(hn)

        xs = _ffn_call(xs, mod[l], norm_ffn1[l], ffn1_w13[l].astype(BF16), ffn1_w2[l].astype(BF16), si=0, tm=tm, fc=fc,
                       ctx_tiles=ctx_tiles, nbatch=nb, ctx_in=(ctx if l == 0 else None))

        qa_t, ka_x, va_t, qb, kb, vb, z, xbc_raw, dt = _inproj_call(
            xs, mod[l], norm_mix[l], w_in_p, ga, gb, hn, kpl, kone, cosa, sina, cosb, sinb,
            tm=tm, ctx_tiles=ctx_tiles, nbatch=nb)

        lamv = jnp.zeros((8, A_QK), F32).at[:4].set(jnp.stack([lam_q1[l], lam_k1[l], lam_q2[l], lam_k2[l]]))
        oa = _attn_a_call(lamv, subln_a[l].reshape(A_V, 1), qa_t, ka_x, va_t, tq=tq, tk=tk, ctx=nctx,
                          q_tile0=(nctx // tq if last else 0), lam_init=lam_init)

        ob = _attn_b_call(sink_b[l].astype(F32), qb, kb, vb, ctx=nctx, blk0=(nctx // B_BLOCK if last else 0))

        xbc_t, bn = _conv_call(xbc_raw, conv_w[l], conv_b[l], tm=tm, ctx=nctx)
        par = (jnp.zeros((8, DT_LANES), F32)
               .at[0, :2 * C_HEADS].set(dt_bias[l].reshape(-1))
               .at[1, :2 * C_HEADS].set(-jnp.exp(a_log[l].astype(F32)).reshape(-1)))
        dsk_col = jnp.repeat(d_skip[l].astype(F32), C_HEAD_DIM).reshape(C_INNER, 1)
        yf, yb = _ssd_call(par, dsk_col, xbc_t, bn, dt, ctx=nctx)

        xs = _outproj_call(xs, mod[l], oa, ob, yf, yb, z, gnorm_c[l].reshape(1, C_INNER), w_out[l].astype(BF16),
                           tm=tm, ctx_tiles=ctx_tiles, tile0=tile0, nbatch=nb)
        xs = _ffn_call(xs, mod[l], norm_ffn2[l], ffn2_w13[l].astype(BF16), ffn2_w2[l].astype(BF16), si=6, tm=tm, fc=fc,
                       ctx_tiles=(0 if last else ctx_tiles), nbatch=nb)
    return xs
```

```python
import functools
import math

import jax
import jax.numpy as jnp
from jax import lax
from jax.experimental import pallas as pl
from jax.experimental.pallas import tpu as pltpu

F32 = jnp.float32
BF16 = jnp.bfloat16
HIGHEST = lax.Precision.HIGHEST
NEG_INF = float("-inf")

EPS = 1e-6
ROPE_BASE = 10000.0
GRID_W = 64
ROPE_LANES = 128
N_MOD = 9

A_HEADS, A_QK, A_V = 4, 32, 64
A_QCOLS = A_HEADS * 2 * A_QK
A_SCALE = A_QK ** -0.5
LOG2E = math.log2(math.e)
A_HLANES = 128
A_VROWS = A_V + 16
A_SHIFT_CAP = 48.0
B_HEADS, B_KV_HEADS, B_DIM = 4, 2, 64
B_QCOLS = B_HEADS * B_DIM
B_KCOLS = B_KV_HEADS * B_DIM
B_SCALE = B_DIM ** -0.5
B_BLOCK = 128
B_QBLOCKS = 2
C_HEADS, C_HEAD_DIM, C_GROUPS, C_STATE, C_CONV = 8, 64, 2, 64, 5
C_INNER = C_HEADS * C_HEAD_DIM
C_GN = C_GROUPS * C_STATE
C_XBC = C_INNER + 2 * C_GN
C_CHUNK = 128
C_STEP_CHUNKS = 2
C_HPG = C_HEADS // C_GROUPS
DT_LANES = 128
HALO = 8

V7X_VMEM_LIMIT = 56 * 1024 * 1024


def _params(sem, vmem=V7X_VMEM_LIMIT):
    return pltpu.CompilerParams(dimension_semantics=sem, vmem_limit_bytes=vmem)


def _rms_mod(x, nw, shift, scale):
    ms = jnp.mean(x * x, axis=-1, keepdims=True)
    y = x * lax.rsqrt(ms + EPS) * nw
    return y * (1.0 + scale) + shift


def _silu(v):
    return v * jax.nn.sigmoid(v)


def _mod_kernel(c_ref, w_ref, b_ref, o_ref):
    sc = _silu(c_ref[...])
    o_ref[...] = jnp.dot(sc, w_ref[...], precision=HIGHEST, preferred_element_type=F32) + b_ref[...]


def _mod_call(cvec, w_mod, b_mod, tn=1152):
    nl, d, n = w_mod.shape
    return pl.pallas_call(
        _mod_kernel,
        grid=(nl, n // tn),
        in_specs=[pl.BlockSpec((8, d), lambda l, j: (0, 0)),
                  pl.BlockSpec((None, d, tn), lambda l, j: (l, 0, j)),
                  pl.BlockSpec((None, 1, tn), lambda l, j: (l, 0, j))],
        out_specs=pl.BlockSpec((None, 8, tn), lambda l, j: (l, 0, j)),
        out_shape=jax.ShapeDtypeStruct((nl, 8, n), F32),
        compiler_params=_params(("parallel", "parallel")),
        name="adaln_mod",
    )(cvec, w_mod, b_mod.reshape(nl, 1, n))


def _ffn_kernel(*refs, si, fc, ff, ctx_tiles, split_input):
    if split_input:
        c_ref, x_ref, mod_ref, nw_ref, w13_ref, w2_ref, o_ref, a_scr = refs
        x = jnp.where(pl.program_id(1) < ctx_tiles, c_ref[...], x_ref[...])
    else:
        x_ref, mod_ref, nw_ref, w13_ref, w2_ref, o_ref, a_scr = refs
        x = x_ref[...]
    m = mod_ref[...]
    h = _rms_mod(x, nw_ref[...], m[si:si + 1], m[si + 1:si + 2]).astype(BF16)
    for c in range(ff // fc):
        g = jnp.dot(h, w13_ref[:, c * fc:(c + 1) * fc], preferred_element_type=F32)
        u = jnp.dot(h, w13_ref[:, ff + c * fc:ff + (c + 1) * fc], preferred_element_type=F32)
        a_scr[:, c * fc:(c + 1) * fc] = (_silu(g) * u).astype(BF16)
    y = jnp.dot(a_scr[...], w2_ref[...], preferred_element_type=F32)
    o_ref[...] = x + (0.5 * m[si + 2:si + 3]) * y


def _ffn_call(xs, mod, nw, w13, w2, *, si, tm, fc, ctx_tiles, nbatch, ctx_in=None):
    b, t, d = xs.shape
    ff = w2.shape[0]
    split = ctx_in is not None
    ntiles = t // tm + (ctx_tiles if split else 0)
    mod_row = lambda bi, i: (jnp.where(i < ctx_tiles, nbatch, bi), 0, 0)
    if split:
        tok_specs = [pl.BlockSpec((None, tm, d), lambda bi, i: (bi, jnp.minimum(i, ctx_tiles - 1), 0)),
                     pl.BlockSpec((None, tm, d), lambda bi, i: (bi, jnp.maximum(i - ctx_tiles, 0), 0))]
        toks = (ctx_in, xs)
    else:
        tok_specs = [pl.BlockSpec((None, tm, d), lambda bi, i: (bi, i, 0))]
        toks = (xs,)
    return pl.pallas_call(
        functools.partial(_ffn_kernel, si=si, fc=fc, ff=ff, ctx_tiles=ctx_tiles, split_input=split),
        grid=(b, ntiles),
        in_specs=tok_specs + [pl.BlockSpec((None, N_MOD, d), mod_row),
                              pl.BlockSpec((1, d), lambda bi, i: (0, 0)),
                              pl.BlockSpec(w13.shape, lambda bi, i: (0, 0), pipeline_mode=pl.Buffered(1)),
                              pl.BlockSpec(w2.shape, lambda bi, i: (0, 0), pipeline_mode=pl.Buffered(1))],
        out_specs=pl.BlockSpec((None, tm, d), lambda bi, i: (bi, i, 0)),
        out_shape=jax.ShapeDtypeStruct((b, ntiles * tm, d), F32),
        scratch_shapes=[pltpu.VMEM((tm, ff), BF16)],
        compiler_params=_params(("parallel", "parallel")),
        name="swiglu_half",
    )(*toks, mod, nw.reshape(1, d), w13, w2)


def _rope(v, cos, sin_signed, quarter):
    n = v.shape[-1]
    lane = lax.broadcasted_iota(jnp.int32, v.shape, 1)
    first = (lane & (2 * quarter - 1)) < quarter
    vr = jnp.where(first, pltpu.roll(v, n - quarter, 1), pltpu.roll(v, quarter, 1))
    return v * cos + vr * sin_signed


def _group_norm(v, gmat, inv_n, w):
    sq = v * v
    hi = sq.astype(BF16)
    lo = (sq - hi.astype(F32)).astype(BF16)
    ms = (jnp.dot(hi, gmat, preferred_element_type=F32) + jnp.dot(lo, gmat, preferred_element_type=F32)) * inv_n
    return v * lax.rsqrt(ms + EPS) * w


def _inproj_kernel(x_ref, mod_ref, nw_ref, w_ref, ga_ref, gb_ref, hn_ref, kpl_ref, kone_ref, dup_ref, dupt_ref,
                   cosa_ref, sina_ref, cosb_ref, sinb_ref, qa_o, ka_o, va_o, qb_o, kb_o, vb_o, z_o, xbc_o, dt_o):
    m = mod_ref[...]
    h = _rms_mod(x_ref[...], nw_ref[...], m[3:4], m[4:5]).astype(BF16)
    p = jnp.dot(h, w_ref[...], preferred_element_type=F32)
    hn = hn_ref[...]
    ga, gb = ga_ref[...], gb_ref[...]
    twice = lambda a: jnp.concatenate([a, a], axis=1)
    cosa, sina, cosb, sinb = twice(cosa_ref[...]), twice(sina_ref[...]), twice(cosb_ref[...]), twice(sinb_ref[...])
    o = 0
    qa = _group_norm(p[:, o:o + A_QCOLS], ga, 1.0 / A_QK, hn[0:1]); o += A_QCOLS
    ka = _group_norm(p[:, o:o + A_QCOLS], ga, 1.0 / A_QK, hn[1:2]); o += A_QCOLS
    qa_o[...] = (_rope(qa, cosa, sina, A_QK // 4) * (A_SCALE * LOG2E)).T.astype(BF16)
    ka16 = _rope(ka, cosa, sina, A_QK // 4).astype(BF16)
    ka_o[...] = (jnp.dot(ka16, kpl_ref[...], preferred_element_type=F32) + kone_ref[...]).astype(BF16)
    va_o[...] = p[:, o:o + A_HEADS * A_V].T.astype(BF16); o += A_HEADS * A_V
    qb = _group_norm(p[:, o:o + B_QCOLS], gb, 1.0 / B_DIM, hn[2:3]); o += B_QCOLS
    kb = _group_norm(p[:, o:o + B_KCOLS], gb[:B_KCOLS, :B_KCOLS], 1.0 / B_DIM, hn[3:4, :B_KCOLS]); o += B_KCOLS
    qb_o[...] = (_rope(qb, cosb, sinb, B_DIM // 4) * B_SCALE).astype(BF16)
    kb16_t = _rope(kb, cosb[:, :B_KCOLS], sinb[:, :B_KCOLS], B_DIM // 4).T.astype(BF16)
    kb_o[...] = jnp.dot(dupt_ref[...], kb16_t, preferred_element_type=F32).astype(BF16)
    vb_o[...] = jnp.dot(p[:, o:o + B_KCOLS].astype(BF16), dup_ref[...],
                        preferred_element_type=F32).astype(BF16); o += B_KCOLS
    z_o[...] = p[:, o:o + C_INNER]; o += C_INNER
    xbc_o[...] = p[:, o:o + C_XBC]; o += C_XBC
    dt_o[...] = p[:, o:o + DT_LANES]


def _inproj_call(xs, mod, nw, w_in_p, ga, gb, hn, kpl, kone, dup, dupt, cosa, sina, cosb, sinb, *, tm, ctx_tiles, nbatch):
    b, t, d = xs.shape
    ntiles = t // tm
    tok = lambda w: pl.BlockSpec((None, tm, w), lambda bi, i: (bi, i, 0))
    tab = lambda w: pl.BlockSpec((tm, w), lambda bi, i: (i, 0))
    full = lambda a: pl.BlockSpec(a.shape, lambda bi, i: (0,) * a.ndim)
    mod_row = lambda bi, i: (jnp.where(i < ctx_tiles, nbatch, bi), 0, 0)
    widths = (A_QCOLS, A_HEADS * A_HLANES, A_HEADS * A_V, B_QCOLS, B_QCOLS, B_QCOLS, C_INNER, C_XBC, DT_LANES)
    transposed = (0, 2, 4)
    dtypes = (BF16,) * 6 + (F32,) * 3
    return pl.pallas_call(
        _inproj_kernel,
        grid=(b, ntiles),
        in_specs=[tok(d), pl.BlockSpec((None, N_MOD, d), mod_row), pl.BlockSpec((1, d), lambda bi, i: (0, 0)),
                  pl.BlockSpec(w_in_p.shape, lambda bi, i: (0, 0), pipeline_mode=pl.Buffered(1)),
                  full(ga), full(gb), full(hn), full(kpl), full(kone), full(dup), full(dupt)] + [tab(ROPE_LANES)] * 4,
        out_specs=[pl.BlockSpec((None, w, tm), lambda bi, i: (bi, 0, i)) if k in transposed else tok(w)
                   for k, w in enumerate(widths)],
        out_shape=[jax.ShapeDtypeStruct((b, w, t) if k in transposed else (b, t, w), dt)
                   for k, (w, dt) in enumerate(zip(widths, dtypes))],
        compiler_params=_params(("parallel", "parallel")),
        name="in_proj_heads",
    )(xs, mod, nw.reshape(1, d), w_in_p, ga, gb, hn, kpl, kone, dup, dupt, cosa, sina, cosb, sinb)


def _attn_a_kernel(lamv_ref, subln_ref, qt_ref, kx_ref, vt_ref, o_ref, kmax_scr, *,
                   tq, tk, ctx, nk, q_tile0, lam_init):
    step = pl.program_id(1)

    @pl.when(step == 0)
    def _():
        r = lax.broadcasted_iota(jnp.int32, (A_HLANES, A_HLANES), 0)
        c = lax.broadcasted_iota(jnp.int32, (A_HLANES, A_HLANES), 1)
        in_comp = jnp.logical_and(r >= c * A_QK, r < (c + 1) * A_QK)
        sel = jnp.where(jnp.logical_and(in_comp, c < 2), 1.0, 0.0).astype(BF16)
        for h in range(A_HEADS):
            kk = kx_ref[:, h * A_HLANES:(h + 1) * A_HLANES].astype(F32)
            ksq = jnp.dot((kk * kk).astype(BF16), sel, preferred_element_type=F32)
            kmax_scr[h:h + 1, :] = jnp.sqrt(jnp.max(ksq, axis=0, keepdims=True))

    lv = lamv_ref[...]
    lam = (jnp.exp(jnp.sum(lv[0:1] * lv[1:2], axis=1, keepdims=True))
           - jnp.exp(jnp.sum(lv[2:3] * lv[3:4], axis=1, keepdims=True)) + lam_init)
    qt = qt_ref[...].astype(F32)
    qcs = [qt[hc * A_QK:(hc + 1) * A_QK, :] for hc in range(2 * A_HEADS)]
    shifts = [jnp.sqrt(jnp.sum(qc * qc, axis=0, keepdims=True)) * kmax_scr[hc // 2:hc // 2 + 1, hc % 2:hc % 2 + 1]
              for hc, qc in enumerate(qcs)]
    shift_max = jnp.max(functools.reduce(jnp.maximum, shifts))

    def q_ext(h, shifted):
        z = jnp.zeros((A_QK, tq), F32)
        top = jnp.concatenate([qcs[2 * h], z], axis=1)
        mid = jnp.concatenate([z, qcs[2 * h + 1]], axis=1)
        row = lax.broadcasted_iota(jnp.int32, (A_HLANES - 2 * A_QK, 2 * tq), 0)
        if shifted:
            bot = jnp.where(row == 0, -jnp.concatenate([shifts[2 * h], shifts[2 * h + 1]], axis=1), 0.0)
        else:
            bot = jnp.zeros(row.shape, F32)
        return jnp.concatenate([top, mid, bot], axis=0).astype(BF16)

    def scores(qx, h, start, size):
        return jnp.dot(kx_ref[pl.ds(start, size), h * A_HLANES:(h + 1) * A_HLANES], qx, preferred_element_type=F32)

    def v_ext(h, start, size):
        ones = jnp.ones((A_VROWS - A_V, size), BF16)
        return jnp.concatenate([vt_ref[h * A_V:(h + 1) * A_V, pl.ds(start, size)], ones], axis=0)

    def finish(accs):
        rows = []
        for acc in accs:
            o = (acc[:A_V, :tq] / acc[A_V:A_V + 1, :tq]) - lam * (acc[:A_V, tq:] / acc[A_V:A_V + 1, tq:])
            ms = jnp.mean(o * o, axis=0, keepdims=True)
            rows.append(o * lax.rsqrt(ms + EPS) * subln_ref[...] * (1.0 - lam_init))
        o_ref[...] = jnp.concatenate(rows, axis=0).T

    def attend(shifted):
        qx = [q_ext(h, shifted) for h in range(A_HEADS)]

        def head_step(h, start, carry):
            s = scores(qx[h], h, start, tk)
            ve = v_ext(h, start, tk)
            if shifted:
                return carry + jnp.dot(ve, jnp.exp2(s).astype(BF16), preferred_element_type=F32)
            m_run, acc = carry
            m_new = jnp.maximum(m_run, jnp.max(s, axis=0, keepdims=True))
            p = jnp.exp2(s - m_new).astype(BF16)
            return m_new, jnp.exp2(m_run - m_new) * acc + jnp.dot(ve, p, preferred_element_type=F32)

        def body(j, carries):
            start = pl.multiple_of(j * tk, tk)
            return tuple(head_step(h, start, carries[h]) for h in range(A_HEADS))

        acc0 = jnp.zeros((A_VROWS, 2 * tq), F32)
        if shifted:
            accs = lax.fori_loop(0, nk, body, (acc0,) * A_HEADS)
        else:
            init = (jnp.full((1, 2 * tq), NEG_INF, F32), acc0)
            accs = [c[1] for c in lax.fori_loop(0, nk, body, (init,) * A_HEADS)]
        finish(accs)

    def attend_ctx():
        accs = []
        for h in range(A_HEADS):
            s = scores(q_ext(h, False), h, 0, ctx)
            p = jnp.exp2(s - jnp.max(s, axis=0, keepdims=True)).astype(BF16)
            accs.append(jnp.dot(v_ext(h, 0, ctx), p, preferred_element_type=F32))
        finish(accs)

    safe = shift_max < A_SHIFT_CAP
    if q_tile0 * tq < ctx:
        is_ctx = (step + q_tile0) * tq < ctx
        pl.when(is_ctx)(attend_ctx)
        pl.when(jnp.logical_and(jnp.logical_not(is_ctx), safe))(lambda: attend(True))
        pl.when(jnp.logical_and(jnp.logical_not(is_ctx), jnp.logical_not(safe)))(lambda: attend(False))
    else:
        pl.when(safe)(lambda: attend(True))
        pl.when(jnp.logical_not(safe))(lambda: attend(False))


def _attn_a_call(lamv, subln_col, qa_t, ka_x, va_t, *, tq, tk, ctx, q_tile0, lam_init):
    b, w, t = qa_t.shape
    nq = t // tq - q_tile0
    return pl.pallas_call(
        functools.partial(_attn_a_kernel, tq=tq, tk=tk, ctx=ctx, nk=t // tk, q_tile0=q_tile0, lam_init=lam_init),
        grid=(b, nq),
        in_specs=[pl.BlockSpec(lamv.shape, lambda bi, qi: (0, 0)),
                  pl.BlockSpec(subln_col.shape, lambda bi, qi: (0, 0)),
                  pl.BlockSpec((None, w, tq), lambda bi, qi: (bi, 0, qi + q_tile0)),
                  pl.BlockSpec((None, t, A_HEADS * A_HLANES), lambda bi, qi: (bi, 0, 0)),
                  pl.BlockSpec((None, w, t), lambda bi, qi: (bi, 0, 0))],
        out_specs=pl.BlockSpec((None, tq, w), lambda bi, qi: (bi, qi, 0)),
        out_shape=jax.ShapeDtypeStruct((b, nq * tq, w), F32),
        scratch_shapes=[pltpu.VMEM((8, A_HLANES), F32)],
        compiler_params=_params(("parallel", "arbitrary")),
        name="diff_attention",
    )(lamv, subln_col, qa_t, ka_x, va_t)


def _attn_b_kernel(sink_ref, band_ref, q_ref, kp_ref, k0_ref, k1_ref, kn_ref, kx_ref,
                   vp_ref, v0_ref, v1_ref, vn_ref, vx_ref, o_ref, *, blk0, ctx_blocks, nblk):
    n0 = pl.program_id(1) * B_QBLOCKS + blk0
    nq = B_QBLOCKS * B_BLOCK
    k_all = jnp.concatenate([kp_ref[...], k0_ref[...], k1_ref[...], kn_ref[...], kx_ref[...]], axis=1)
    v_all = jnp.concatenate([vp_ref[...], v0_ref[...], v1_ref[...], vn_ref[...], vx_ref[...]], axis=0)
    lane = lax.broadcasted_iota(jnp.int32, (1, k_all.shape[1]), 1)
    lat = n0 >= ctx_blocks
    pen = jnp.zeros(lane.shape, F32)
    for d in range(B_QBLOCKS + 2):
        kb = n0 - 1 + d
        ok = jnp.logical_and(lat, jnp.logical_and(kb >= ctx_blocks, kb <= nblk - 1))
        in_blk = jnp.logical_and(lane >= d * B_BLOCK, lane < (d + 1) * B_BLOCK)
        pen = jnp.where(in_blk, jnp.where(ok, 0.0, NEG_INF), pen)
    bias = band_ref[...] + pen
    hpg = B_HEADS // B_KV_HEADS
    gl = hpg * B_DIM
    qlane = lax.broadcasted_iota(jnp.int32, (nq, gl), 1) // B_DIM
    outs = []
    for g in range(B_KV_HEADS):
        q_g = q_ref[:, g * gl:(g + 1) * gl]
        o_g = jnp.zeros((nq, gl), F32)
        for e in range(hpg):
            q_e = jnp.where(qlane == e, q_g, jnp.zeros_like(q_g))
            s = jnp.dot(q_e, k_all[g * gl:(g + 1) * gl, :], preferred_element_type=F32) + bias
            sk = sink_ref[g * hpg + e]
            m = jnp.maximum(jnp.max(s, axis=1, keepdims=True), sk)
            p = jnp.exp(s - m)
            den = jnp.sum(p, axis=1, keepdims=True) + jnp.exp(sk - m)
            pv = jnp.dot(p.astype(BF16), v_all[:, g * gl:(g + 1) * gl], preferred_element_type=F32)
            o_g = jnp.where(qlane == e, pv / den, o_g)
        outs.append(o_g)
    o_ref[...] = jnp.concatenate(outs, axis=1)


def _attn_b_call(sink, band, qb, kb_t, vb2, *, ctx, blk0):
    b, t, w = qb.shape
    nblk = t // B_BLOCK
    ctx_blocks = ctx // B_BLOCK
    nq = B_QBLOCKS * B_BLOCK
    first = lambda i: i * B_QBLOCKS + blk0
    clamp = lambda n: jnp.clip(n, 0, nblk - 1)
    kspec = lambda d: pl.BlockSpec((None, w, B_BLOCK), lambda bi, i: (bi, 0, clamp(first(i) - 1 + d)))
    vspec = lambda d: pl.BlockSpec((None, B_BLOCK, w), lambda bi, i: (bi, clamp(first(i) - 1 + d), 0))
    nwb = B_QBLOCKS + 2
    return pl.pallas_call(
        functools.partial(_attn_b_kernel, blk0=blk0, ctx_blocks=ctx_blocks, nblk=nblk),
        grid=(b, (nblk - blk0) // B_QBLOCKS),
        in_specs=[pl.BlockSpec(memory_space=pltpu.SMEM),
                  pl.BlockSpec(band.shape, lambda bi, i: (0, 0)),
                  pl.BlockSpec((None, nq, w), lambda bi, i: (bi, i + blk0 // B_QBLOCKS, 0))]
                 + [kspec(d) for d in range(nwb)] + [pl.BlockSpec((None, w, ctx), lambda bi, i: (bi, 0, 0))]
                 + [vspec(d) for d in range(nwb)] + [pl.BlockSpec((None, ctx, w), lambda bi, i: (bi, 0, 0))],
        out_specs=pl.BlockSpec((None, nq, w), lambda bi, i: (bi, i, 0)),
        out_shape=jax.ShapeDtypeStruct((b, (nblk - blk0) * B_BLOCK, w), F32),
        compiler_params=_params(("parallel", "parallel")),
        name="window_attention",
    )(sink, band, qb, *([kb_t] * (nwb + 1)), *([vb2] * (nwb + 1)))


def _conv_kernel(u_ref, up_ref, un_ref, w_ref, b_ref, xt_o, bn_o, *, tm, ctx_tiles, ntiles):
    i = pl.program_id(1)
    u = u_ref[...]
    has_prev = jnp.logical_and(i != 0, i != ctx_tiles)
    has_next = jnp.logical_and(i != ctx_tiles - 1, i != ntiles - 1)
    up = jnp.where(has_prev, up_ref[...], 0.0)
    un = jnp.where(has_next, un_ref[...], 0.0)
    full = jnp.concatenate([up, u, un], axis=0)
    w = w_ref[...]
    acc = b_ref[...] + w[0:1] * full[HALO - 2:HALO - 2 + tm]
    for k in range(1, C_CONV):
        acc = acc + w[k:k + 1] * full[HALO - 2 + k:HALO - 2 + k + tm]
    y = _silu(acc)
    xt_o[...] = y.T
    bn_o[...] = y[:, C_INNER:C_INNER + C_GN]


def _conv_call(xbc_raw, conv_w, conv_b, *, tm, ctx):
    b, t, ch = xbc_raw.shape
    ntiles = t // tm
    hb = tm // HALO
    nh = t // HALO
    w8 = jnp.zeros((8, ch), F32).at[:C_CONV].set(conv_w)
    return pl.pallas_call(
        functools.partial(_conv_kernel, tm=tm, ctx_tiles=ctx // tm, ntiles=ntiles),
        grid=(b, ntiles),
        in_specs=[pl.BlockSpec((None, tm, ch), lambda bi, i: (bi, i, 0)),
                  pl.BlockSpec((None, HALO, ch), lambda bi, i: (bi, jnp.maximum(i * hb - 1, 0), 0)),
                  pl.BlockSpec((None, HALO, ch), lambda bi, i: (bi, jnp.minimum((i + 1) * hb, nh - 1), 0)),
                  pl.BlockSpec((8, ch), lambda bi, i: (0, 0)),
                  pl.BlockSpec((1, ch), lambda bi, i: (0, 0))],
        out_specs=[pl.BlockSpec((None, ch, tm), lambda bi, i: (bi, 0, i)),
                   pl.BlockSpec((None, tm, C_GN), lambda bi, i: (bi, i, 0))],
        out_shape=[jax.ShapeDtypeStruct((b, ch, t), F32), jax.ShapeDtypeStruct((b, t, C_GN), F32)],
        compiler_params=_params(("parallel", "parallel")),
        name="ssm_conv",
    )(xbc_raw, xbc_raw, xbc_raw, w8, conv_b.reshape(1, ch))


def _softplus(v):
    return jnp.maximum(v, 0.0) + jnp.log1p(jnp.exp(-jnp.abs(v)))


def _split3(v):
    v1 = v.astype(BF16)
    r1 = v - v1.astype(F32)
    v2 = r1.astype(BF16)
    v3 = (r1 - v2.astype(F32)).astype(BF16)
    return v1, v2, v3


def _ssd_prepare(xt_all, bn, dt_raw, par, *, lane0, backward):
    c_t = xt_all[C_INNER + C_GN:]
    dt = _softplus(dt_raw + par[0:1])
    a = dt * par[1:2]
    s_idx = lax.broadcasted_iota(jnp.int32, (C_CHUNK, C_CHUNK), 0)
    l_idx = lax.broadcasted_iota(jnp.int32, (C_CHUNK, C_CHUNK), 1)
    tri = jnp.where(l_idx <= s_idx, 1.0, 0.0).astype(BF16)
    cs = sum(jnp.dot(tri, part, preferred_element_type=F32) for part in _split3(a))
    key = cs - a if backward else cs
    k1, k2, k3 = _split3(key)
    parts = (k1.astype(F32) + pltpu.roll(k2.astype(F32), 2 * C_HEADS, 1)
             + pltpu.roll(k3.astype(F32), 4 * C_HEADS, 1)).astype(BF16)
    src = lax.broadcasted_iota(jnp.int32, (DT_LANES, C_HEADS * C_CHUNK), 0)
    dst = lax.broadcasted_iota(jnp.int32, (DT_LANES, C_HEADS * C_CHUNK), 1)
    pick = jnp.logical_and(src < 6 * C_HEADS, (src & (2 * C_HEADS - 1)) == lane0 + dst // C_CHUNK)
    colb_all = jnp.dot(parts, jnp.where(pick, 1.0, 0.0).astype(BF16), preferred_element_type=F32)
    return dict(x_t=xt_all[:C_INNER], c_t=c_t, c16=c_t.astype(BF16), bn=bn, dt_t=dt.T, key_t=key.T, cs=cs,
                colb_all=colb_all, mask=(s_idx >= l_idx) if backward else (s_idx <= l_idx))


def _ssd_group(st, g, hs, *, lane0, backward):
    lane_g = lax.broadcasted_iota(jnp.int32, st["bn"].shape, 1) // C_STATE
    row_g = lax.broadcasted_iota(jnp.int32, st["c_t"].shape, 0) // C_STATE
    bn_g = jnp.where(lane_g == g, st["bn"], 0.0).astype(BF16)
    ct_g = jnp.where(row_g == g, st["c_t"], 0.0).astype(BF16)
    cb_t = jnp.dot(bn_g, st["c16"], preferred_element_type=F32)
    y_off = jnp.dot(hs.astype(BF16), ct_g, preferred_element_type=F32)
    y_g, xw_g, hs_new = [], [], []
    for r in range(C_HPG):
        h = g * C_HPG + r
        hl = lane0 + h
        colb = st["colb_all"][:, h * C_CHUNK:(h + 1) * C_CHUNK]
        row = st["key_t"][hl:hl + 1, :]
        tot_h = st["cs"][C_CHUNK - 1:C_CHUNK, hl:hl + 1]
        d = (colb - row) if backward else (row - colb)
        m_t = (cb_t * jnp.exp(jnp.where(st["mask"], d, NEG_INF))).astype(BF16)
        xdt = st["x_t"][h * C_HEAD_DIM:(h + 1) * C_HEAD_DIM] * st["dt_t"][hl:hl + 1, :]
        y_diag = jnp.dot(xdt.astype(BF16), m_t, preferred_element_type=F32)
        carry = jnp.exp(tot_h - row) if backward else jnp.exp(row)
        y_g.append(y_diag + y_off[r * C_HEAD_DIM:(r + 1) * C_HEAD_DIM] * carry)
        w_row = jnp.exp(row) if backward else jnp.exp(tot_h - row)
        xw_g.append((xdt * w_row).astype(BF16))
        hs_new.append(jnp.exp(tot_h) * hs[r * C_HEAD_DIM:(r + 1) * C_HEAD_DIM])
    state = jnp.concatenate(hs_new, axis=0) + jnp.dot(jnp.concatenate(xw_g, axis=0), bn_g, preferred_element_type=F32)
    return y_g, state


def _ssd_kernel(par_ref, dsk_ref, xf_ref, bf_ref, dtf_ref, xb_ref, bb_ref, dtb_ref, yf_ref, yb_ref,
                hf_scr, hb_scr, dsk_scr):
    @pl.when(pl.program_id(1) == 0)
    def _():
        hf_scr[...] = jnp.zeros(hf_scr.shape, F32)
        hb_scr[...] = jnp.zeros(hb_scr.shape, F32)
        dsk_scr[...] = jnp.broadcast_to(dsk_ref[...], dsk_scr.shape)

    par = par_ref[...]
    hf = [hf_scr[g] for g in range(C_GROUPS)]
    hb = [hb_scr[g] for g in range(C_GROUPS)]
    for k in range(C_STEP_CHUNKS):
        cf = slice(k * C_CHUNK, (k + 1) * C_CHUNK)
        cb = slice((C_STEP_CHUNKS - 1 - k) * C_CHUNK, (C_STEP_CHUNKS - k) * C_CHUNK)
        sf = _ssd_prepare(xf_ref[:, cf], bf_ref[cf, :], dtf_ref[cf, :], par, lane0=0, backward=False)
        sb = _ssd_prepare(xb_ref[:, cb], bb_ref[cb, :], dtb_ref[cb, :], par, lane0=C_HEADS, backward=True)
        rows_f, rows_b = [], []
        for g in range(C_GROUPS):
            y_g, hf[g] = _ssd_group(sf, g, hf[g], lane0=0, backward=False)
            rows_f += y_g
            y_g, hb[g] = _ssd_group(sb, g, hb[g], lane0=C_HEADS, backward=True)
            rows_b += y_g
        yf_ref[cf, :] = (jnp.concatenate(rows_f, axis=0) + dsk_scr[...] * sf["x_t"]).T
        yb_ref[cb, :] = jnp.concatenate(rows_b, axis=0).T
    for g in range(C_GROUPS):
        hf_scr[g] = hf[g]
        hb_scr[g] = hb[g]


def _ssd_call(par, dsk_col, xbc_t, bn, dt, *, ctx):
    b, ch, t = xbc_t.shape
    blk = C_STEP_CHUNKS * C_CHUNK
    nblk = t // blk
    ncb = ctx // blk
    bblock = lambda j: jnp.where(j < ncb, ncb - 1 - j, nblk - 1 + ncb - j)
    fwd = lambda bi, j: (bi, j, 0)
    bwd = lambda bi, j: (bi, bblock(j), 0)
    fwd_t = lambda bi, j: (bi, 0, j)
    bwd_t = lambda bi, j: (bi, 0, bblock(j))
    tspec = lambda im: pl.BlockSpec((None, ch, blk), im)
    nspec = lambda w, im: pl.BlockSpec((None, blk, w), im)
    return pl.pallas_call(
        _ssd_kernel,
        grid=(b, nblk),
        in_specs=[pl.BlockSpec((8, DT_LANES), lambda bi, j: (0, 0)),
                  pl.BlockSpec(dsk_col.shape, lambda bi, j: (0, 0)),
                  tspec(fwd_t), nspec(C_GN, fwd), nspec(DT_LANES, fwd),
                  tspec(bwd_t), nspec(C_GN, bwd), nspec(DT_LANES, bwd)],
        out_specs=[nspec(C_INNER, fwd), nspec(C_INNER, bwd)],
        out_shape=[jax.ShapeDtypeStruct((b, t, C_INNER), F32)] * 2,
        scratch_shapes=[pltpu.VMEM((C_GROUPS, C_HPG * C_HEAD_DIM, C_GN), F32)] * 2
                       + [pltpu.VMEM((C_INNER, C_CHUNK), F32)],
        compiler_params=_params(("parallel", "arbitrary")),
        name="ssd_scan",
    )(par, dsk_col, xbc_t, bn, dt, xbc_t, bn, dt)


def _outproj_kernel(x_ref, mod_ref, oa_ref, ob_ref, yf_ref, yb_ref, z_ref, gn_ref, w_ref, o_ref):
    m = mod_ref[...]
    gn = gn_ref[...]
    y = (yf_ref[...] + yb_ref[...]) * _silu(z_ref[...])
    gw = C_INNER // C_GROUPS
    oc = []
    for g in range(C_GROUPS):
        yg = y[:, g * gw:(g + 1) * gw]
        ms = jnp.mean(yg * yg, axis=1, keepdims=True)
        oc.append(yg * lax.rsqrt(ms + EPS) * gn[:, g * gw:(g + 1) * gw])
    mix = jnp.concatenate([oa_ref[...], ob_ref[...]] + oc, axis=1).astype(BF16)
    o_ref[...] = x_ref[...] + m[5:6] * jnp.dot(mix, w_ref[...], preferred_element_type=F32)


def _outproj_call(xs, mod, oa, ob, yf, yb, z, gn, w_out, *, tm, ctx_tiles, tile0, nbatch):
    b, t, d = xs.shape
    ntiles = t // tm - tile0
    tok = lambda w: pl.BlockSpec((None, tm, w), lambda bi, i: (bi, i + tile0, 0))
    tok0 = lambda w: pl.BlockSpec((None, tm, w), lambda bi, i: (bi, i, 0))
    mod_row = lambda bi, i: (jnp.where(i + tile0 < ctx_tiles, nbatch, bi), 0, 0)
    return pl.pallas_call(
        _outproj_kernel,
        grid=(b, ntiles),
        in_specs=[tok(d), pl.BlockSpec((None, N_MOD, d), mod_row),
                  tok0(A_HEADS * A_V), tok0(B_QCOLS), tok(C_INNER), tok(C_INNER), tok(C_INNER),
                  pl.BlockSpec(gn.shape, lambda bi, i: (0, 0)),
                  pl.BlockSpec(w_out.shape, lambda bi, i: (0, 0), pipeline_mode=pl.Buffered(1))],
        out_specs=tok0(d),
        out_shape=jax.ShapeDtypeStruct((b, ntiles * tm, d), F32),
        compiler_params=_params(("parallel", "parallel")),
        name="mix_out_proj",
    )(xs, mod, oa, ob, yf, yb, z, gn, w_out)


def _rope_tables(seq, ctx, dim, reps):
    rows = seq // GRID_W
    row = jnp.repeat(jnp.arange(rows, dtype=F32), GRID_W)
    col = jnp.tile(jnp.arange(GRID_W, dtype=F32), rows)
    quarter = dim // 4
    inv_freq = ROPE_BASE ** (-jnp.arange(quarter, dtype=F32) / quarter)
    ar = row[:, None] * inv_freq
    ac = col[:, None] * inv_freq
    ang = jnp.concatenate([ar, ar, ac, ac], axis=-1)
    sign = jnp.where((jnp.arange(dim) % (dim // 2)) < quarter, -1.0, 1.0).astype(F32)
    cos = jnp.concatenate([jnp.ones((ctx, dim), F32), jnp.cos(ang)], axis=0)
    sin = jnp.concatenate([jnp.zeros((ctx, dim), F32), jnp.sin(ang) * sign], axis=0)
    return jnp.tile(cos, (1, reps)), jnp.tile(sin, (1, reps))


def _group_matrix(n, group):
    idx = jnp.arange(n) // group
    return (idx[:, None] == idx[None, :]).astype(BF16)


def kernel(x, c, ctx, c_ctx, w_mod, b_mod, norm_ffn1, ffn1_w13, ffn1_w2, norm_mix, w_in, w_out, qn_a, kn_a, lam_q1, lam_k1, lam_q2, lam_k2, subln_a, qn_b, kn_b, sink_b, conv_w, conv_b, dt_bias, a_log, d_skip, gnorm_c, norm_ffn2, ffn2_w13, ffn2_w2):
    nb, seq, d = x.shape
    nctx = ctx.shape[1]
    t = nctx + seq
    depth = w_mod.shape[0]
    tm = 256
    fc = 256
    tq, tk = 256, (768 if t % 768 == 0 and t > 768 else 256)
    assert nctx % tm == 0 and t % tm == 0 and nctx % (C_STEP_CHUNKS * C_CHUNK) == 0 and seq % GRID_W == 0 and nb < 8
    ctx_tiles = nctx // tm

    xs = x
    cvec = jnp.zeros((8, d), F32).at[:nb].set(c).at[nb].set(c_ctx)
    mod = _mod_call(cvec, w_mod, b_mod).reshape(depth, 8, N_MOD, d)

    cosa, sina = _rope_tables(seq, nctx, A_QK, ROPE_LANES // A_QK)
    cosb, sinb = _rope_tables(seq, nctx, B_DIM, ROPE_LANES // B_DIM)
    ga = _group_matrix(A_QCOLS, A_QK)
    gb = _group_matrix(B_QCOLS, B_DIM)
    src = jnp.arange(A_QCOLS)
    dst = (src // (2 * A_QK)) * A_HLANES + src % (2 * A_QK)
    kpl = (dst[:, None] == jnp.arange(A_HEADS * A_HLANES)[None, :]).astype(BF16)
    kone = (jnp.arange(A_HEADS * A_HLANES) % A_HLANES == 2 * A_QK).astype(F32).reshape(1, -1)
    dsrc = jnp.arange(B_KCOLS)
    dcol = jnp.arange(B_QCOLS)
    hpg = B_HEADS // B_KV_HEADS
    dup = jnp.logical_and(dcol[None, :] // (hpg * B_DIM) == dsrc[:, None] // B_DIM,
                          dcol[None, :] % B_DIM == dsrc[:, None] % B_DIM).astype(BF16)
    qi = jnp.arange(B_QBLOCKS * B_BLOCK)[:, None]
    kj = jnp.arange((B_QBLOCKS + 2) * B_BLOCK + nctx)[None, :]
    in_band = jnp.logical_and(kj - qi >= 0, kj - qi <= 2 * B_BLOCK)
    band = jnp.where(jnp.logical_or(in_band, kj >= (B_QBLOCKS + 2) * B_BLOCK), 0.0, NEG_INF).astype(F32)
    in_cols = w_in.shape[2]
    in_pad = (-in_cols) % DT_LANES

    for l in range(depth):
        last = l == depth - 1
        tile0 = ctx_tiles if last else 0
        lam_init = 0.8 - 0.6 * math.exp(-0.3 * l)
        w_in_p = jnp.pad(w_in[l], ((0, 0), (0, in_pad))).astype(BF16)
        hn = jnp.stack([jnp.tile(qn_a[l], A_QCOLS // A_QK), jnp.tile(kn_a[l], A_QCOLS // A_QK),
                        jnp.tile(qn_b[l], B_QCOLS // B_DIM), jnp.tile(kn_b[l], B_QCOLS // B_DIM)]).astype(F32)
        hn = jnp.zeros((8, A_QCOLS), F32).at[:4].set(hn)

        xs = _ffn_call(xs, mod[l], norm_ffn1[l], ffn1_w13[l].astype(BF16), ffn1_w2[l].astype(BF16), si=0, tm=tm, fc=fc,
                       ctx_tiles=ctx_tiles, nbatch=nb, ctx_in=(ctx if l == 0 else None))

        qa_t, ka_x, va_t, qb, kb_t, vb2, z, xbc_raw, dt = _inproj_call(
            xs, mod[l], norm_mix[l], w_in_p, ga, gb, hn, kpl, kone, dup, dup.T, cosa, sina, cosb, sinb,
            tm=tm, ctx_tiles=ctx_tiles, nbatch=nb)

        lamv = jnp.zeros((8, A_QK), F32).at[:4].set(jnp.stack([lam_q1[l], lam_k1[l], lam_q2[l], lam_k2[l]]))
        oa = _attn_a_call(lamv, subln_a[l].reshape(A_V, 1), qa_t, ka_x, va_t, tq=tq, tk=tk, ctx=nctx,
                          q_tile0=(nctx // tq if last else 0), lam_init=lam_init)

        ob = _attn_b_call(sink_b[l].astype(F32), band, qb, kb_t, vb2, ctx=nctx, blk0=(nctx // B_BLOCK if last else 0))

        xbc_t, bn = _conv_call(xbc_raw, conv_w[l], conv_b[l], tm=tm, ctx=nctx)
        par = (jnp.zeros((8, DT_LANES), F32)
               .at[0, :2 * C_HEADS].set(dt_bias[l].reshape(-1))
               .at[1, :2 * C_HEADS].set(-jnp.exp(a_log[l].astype(F32)).reshape(-1)))
        dsk_col = jnp.repeat(d_skip[l].astype(F32), C_HEAD_DIM).reshape(C_INNER, 1)
        yf, yb = _ssd_call(par, dsk_col, xbc_t, bn, dt, ctx=nctx)

        xs = _outproj_call(xs, mod[l], oa, ob, yf, yb, z, gnorm_c[l].reshape(1, C_INNER), w_out[l].astype(BF16),
                           tm=tm, ctx_tiles=ctx_tiles, tile0=tile0, nbatch=nb)
        xs = _ffn_call(xs, mod[l], norm_ffn2[l], ffn2_w13[l].astype(BF16), ffn2_w2[l].astype(BF16), si=6, tm=tm, fc=fc,
                       ctx_tiles=(0 if last else ctx_tiles), nbatch=nb)
    return xs
```

```python
import functools
import math

import jax
import jax.numpy as jnp
from jax import lax
from jax.experimental import pallas as pl
from jax.experimental.pallas import tpu as pltpu

F32 = jnp.float32
BF16 = jnp.bfloat16
HIGHEST = lax.Precision.HIGHEST
NEG_INF = float("-inf")

EPS = 1e-6
ROPE_BASE = 10000.0
GRID_W = 64
ROPE_LANES = 128
N_MOD = 9

A_HEADS, A_QK, A_V = 4, 32, 64
A_QCOLS = A_HEADS * 2 * A_QK
A_SCALE = A_QK ** -0.5
LOG2E = math.log2(math.e)
A_HLANES = 128
A_VROWS = A_V + 16
A_SHIFT_CAP = 48.0
B_HEADS, B_KV_HEADS, B_DIM = 4, 2, 64
B_QCOLS = B_HEADS * B_DIM
B_KCOLS = B_KV_HEADS * B_DIM
B_SCALE = B_DIM ** -0.5
B_BLOCK = 128
B_QBLOCKS = 2
C_HEADS, C_HEAD_DIM, C_GROUPS, C_STATE, C_CONV = 8, 64, 2, 64, 5
C_INNER = C_HEADS * C_HEAD_DIM
C_GN = C_GROUPS * C_STATE
C_XBC = C_INNER + 2 * C_GN
C_CHUNK = 128
C_STEP_CHUNKS = 2
C_HPG = C_HEADS // C_GROUPS
DT_LANES = 128
HALO = 8

SUB_ROWS = 256

V7X_VMEM_LIMIT = 56 * 1024 * 1024


def _params(sem, vmem=V7X_VMEM_LIMIT):
    return pltpu.CompilerParams(dimension_semantics=sem, vmem_limit_bytes=vmem)


def _rms_mod(x, nw, shift, scale):
    ms = jnp.mean(x * x, axis=-1, keepdims=True)
    y = x * lax.rsqrt(ms + EPS) * nw
    return y * (1.0 + scale) + shift


def _silu(v):
    return v * jax.nn.sigmoid(v)


def _mod_kernel(c_ref, w_ref, b_ref, o_ref):
    sc = _silu(c_ref[...])
    o_ref[...] = jnp.dot(sc, w_ref[...], precision=HIGHEST, preferred_element_type=F32) + b_ref[...]


def _mod_call(cvec, w_mod, b_mod, tn=1152):
    nl, d, n = w_mod.shape
    return pl.pallas_call(
        _mod_kernel,
        grid=(nl, n // tn),
        in_specs=[pl.BlockSpec((8, d), lambda l, j: (0, 0)),
                  pl.BlockSpec((None, d, tn), lambda l, j: (l, 0, j)),
                  pl.BlockSpec((None, 1, tn), lambda l, j: (l, 0, j))],
        out_specs=pl.BlockSpec((None, 8, tn), lambda l, j: (l, 0, j)),
        out_shape=jax.ShapeDtypeStruct((nl, 8, n), F32),
        compiler_params=_params(("parallel", "parallel")),
        name="adaln_mod",
    )(cvec, w_mod, b_mod.reshape(nl, 1, n))


def _ffn_kernel(*refs, si, fc, ff, sub, nsub, ctx_subs, nbatch, split_input):
    if split_input:
        c_ref, x_ref, mod_ref, nw_ref, w13_ref, w2_ref, o_ref, a_scr = refs
    else:
        x_ref, mod_ref, nw_ref, w13_ref, w2_ref, o_ref, a_scr = refs
    m_lat, m_ctx = mod_ref[pl.program_id(0)], mod_ref[nbatch]
    for k in range(nsub):
        rows = slice(k * sub, (k + 1) * sub)
        is_ctx = pl.program_id(1) * nsub + k < ctx_subs
        x = jnp.where(is_ctx, c_ref[...], x_ref[...]) if split_input else x_ref[rows, :]
        m = jnp.where(is_ctx, m_ctx, m_lat)
        h = _rms_mod(x, nw_ref[...], m[si:si + 1], m[si + 1:si + 2]).astype(BF16)
        for c in range(ff // fc):
            g = jnp.dot(h, w13_ref[:, c * fc:(c + 1) * fc], preferred_element_type=F32)
            u = jnp.dot(h, w13_ref[:, ff + c * fc:ff + (c + 1) * fc], preferred_element_type=F32)
            a_scr[k, :, c * fc:(c + 1) * fc] = (_silu(g) * u).astype(BF16)
        y = jnp.dot(a_scr[k], w2_ref[...], preferred_element_type=F32)
        o_ref[rows, :] = x + (0.5 * m[si + 2:si + 3]) * y


def _ffn_call(xs, mod, nw, w13, w2, *, si, tm, fc, ctx_rows, nbatch, ctx_in=None):
    b, t, d = xs.shape
    ff = w2.shape[0]
    split = ctx_in is not None
    nsub = tm // SUB_ROWS
    ctx_subs = ctx_rows // SUB_ROWS
    assert not split or nsub == 1
    ntiles = t // tm + (ctx_subs if split else 0)
    if split:
        tok_specs = [pl.BlockSpec((None, tm, d), lambda bi, i: (bi, jnp.minimum(i, ctx_subs - 1), 0)),
                     pl.BlockSpec((None, tm, d), lambda bi, i: (bi, jnp.maximum(i - ctx_subs, 0), 0))]
        toks = (ctx_in, xs)
    else:
        tok_specs = [pl.BlockSpec((None, tm, d), lambda bi, i: (bi, i, 0))]
        toks = (xs,)
    return pl.pallas_call(
        functools.partial(_ffn_kernel, si=si, fc=fc, ff=ff, sub=SUB_ROWS, nsub=nsub, ctx_subs=ctx_subs,
                          nbatch=nbatch, split_input=split),
        grid=(b, ntiles),
        in_specs=tok_specs + [pl.BlockSpec(mod.shape, lambda bi, i: (0, 0, 0)),
                              pl.BlockSpec((1, d), lambda bi, i: (0, 0)),
                              pl.BlockSpec(w13.shape, lambda bi, i: (0, 0), pipeline_mode=pl.Buffered(1)),
                              pl.BlockSpec(w2.shape, lambda bi, i: (0, 0), pipeline_mode=pl.Buffered(1))],
        out_specs=pl.BlockSpec((None, tm, d), lambda bi, i: (bi, i, 0)),
        out_shape=jax.ShapeDtypeStruct((b, ntiles * tm, d), F32),
        scratch_shapes=[pltpu.VMEM((nsub, SUB_ROWS, ff), BF16)],
        compiler_params=_params(("parallel", "parallel")),
        name="swiglu_half",
    )(*toks, mod, nw.reshape(1, d), w13, w2)


def _rope(v, cos, sin_signed, quarter):
    n = v.shape[-1]
    lane = lax.broadcasted_iota(jnp.int32, v.shape, 1)
    first = (lane & (2 * quarter - 1)) < quarter
    vr = jnp.where(first, pltpu.roll(v, n - quarter, 1), pltpu.roll(v, quarter, 1))
    return v * cos + vr * sin_signed


def _group_norm(v, gmat, inv_n, w):
    sq = v * v
    hi = sq.astype(BF16)
    lo = (sq - hi.astype(F32)).astype(BF16)
    ms = (jnp.dot(hi, gmat, preferred_element_type=F32) + jnp.dot(lo, gmat, preferred_element_type=F32)) * inv_n
    return v * lax.rsqrt(ms + EPS) * w


def _inproj_kernel(x_ref, mod_ref, nw_ref, w_ref, ga_ref, gb_ref, hn_ref, kpl_ref, kone_ref, dup_ref, dupt_ref,
                   cosa_ref, sina_ref, cosb_ref, sinb_ref, qa_o, ka_o, va_o, qb_o, kb_o, vb_o, z_o, xbc_o, dt_o, *,
                   sub, nsub, ctx_subs, nbatch):
    m_lat, m_ctx = mod_ref[pl.program_id(0)], mod_ref[nbatch]
    hn = hn_ref[...]
    ga, gb = ga_ref[...], gb_ref[...]
    twice = lambda a: jnp.concatenate([a, a], axis=1)
    for k in range(nsub):
        rows = slice(k * sub, (k + 1) * sub)
        m = jnp.where(pl.program_id(1) * nsub + k < ctx_subs, m_ctx, m_lat)
        h = _rms_mod(x_ref[rows, :], nw_ref[...], m[3:4], m[4:5]).astype(BF16)
        p = jnp.dot(h, w_ref[...], preferred_element_type=F32)
        cosa, sina = twice(cosa_ref[rows, :]), twice(sina_ref[rows, :])
        cosb, sinb = twice(cosb_ref[rows, :]), twice(sinb_ref[rows, :])
        o = 0
        qa = _group_norm(p[:, o:o + A_QCOLS], ga, 1.0 / A_QK, hn[0:1]); o += A_QCOLS
        ka = _group_norm(p[:, o:o + A_QCOLS], ga, 1.0 / A_QK, hn[1:2]); o += A_QCOLS
        qa_o[:, rows] = (_rope(qa, cosa, sina, A_QK // 4) * (A_SCALE * LOG2E)).T.astype(BF16)
        ka16 = _rope(ka, cosa, sina, A_QK // 4).astype(BF16)
        ka_o[rows, :] = (jnp.dot(ka16, kpl_ref[...], preferred_element_type=F32) + kone_ref[...]).astype(BF16)
        va_o[:, rows] = p[:, o:o + A_HEADS * A_V].T.astype(BF16); o += A_HEADS * A_V
        qb = _group_norm(p[:, o:o + B_QCOLS], gb, 1.0 / B_DIM, hn[2:3]); o += B_QCOLS
        kb = _group_norm(p[:, o:o + B_KCOLS], gb[:B_KCOLS, :B_KCOLS], 1.0 / B_DIM, hn[3:4, :B_KCOLS]); o += B_KCOLS
        qb_o[rows, :] = (_rope(qb, cosb, sinb, B_DIM // 4) * B_SCALE).astype(BF16)
        kb16_t = _rope(kb, cosb[:, :B_KCOLS], sinb[:, :B_KCOLS], B_DIM // 4).T.astype(BF16)
        kb_o[:, rows] = jnp.dot(dupt_ref[...], kb16_t, preferred_element_type=F32).astype(BF16)
        vb_o[rows, :] = jnp.dot(p[:, o:o + B_KCOLS].astype(BF16), dup_ref[...],
                                preferred_element_type=F32).astype(BF16); o += B_KCOLS
        z_o[rows, :] = p[:, o:o + C_INNER]; o += C_INNER
        xbc_o[rows, :] = p[:, o:o + C_XBC]; o += C_XBC
        dt_o[rows, :] = p[:, o:o + DT_LANES]


def _inproj_call(xs, mod, nw, w_in_p, ga, gb, hn, kpl, kone, dup, dupt, cosa, sina, cosb, sinb, *, tm, ctx_rows, nbatch):
    b, t, d = xs.shape
    ntiles = t // tm
    tok = lambda w: pl.BlockSpec((None, tm, w), lambda bi, i: (bi, i, 0))
    tab = lambda w: pl.BlockSpec((tm, w), lambda bi, i: (i, 0))
    full = lambda a: pl.BlockSpec(a.shape, lambda bi, i: (0,) * a.ndim)
    widths = (A_QCOLS, A_HEADS * A_HLANES, A_HEADS * A_V, B_QCOLS, B_QCOLS, B_QCOLS, C_INNER, C_XBC, DT_LANES)
    transposed = (0, 2, 4)
    dtypes = (BF16,) * 6 + (F32,) * 3
    return pl.pallas_call(
        functools.partial(_inproj_kernel, sub=SUB_ROWS, nsub=tm // SUB_ROWS, ctx_subs=ctx_rows // SUB_ROWS,
                          nbatch=nbatch),
        grid=(b, ntiles),
        in_specs=[tok(d), full(mod), pl.BlockSpec((1, d), lambda bi, i: (0, 0)),
                  pl.BlockSpec(w_in_p.shape, lambda bi, i: (0, 0), pipeline_mode=pl.Buffered(1)),
                  full(ga), full(gb), full(hn), full(kpl), full(kone), full(dup), full(dupt)] + [tab(ROPE_LANES)] * 4,
        out_specs=[pl.BlockSpec((None, w, tm), lambda bi, i: (bi, 0, i)) if k in transposed else tok(w)
                   for k, w in enumerate(widths)],
        out_shape=[jax.ShapeDtypeStruct((b, w, t) if k in transposed else (b, t, w), dt)
                   for k, (w, dt) in enumerate(zip(widths, dtypes))],
        compiler_params=_params(("parallel", "parallel")),
        name="in_proj_heads",
    )(xs, mod, nw.reshape(1, d), w_in_p, ga, gb, hn, kpl, kone, dup, dupt, cosa, sina, cosb, sinb)


def _attn_a_kernel(lamv_ref, subln_ref, qt_ref, kx_ref, vt_ref, o_ref, kmax_scr, *,
                   tq, tk, ctx, nk, q_tile0, lam_init):
    step = pl.program_id(1)

    @pl.when(step == 0)
    def _():
        r = lax.broadcasted_iota(jnp.int32, (A_HLANES, A_HLANES), 0)
        c = lax.broadcasted_iota(jnp.int32, (A_HLANES, A_HLANES), 1)
        in_comp = jnp.logical_and(r >= c * A_QK, r < (c + 1) * A_QK)
        sel = jnp.where(jnp.logical_and(in_comp, c < 2), 1.0, 0.0).astype(BF16)
        for h in range(A_HEADS):
            kk = kx_ref[:, h * A_HLANES:(h + 1) * A_HLANES].astype(F32)
            ksq = jnp.dot((kk * kk).astype(BF16), sel, preferred_element_type=F32)
            kmax_scr[h:h + 1, :] = jnp.sqrt(jnp.max(ksq, axis=0, keepdims=True))

    lv = lamv_ref[...]
    lam = (jnp.exp(jnp.sum(lv[0:1] * lv[1:2], axis=1, keepdims=True))
           - jnp.exp(jnp.sum(lv[2:3] * lv[3:4], axis=1, keepdims=True)) + lam_init)
    qt = qt_ref[...].astype(F32)
    qcs = [qt[hc * A_QK:(hc + 1) * A_QK, :] for hc in range(2 * A_HEADS)]
    shifts = [jnp.sqrt(jnp.sum(qc * qc, axis=0, keepdims=True)) * kmax_scr[hc // 2:hc // 2 + 1, hc % 2:hc % 2 + 1]
              for hc, qc in enumerate(qcs)]
    shift_max = jnp.max(functools.reduce(jnp.maximum, shifts))

    def q_ext(h, shifted):
        z = jnp.zeros((A_QK, tq), F32)
        top = jnp.concatenate([qcs[2 * h], z], axis=1)
        mid = jnp.concatenate([z, qcs[2 * h + 1]], axis=1)
        row = lax.broadcasted_iota(jnp.int32, (A_HLANES - 2 * A_QK, 2 * tq), 0)
        if shifted:
            bot = jnp.where(row == 0, -jnp.concatenate([shifts[2 * h], shifts[2 * h + 1]], axis=1), 0.0)
        else:
            bot = jnp.zeros(row.shape, F32)
        return jnp.concatenate([top, mid, bot], axis=0).astype(BF16)

    def scores(qx, h, start, size):
        return jnp.dot(kx_ref[pl.ds(start, size), h * A_HLANES:(h + 1) * A_HLANES], qx, preferred_element_type=F32)

    def v_ext(h, start, size):
        ones = jnp.ones((A_VROWS - A_V, size), BF16)
        return jnp.concatenate([vt_ref[h * A_V:(h + 1) * A_V, pl.ds(start, size)], ones], axis=0)

    def finish(accs):
        rows = []
        for acc in accs:
            o = (acc[:A_V, :tq] / acc[A_V:A_V + 1, :tq]) - lam * (acc[:A_V, tq:] / acc[A_V:A_V + 1, tq:])
            ms = jnp.mean(o * o, axis=0, keepdims=True)
            rows.append(o * lax.rsqrt(ms + EPS) * subln_ref[...] * (1.0 - lam_init))
        o_ref[...] = jnp.concatenate(rows, axis=0).T

    def attend(shifted):
        qx = [q_ext(h, shifted) for h in range(A_HEADS)]

        def head_step(h, start, carry):
            s = scores(qx[h], h, start, tk)
            ve = v_ext(h, start, tk)
            if shifted:
                return carry + jnp.dot(ve, jnp.exp2(s).astype(BF16), preferred_element_type=F32)
            m_run, acc = carry
            m_new = jnp.maximum(m_run, jnp.max(s, axis=0, keepdims=True))
            p = jnp.exp2(s - m_new).astype(BF16)
            return m_new, jnp.exp2(m_run - m_new) * acc + jnp.dot(ve, p, preferred_element_type=F32)

        def body(j, carries):
            start = pl.multiple_of(j * tk, tk)
            return tuple(head_step(h, start, carries[h]) for h in range(A_HEADS))

        acc0 = jnp.zeros((A_VROWS, 2 * tq), F32)
        if shifted:
            accs = lax.fori_loop(0, nk, body, (acc0,) * A_HEADS, unroll=True)
        else:
            init = (jnp.full((1, 2 * tq), NEG_INF, F32), acc0)
            accs = [c[1] for c in lax.fori_loop(0, nk, body, (init,) * A_HEADS)]
        finish(accs)

    def attend_ctx():
        accs = []
        for h in range(A_HEADS):
            s = scores(q_ext(h, False), h, 0, ctx)
            p = jnp.exp2(s - jnp.max(s, axis=0, keepdims=True)).astype(BF16)
            accs.append(jnp.dot(v_ext(h, 0, ctx), p, preferred_element_type=F32))
        finish(accs)

    safe = shift_max < A_SHIFT_CAP
    if q_tile0 * tq < ctx:
        is_ctx = (step + q_tile0) * tq < ctx
        pl.when(is_ctx)(attend_ctx)
        pl.when(jnp.logical_and(jnp.logical_not(is_ctx), safe))(lambda: attend(True))
        pl.when(jnp.logical_and(jnp.logical_not(is_ctx), jnp.logical_not(safe)))(lambda: attend(False))
    else:
        pl.when(safe)(lambda: attend(True))
        pl.when(jnp.logical_not(safe))(lambda: attend(False))


def _attn_a_call(lamv, subln_col, qa_t, ka_x, va_t, *, tq, tk, ctx, q_tile0, lam_init):
    b, w, t = qa_t.shape
    nq = t // tq - q_tile0
    return pl.pallas_call(
        functools.partial(_attn_a_kernel, tq=tq, tk=tk, ctx=ctx, nk=t // tk, q_tile0=q_tile0, lam_init=lam_init),
        grid=(b, nq),
        in_specs=[pl.BlockSpec(lamv.shape, lambda bi, qi: (0, 0)),
                  pl.BlockSpec(subln_col.shape, lambda bi, qi: (0, 0)),
                  pl.BlockSpec((None, w, tq), lambda bi, qi: (bi, 0, qi + q_tile0)),
                  pl.BlockSpec((None, t, A_HEADS * A_HLANES), lambda bi, qi: (bi, 0, 0)),
                  pl.BlockSpec((None, w, t), lambda bi, qi: (bi, 0, 0))],
        out_specs=pl.BlockSpec((None, tq, w), lambda bi, qi: (bi, qi, 0)),
        out_shape=jax.ShapeDtypeStruct((b, nq * tq, w), F32),
        scratch_shapes=[pltpu.VMEM((8, A_HLANES), F32)],
        compiler_params=_params(("parallel", "arbitrary")),
        name="diff_attention",
    )(lamv, subln_col, qa_t, ka_x, va_t)


def _attn_b_kernel(sink_ref, band_ref, q_ref, kp_ref, k0_ref, k1_ref, kn_ref, kx_ref,
                   vp_ref, v0_ref, v1_ref, vn_ref, vx_ref, o_ref, *, blk0, ctx_blocks, nblk):
    n0 = pl.program_id(1) * B_QBLOCKS + blk0
    nq = B_QBLOCKS * B_BLOCK
    k_all = jnp.concatenate([kp_ref[...], k0_ref[...], k1_ref[...], kn_ref[...], kx_ref[...]], axis=1)
    v_all = jnp.concatenate([vp_ref[...], v0_ref[...], v1_ref[...], vn_ref[...], vx_ref[...]], axis=0)
    lane = lax.broadcasted_iota(jnp.int32, (1, k_all.shape[1]), 1)
    lat = n0 >= ctx_blocks
    pen = jnp.zeros(lane.shape, F32)
    for d in range(B_QBLOCKS + 2):
        kb = n0 - 1 + d
        ok = jnp.logical_and(lat, jnp.logical_and(kb >= ctx_blocks, kb <= nblk - 1))
        in_blk = jnp.logical_and(lane >= d * B_BLOCK, lane < (d + 1) * B_BLOCK)
        pen = jnp.where(in_blk, jnp.where(ok, 0.0, NEG_INF), pen)
    bias = band_ref[...] + pen
    hpg = B_HEADS // B_KV_HEADS
    gl = hpg * B_DIM
    qlane = lax.broadcasted_iota(jnp.int32, (nq, gl), 1) // B_DIM
    outs = []
    for g in range(B_KV_HEADS):
        q_g = q_ref[:, g * gl:(g + 1) * gl]
        o_g = jnp.zeros((nq, gl), F32)
        for e in range(hpg):
            q_e = jnp.where(qlane == e, q_g, jnp.zeros_like(q_g))
            s = jnp.dot(q_e, k_all[g * gl:(g + 1) * gl, :], preferred_element_type=F32) + bias
            sk = sink_ref[g * hpg + e]
            m = jnp.maximum(jnp.max(s, axis=1, keepdims=True), sk)
            p = jnp.exp(s - m)
            den = jnp.sum(p, axis=1, keepdims=True) + jnp.exp(sk - m)
            pv = jnp.dot(p.astype(BF16), v_all[:, g * gl:(g + 1) * gl], preferred_element_type=F32)
            o_g = jnp.where(qlane == e, pv / den, o_g)
        outs.append(o_g)
    o_ref[...] = jnp.concatenate(outs, axis=1)


def _attn_b_call(sink, band, qb, kb_t, vb2, *, ctx, blk0):
    b, t, w = qb.shape
    nblk = t // B_BLOCK
    ctx_blocks = ctx // B_BLOCK
    nq = B_QBLOCKS * B_BLOCK
    first = lambda i: i * B_QBLOCKS + blk0
    clamp = lambda n: jnp.clip(n, 0, nblk - 1)
    kspec = lambda d: pl.BlockSpec((None, w, B_BLOCK), lambda bi, i: (bi, 0, clamp(first(i) - 1 + d)))
    vspec = lambda d: pl.BlockSpec((None, B_BLOCK, w), lambda bi, i: (bi, clamp(first(i) - 1 + d), 0))
    nwb = B_QBLOCKS + 2
    return pl.pallas_call(
        functools.partial(_attn_b_kernel, blk0=blk0, ctx_blocks=ctx_blocks, nblk=nblk),
        grid=(b, (nblk - blk0) // B_QBLOCKS),
        in_specs=[pl.BlockSpec(memory_space=pltpu.SMEM),
                  pl.BlockSpec(band.shape, lambda bi, i: (0, 0)),
                  pl.BlockSpec((None, nq, w), lambda bi, i: (bi, i + blk0 // B_QBLOCKS, 0))]
                 + [kspec(d) for d in range(nwb)] + [pl.BlockSpec((None, w, ctx), lambda bi, i: (bi, 0, 0))]
                 + [vspec(d) for d in range(nwb)] + [pl.BlockSpec((None, ctx, w), lambda bi, i: (bi, 0, 0))],
        out_specs=pl.BlockSpec((None, nq, w), lambda bi, i: (bi, i, 0)),
        out_shape=jax.ShapeDtypeStruct((b, (nblk - blk0) * B_BLOCK, w), F32),
        compiler_params=_params(("parallel", "parallel")),
        name="window_attention",
    )(sink, band, qb, *([kb_t] * (nwb + 1)), *([vb2] * (nwb + 1)))


def _conv_kernel(u_ref, up_ref, un_ref, w_ref, b_ref, xt_o, bn_o, *, tm, ctx_tiles, ntiles):
    i = pl.program_id(1)
    u = u_ref[...]
    has_prev = jnp.logical_and(i != 0, i != ctx_tiles)
    has_next = jnp.logical_and(i != ctx_tiles - 1, i != ntiles - 1)
    up = jnp.where(has_prev, up_ref[...], 0.0)
    un = jnp.where(has_next, un_ref[...], 0.0)
    full = jnp.concatenate([up, u, un], axis=0)
    w = w_ref[...]
    acc = b_ref[...] + w[0:1] * full[HALO - 2:HALO - 2 + tm]
    for k in range(1, C_CONV):
        acc = acc + w[k:k + 1] * full[HALO - 2 + k:HALO - 2 + k + tm]
    y = _silu(acc)
    xt_o[...] = y.T
    bn_o[...] = y[:, C_INNER:C_INNER + C_GN]


def _conv_call(xbc_raw, conv_w, conv_b, *, tm, ctx):
    b, t, ch = xbc_raw.shape
    ntiles = t // tm
    hb = tm // HALO
    nh = t // HALO
    w8 = jnp.zeros((8, ch), F32).at[:C_CONV].set(conv_w)
    return pl.pallas_call(
        functools.partial(_conv_kernel, tm=tm, ctx_tiles=ctx // tm, ntiles=ntiles),
        grid=(b, ntiles),
        in_specs=[pl.BlockSpec((None, tm, ch), lambda bi, i: (bi, i, 0)),
                  pl.BlockSpec((None, HALO, ch), lambda bi, i: (bi, jnp.maximum(i * hb - 1, 0), 0)),
                  pl.BlockSpec((None, HALO, ch), lambda bi, i: (bi, jnp.minimum((i + 1) * hb, nh - 1), 0)),
                  pl.BlockSpec((8, ch), lambda bi, i: (0, 0)),
                  pl.BlockSpec((1, ch), lambda bi, i: (0, 0))],
        out_specs=[pl.BlockSpec((None, ch, tm), lambda bi, i: (bi, 0, i)),
                   pl.BlockSpec((None, tm, C_GN), lambda bi, i: (bi, i, 0))],
        out_shape=[jax.ShapeDtypeStruct((b, ch, t), F32), jax.ShapeDtypeStruct((b, t, C_GN), F32)],
        compiler_params=_params(("parallel", "parallel")),
        name="ssm_conv",
    )(xbc_raw, xbc_raw, xbc_raw, w8, conv_b.reshape(1, ch))


def _softplus(v):
    return jnp.maximum(v, 0.0) + jnp.log1p(jnp.exp(-jnp.abs(v)))


def _split3(v):
    v1 = v.astype(BF16)
    r1 = v - v1.astype(F32)
    v2 = r1.astype(BF16)
    v3 = (r1 - v2.astype(F32)).astype(BF16)
    return v1, v2, v3


def _ssd_prepare(xt_all, bn, dt_raw, par, *, lane0, backward):
    c_t = xt_all[C_INNER + C_GN:]
    dt = _softplus(dt_raw + par[0:1])
    a = dt * par[1:2]
    s_idx = lax.broadcasted_iota(jnp.int32, (C_CHUNK, C_CHUNK), 0)
    l_idx = lax.broadcasted_iota(jnp.int32, (C_CHUNK, C_CHUNK), 1)
    tri = jnp.where(l_idx <= s_idx, 1.0, 0.0).astype(BF16)
    cs = sum(jnp.dot(tri, part, preferred_element_type=F32) for part in _split3(a))
    key = cs - a if backward else cs
    k1, k2, k3 = _split3(key)
    parts = (k1.astype(F32) + pltpu.roll(k2.astype(F32), 2 * C_HEADS, 1)
             + pltpu.roll(k3.astype(F32), 4 * C_HEADS, 1)).astype(BF16)
    src = lax.broadcasted_iota(jnp.int32, (DT_LANES, C_HEADS * C_CHUNK), 0)
    dst = lax.broadcasted_iota(jnp.int32, (DT_LANES, C_HEADS * C_CHUNK), 1)
    pick = jnp.logical_and(src < 6 * C_HEADS, (src & (2 * C_HEADS - 1)) == lane0 + dst // C_CHUNK)
    colb_all = jnp.dot(parts, jnp.where(pick, 1.0, 0.0).astype(BF16), preferred_element_type=F32)
    return dict(x_t=xt_all[:C_INNER], c_t=c_t, c16=c_t.astype(BF16), bn=bn, dt_t=dt.T, key_t=key.T, cs=cs,
                colb_all=colb_all, mask=(s_idx >= l_idx) if backward else (s_idx <= l_idx))


def _ssd_group(st, g, hs, *, lane0, backward):
    lane_g = lax.broadcasted_iota(jnp.int32, st["bn"].shape, 1) // C_STATE
    row_g = lax.broadcasted_iota(jnp.int32, st["c_t"].shape, 0) // C_STATE
    bn_g = jnp.where(lane_g == g, st["bn"], 0.0).astype(BF16)
    ct_g = jnp.where(row_g == g, st["c_t"], 0.0).astype(BF16)
    cb_t = jnp.dot(bn_g, st["c16"], preferred_element_type=F32)
    y_off = jnp.dot(hs.astype(BF16), ct_g, preferred_element_type=F32)
    y_g, xw_g, hs_new = [], [], []
    for r in range(C_HPG):
        h = g * C_HPG + r
        hl = lane0 + h
        colb = st["colb_all"][:, h * C_CHUNK:(h + 1) * C_CHUNK]
        row = st["key_t"][hl:hl + 1, :]
        tot_h = st["cs"][C_CHUNK - 1:C_CHUNK, hl:hl + 1]
        d = (colb - row) if backward else (row - colb)
        m_t = (cb_t * jnp.exp(jnp.where(st["mask"], d, NEG_INF))).astype(BF16)
        xdt = st["x_t"][h * C_HEAD_DIM:(h + 1) * C_HEAD_DIM] * st["dt_t"][hl:hl + 1, :]
        y_diag = jnp.dot(xdt.astype(BF16), m_t, preferred_element_type=F32)
        carry = jnp.exp(tot_h - row) if backward else jnp.exp(row)
        y_g.append(y_diag + y_off[r * C_HEAD_DIM:(r + 1) * C_HEAD_DIM] * carry)
        w_row = jnp.exp(row) if backward else jnp.exp(tot_h - row)
        xw_g.append((xdt * w_row).astype(BF16))
        hs_new.append(jnp.exp(tot_h) * hs[r * C_HEAD_DIM:(r + 1) * C_HEAD_DIM])
    state = jnp.concatenate(hs_new, axis=0) + jnp.dot(jnp.concatenate(xw_g, axis=0), bn_g, preferred_element_type=F32)
    return y_g, state


def _ssd_kernel(par_ref, dsk_ref, xf_ref, bf_ref, dtf_ref, xb_ref, bb_ref, dtb_ref, yf_ref, yb_ref,
                hf_scr, hb_scr, dsk_scr):
    @pl.when(pl.program_id(1) == 0)
    def _():
        hf_scr[...] = jnp.zeros(hf_scr.shape, F32)
        hb_scr[...] = jnp.zeros(hb_scr.shape, F32)
        dsk_scr[...] = jnp.broadcast_to(dsk_ref[...], dsk_scr.shape)

    par = par_ref[...]
    hf = [hf_scr[g] for g in range(C_GROUPS)]
    hb = [hb_scr[g] for g in range(C_GROUPS)]
    for k in range(C_STEP_CHUNKS):
        cf = slice(k * C_CHUNK, (k + 1) * C_CHUNK)
        cb = slice((C_STEP_CHUNKS - 1 - k) * C_CHUNK, (C_STEP_CHUNKS - k) * C_CHUNK)
        sf = _ssd_prepare(xf_ref[:, cf], bf_ref[cf, :], dtf_ref[cf, :], par, lane0=0, backward=False)
        sb = _ssd_prepare(xb_ref[:, cb], bb_ref[cb, :], dtb_ref[cb, :], par, lane0=C_HEADS, backward=True)
        rows_f, rows_b = [], []
        for g in range(C_GROUPS):
            y_g, hf[g] = _ssd_group(sf, g, hf[g], lane0=0, backward=False)
            rows_f += y_g
            y_g, hb[g] = _ssd_group(sb, g, hb[g], lane0=C_HEADS, backward=True)
            rows_b += y_g
        yf_ref[cf, :] = (jnp.concatenate(rows_f, axis=0) + dsk_scr[...] * sf["x_t"]).T
        yb_ref[cb, :] = jnp.concatenate(rows_b, axis=0).T
    for g in range(C_GROUPS):
        hf_scr[g] = hf[g]
        hb_scr[g] = hb[g]


def _ssd_call(par, dsk_col, xbc_t, bn, dt, *, ctx):
    b, ch, t = xbc_t.shape
    blk = C_STEP_CHUNKS * C_CHUNK
    nblk = t // blk
    ncb = ctx // blk
    bblock = lambda j: jnp.where(j < ncb, ncb - 1 - j, nblk - 1 + ncb - j)
    fwd = lambda bi, j: (bi, j, 0)
    bwd = lambda bi, j: (bi, bblock(j), 0)
    fwd_t = lambda bi, j: (bi, 0, j)
    bwd_t = lambda bi, j: (bi, 0, bblock(j))
    tspec = lambda im: pl.BlockSpec((None, ch, blk), im)
    nspec = lambda w, im: pl.BlockSpec((None, blk, w), im)
    return pl.pallas_call(
        _ssd_kernel,
        grid=(b, nblk),
        in_specs=[pl.BlockSpec((8, DT_LANES), lambda bi, j: (0, 0)),
                  pl.BlockSpec(dsk_col.shape, lambda bi, j: (0, 0)),
                  tspec(fwd_t), nspec(C_GN, fwd), nspec(DT_LANES, fwd),
                  tspec(bwd_t), nspec(C_GN, bwd), nspec(DT_LANES, bwd)],
        out_specs=[nspec(C_INNER, fwd), nspec(C_INNER, bwd)],
        out_shape=[jax.ShapeDtypeStruct((b, t, C_INNER), F32)] * 2,
        scratch_shapes=[pltpu.VMEM((C_GROUPS, C_HPG * C_HEAD_DIM, C_GN), F32)] * 2
                       + [pltpu.VMEM((C_INNER, C_CHUNK), F32)],
        compiler_params=_params(("parallel", "arbitrary")),
        name="ssd_scan",
    )(par, dsk_col, xbc_t, bn, dt, xbc_t, bn, dt)


def _outproj_kernel(x_ref, mod_ref, oa_ref, ob_ref, yf_ref, yb_ref, z_ref, gn_ref, w_ref, o_ref, *,
                    sub, nsub, sub0, ctx_subs, nbatch):
    m_lat, m_ctx = mod_ref[pl.program_id(0)], mod_ref[nbatch]
    gn = gn_ref[...]
    gw = C_INNER // C_GROUPS
    for k in range(nsub):
        rows = slice(k * sub, (k + 1) * sub)
        m = jnp.where(pl.program_id(1) * nsub + k + sub0 < ctx_subs, m_ctx, m_lat)
        y = (yf_ref[rows, :] + yb_ref[rows, :]) * _silu(z_ref[rows, :])
        oc = []
        for g in range(C_GROUPS):
            yg = y[:, g * gw:(g + 1) * gw]
            ms = jnp.mean(yg * yg, axis=1, keepdims=True)
            oc.append(yg * lax.rsqrt(ms + EPS) * gn[:, g * gw:(g + 1) * gw])
        mix = jnp.concatenate([oa_ref[rows, :], ob_ref[rows, :]] + oc, axis=1).astype(BF16)
        o_ref[rows, :] = x_ref[rows, :] + m[5:6] * jnp.dot(mix, w_ref[...], preferred_element_type=F32)


def _outproj_call(xs, mod, oa, ob, yf, yb, z, gn, w_out, *, tm, ctx_rows, row0, nbatch):
    b, t, d = xs.shape
    assert row0 % tm == 0
    ntiles = (t - row0) // tm
    tile0 = row0 // tm
    tok = lambda w: pl.BlockSpec((None, tm, w), lambda bi, i: (bi, i + tile0, 0))
    tok0 = lambda w: pl.BlockSpec((None, tm, w), lambda bi, i: (bi, i, 0))
    return pl.pallas_call(
        functools.partial(_outproj_kernel, sub=SUB_ROWS, nsub=tm // SUB_ROWS, sub0=row0 // SUB_ROWS,
                          ctx_subs=ctx_rows // SUB_ROWS, nbatch=nbatch),
        grid=(b, ntiles),
        in_specs=[tok(d), pl.BlockSpec(mod.shape, lambda bi, i: (0, 0, 0)),
                  tok0(A_HEADS * A_V), tok0(B_QCOLS), tok(C_INNER), tok(C_INNER), tok(C_INNER),
                  pl.BlockSpec(gn.shape, lambda bi, i: (0, 0)),
                  pl.BlockSpec(w_out.shape, lambda bi, i: (0, 0), pipeline_mode=pl.Buffered(1))],
        out_specs=tok0(d),
        out_shape=jax.ShapeDtypeStruct((b, ntiles * tm, d), F32),
        compiler_params=_params(("parallel", "parallel")),
        name="mix_out_proj",
    )(xs, mod, oa, ob, yf, yb, z, gn, w_out)


def _rope_tables(seq, ctx, dim, reps):
    rows = seq // GRID_W
    row = jnp.repeat(jnp.arange(rows, dtype=F32), GRID_W)
    col = jnp.tile(jnp.arange(GRID_W, dtype=F32), rows)
    quarter = dim // 4
    inv_freq = ROPE_BASE ** (-jnp.arange(quarter, dtype=F32) / quarter)
    ar = row[:, None] * inv_freq
    ac = col[:, None] * inv_freq
    ang = jnp.concatenate([ar, ar, ac, ac], axis=-1)
    sign = jnp.where((jnp.arange(dim) % (dim // 2)) < quarter, -1.0, 1.0).astype(F32)
    cos = jnp.concatenate([jnp.ones((ctx, dim), F32), jnp.cos(ang)], axis=0)
    sin = jnp.concatenate([jnp.zeros((ctx, dim), F32), jnp.sin(ang) * sign], axis=0)
    return jnp.tile(cos, (1, reps)), jnp.tile(sin, (1, reps))


def _group_matrix(n, group):
    idx = jnp.arange(n) // group
    return (idx[:, None] == idx[None, :]).astype(BF16)


def kernel(x, c, ctx, c_ctx, w_mod, b_mod, norm_ffn1, ffn1_w13, ffn1_w2, norm_mix, w_in, w_out, qn_a, kn_a, lam_q1, lam_k1, lam_q2, lam_k2, subln_a, qn_b, kn_b, sink_b, conv_w, conv_b, dt_bias, a_log, d_skip, gnorm_c, norm_ffn2, ffn2_w13, ffn2_w2):
    nb, seq, d = x.shape
    nctx = ctx.shape[1]
    t = nctx + seq
    depth = w_mod.shape[0]
    tm = 256
    big = lambda rows: next(c for c in (768, 512, 256) if rows % c == 0)
    tm_all, tm_lat = big(t), big(seq)
    fc = 256
    tq, tk = 256, (768 if t % 768 == 0 and t > 768 else 256)
    assert nctx == tm == SUB_ROWS and nctx % (C_STEP_CHUNKS * C_CHUNK) == 0 and seq % GRID_W == 0 and nb < 8

    xs = x
    cvec = jnp.zeros((8, d), F32).at[:nb].set(c).at[nb].set(c_ctx)
    mod = _mod_call(cvec, w_mod, b_mod).reshape(depth, 8, N_MOD, d)

    cosa, sina = _rope_tables(seq, nctx, A_QK, ROPE_LANES // A_QK)
    cosb, sinb = _rope_tables(seq, nctx, B_DIM, ROPE_LANES // B_DIM)
    ga = _group_matrix(A_QCOLS, A_QK)
    gb = _group_matrix(B_QCOLS, B_DIM)
    src = jnp.arange(A_QCOLS)
    dst = (src // (2 * A_QK)) * A_HLANES + src % (2 * A_QK)
    kpl = (dst[:, None] == jnp.arange(A_HEADS * A_HLANES)[None, :]).astype(BF16)
    kone = (jnp.arange(A_HEADS * A_HLANES) % A_HLANES == 2 * A_QK).astype(F32).reshape(1, -1)
    dsrc = jnp.arange(B_KCOLS)
    dcol = jnp.arange(B_QCOLS)
    hpg = B_HEADS // B_KV_HEADS
    dup = jnp.logical_and(dcol[None, :] // (hpg * B_DIM) == dsrc[:, None] // B_DIM,
                          dcol[None, :] % B_DIM == dsrc[:, None] % B_DIM).astype(BF16)
    qi = jnp.arange(B_QBLOCKS * B_BLOCK)[:, None]
    kj = jnp.arange((B_QBLOCKS + 2) * B_BLOCK + nctx)[None, :]
    in_band = jnp.logical_and(kj - qi >= 0, kj - qi <= 2 * B_BLOCK)
    band = jnp.where(jnp.logical_or(in_band, kj >= (B_QBLOCKS + 2) * B_BLOCK), 0.0, NEG_INF).astype(F32)
    in_cols = w_in.shape[2]
    in_pad = (-in_cols) % DT_LANES

    for l in range(depth):
        last = l == depth - 1
        lam_init = 0.8 - 0.6 * math.exp(-0.3 * l)
        w_in_p = jnp.pad(w_in[l], ((0, 0), (0, in_pad))).astype(BF16)
        hn = jnp.stack([jnp.tile(qn_a[l], A_QCOLS // A_QK), jnp.tile(kn_a[l], A_QCOLS // A_QK),
                        jnp.tile(qn_b[l], B_QCOLS // B_DIM), jnp.tile(kn_b[l], B_QCOLS // B_DIM)]).astype(F32)
        hn = jnp.zeros((8, A_QCOLS), F32).at[:4].set(hn)

        xs = _ffn_call(xs, mod[l], norm_ffn1[l], ffn1_w13[l].astype(BF16), ffn1_w2[l].astype(BF16), si=0, fc=fc,
                       tm=(SUB_ROWS if l == 0 else tm_all), ctx_rows=nctx, nbatch=nb, ctx_in=(ctx if l == 0 else None))

        qa_t, ka_x, va_t, qb, kb_t, vb2, z, xbc_raw, dt = _inproj_call(
            xs, mod[l], norm_mix[l], w_in_p, ga, gb, hn, kpl, kone, dup, dup.T, cosa, sina, cosb, sinb,
            tm=tm_all, ctx_rows=nctx, nbatch=nb)

        lamv = jnp.zeros((8, A_QK), F32).at[:4].set(jnp.stack([lam_q1[l], lam_k1[l], lam_q2[l], lam_k2[l]]))
        oa = _attn_a_call(lamv, subln_a[l].reshape(A_V, 1), qa_t, ka_x, va_t, tq=tq, tk=tk, ctx=nctx,
                          q_tile0=(nctx // tq if last else 0), lam_init=lam_init)

        ob = _attn_b_call(sink_b[l].astype(F32), band, qb, kb_t, vb2, ctx=nctx, blk0=(nctx // B_BLOCK if last else 0))

        xbc_t, bn = _conv_call(xbc_raw, conv_w[l], conv_b[l], tm=tm, ctx=nctx)
        par = (jnp.zeros((8, DT_LANES), F32)
               .at[0, :2 * C_HEADS].set(dt_bias[l].reshape(-1))
               .at[1, :2 * C_HEADS].set(-jnp.exp(a_log[l].astype(F32)).reshape(-1)))
        dsk_col = jnp.repeat(d_skip[l].astype(F32), C_HEAD_DIM).reshape(C_INNER, 1)
        yf, yb = _ssd_call(par, dsk_col, xbc_t, bn, dt, ctx=nctx)

        xs = _outproj_call(xs, mod[l], oa, ob, yf, yb, z, gnorm_c[l].reshape(1, C_INNER), w_out[l].astype(BF16),
                           tm=(SUB_ROWS if last else tm_all), ctx_rows=nctx, row0=(nctx if last else 0), nbatch=nb)
        xs = _ffn_call(xs, mod[l], norm_ffn2[l], ffn2_w13[l].astype(BF16), ffn2_w2[l].astype(BF16), si=6, fc=fc,
                       tm=(tm_lat if last else tm_all), ctx_rows=(0 if last else nctx), nbatch=nb)
    return xs
```

```python
import functools
import math

import jax
import jax.numpy as jnp
from jax import lax
from jax.experimental import pallas as pl
from jax.experimental.pallas import tpu as pltpu

F32 = jnp.float32
BF16 = jnp.bfloat16
NEG_INF = float("-inf")

EPS = 1e-6
ROPE_BASE = 10000.0
GRID_W = 64
ROPE_LANES = 128
N_MOD = 9

A_HEADS, A_QK, A_V = 4, 32, 64
A_QCOLS = A_HEADS * 2 * A_QK
A_SCALE = A_QK ** -0.5
LOG2E = math.log2(math.e)
A_HLANES = 128
A_VROWS = A_V + 16
A_SHIFT_CAP = 48.0
B_HEADS, B_KV_HEADS, B_DIM = 4, 2, 64
B_QCOLS = B_HEADS * B_DIM
B_KCOLS = B_KV_HEADS * B_DIM
B_SCALE = B_DIM ** -0.5
B_BLOCK = 128
B_QBLOCKS = 2
C_HEADS, C_HEAD_DIM, C_GROUPS, C_STATE, C_CONV = 8, 64, 2, 64, 5
C_INNER = C_HEADS * C_HEAD_DIM
C_GN = C_GROUPS * C_STATE
C_XBC = C_INNER + 2 * C_GN
C_CHUNK = 128
C_STEP_CHUNKS = 2
C_HPG = C_HEADS // C_GROUPS
DT_LANES = 128
HALO = 8

SUB_ROWS = 256

V7X_VMEM_LIMIT = 56 * 1024 * 1024


def _params(sem, vmem=V7X_VMEM_LIMIT):
    return pltpu.CompilerParams(dimension_semantics=sem, vmem_limit_bytes=vmem)


def _rms_mod(x, nw, shift, scale):
    ms = jnp.mean(x * x, axis=-1, keepdims=True)
    y = x * lax.rsqrt(ms + EPS) * nw
    return y * (1.0 + scale) + shift


def _silu(v):
    return v * jax.nn.sigmoid(v)


def _mod_kernel(c_ref, w_ref, b_ref, o_ref):
    sc = _silu(c_ref[...])
    c1 = sc.astype(BF16)
    c2 = (sc - c1.astype(F32)).astype(BF16)
    w = w_ref[...]
    w1 = w.astype(BF16)
    w2 = (w - w1.astype(F32)).astype(BF16)
    dot = lambda a, bb: jnp.dot(a, bb, preferred_element_type=F32)
    o_ref[...] = dot(c1, w1) + (dot(c1, w2) + dot(c2, w1)) + b_ref[...]


def _mod_call(cvec, w_mod, b_mod, tn=1152):
    nl, d, n = w_mod.shape
    return pl.pallas_call(
        _mod_kernel,
        grid=(nl, n // tn),
        in_specs=[pl.BlockSpec((8, d), lambda l, j: (0, 0)),
                  pl.BlockSpec((None, d, tn), lambda l, j: (l, 0, j)),
                  pl.BlockSpec((None, 1, tn), lambda l, j: (l, 0, j))],
        out_specs=pl.BlockSpec((None, 8, tn), lambda l, j: (l, 0, j)),
        out_shape=jax.ShapeDtypeStruct((nl, 8, n), F32),
        compiler_params=_params(("parallel", "parallel")),
        name="adaln_mod",
    )(cvec, w_mod, b_mod.reshape(nl, 1, n))


def _ffn_kernel(*refs, si, fc, ff, sub, nsub, ctx_subs, nbatch, split_input):
    if split_input:
        c_ref, x_ref, mod_ref, nw_ref, w13_ref, w2_ref, o_ref, a_scr = refs
    else:
        x_ref, mod_ref, nw_ref, w13_ref, w2_ref, o_ref, a_scr = refs
    m_lat, m_ctx = mod_ref[pl.program_id(0)], mod_ref[nbatch]
    for k in range(nsub):
        rows = slice(k * sub, (k + 1) * sub)
        is_ctx = pl.program_id(1) * nsub + k < ctx_subs
        x = jnp.where(is_ctx, c_ref[...], x_ref[...]) if split_input else x_ref[rows, :]
        m = jnp.where(is_ctx, m_ctx, m_lat)
        h = _rms_mod(x, nw_ref[...], m[si:si + 1], m[si + 1:si + 2]).astype(BF16)
        for c in range(ff // fc):
            g = jnp.dot(h, w13_ref[:, c * fc:(c + 1) * fc], preferred_element_type=F32)
            u = jnp.dot(h, w13_ref[:, ff + c * fc:ff + (c + 1) * fc], preferred_element_type=F32)
            a_scr[k, :, c * fc:(c + 1) * fc] = (_silu(g) * u).astype(BF16)
        y = jnp.dot(a_scr[k], w2_ref[...], preferred_element_type=F32)
        o_ref[rows, :] = x + (0.5 * m[si + 2:si + 3]) * y


def _ffn_call(xs, mod, nw, w13, w2, layer, *, si, tm, fc, ctx_rows, nbatch, ctx_in=None):
    b, t, d = xs.shape
    ff = w2.shape[1]
    split = ctx_in is not None
    nsub = tm // SUB_ROWS
    ctx_subs = ctx_rows // SUB_ROWS
    assert not split or nsub == 1
    ntiles = t // tm + (ctx_subs if split else 0)
    if split:
        tok_specs = [pl.BlockSpec((None, tm, d), lambda bi, i: (bi, jnp.minimum(i, ctx_subs - 1), 0)),
                     pl.BlockSpec((None, tm, d), lambda bi, i: (bi, jnp.maximum(i - ctx_subs, 0), 0))]
        toks = (ctx_in, xs)
    else:
        tok_specs = [pl.BlockSpec((None, tm, d), lambda bi, i: (bi, i, 0))]
        toks = (xs,)
    return pl.pallas_call(
        functools.partial(_ffn_kernel, si=si, fc=fc, ff=ff, sub=SUB_ROWS, nsub=nsub, ctx_subs=ctx_subs,
                          nbatch=nbatch, split_input=split),
        grid=(b, ntiles),
        in_specs=tok_specs + [pl.BlockSpec(mod.shape, lambda bi, i: (0, 0, 0)),
                              pl.BlockSpec((1, d), lambda bi, i: (0, 0)),
                              pl.BlockSpec((None,) + w13.shape[1:], lambda bi, i: (layer, 0, 0),
                                           pipeline_mode=pl.Buffered(1)),
                              pl.BlockSpec((None,) + w2.shape[1:], lambda bi, i: (layer, 0, 0),
                                           pipeline_mode=pl.Buffered(1))],
        out_specs=pl.BlockSpec((None, tm, d), lambda bi, i: (bi, i, 0)),
        out_shape=jax.ShapeDtypeStruct((b, ntiles * tm, d), F32),
        scratch_shapes=[pltpu.VMEM((nsub, SUB_ROWS, ff), BF16)],
        compiler_params=_params(("parallel", "parallel")),
        name="swiglu_half",
    )(*toks, mod, nw.reshape(1, d), w13, w2)


def _rope(v, cos, sin_signed, quarter):
    n = v.shape[-1]
    lane = lax.broadcasted_iota(jnp.int32, v.shape, 1)
    first = (lane & (2 * quarter - 1)) < quarter
    vr = jnp.where(first, pltpu.roll(v, n - quarter, 1), pltpu.roll(v, quarter, 1))
    return v * cos + vr * sin_signed


def _group_norm(v, gmat, inv_n, w):
    sq = v * v
    hi = sq.astype(BF16)
    lo = (sq - hi.astype(F32)).astype(BF16)
    ms = (jnp.dot(hi, gmat, preferred_element_type=F32) + jnp.dot(lo, gmat, preferred_element_type=F32)) * inv_n
    return v * lax.rsqrt(ms + EPS) * w


def _inproj_kernel(x_ref, mod_ref, nw_ref, w_ref, ga_ref, gb_ref, hn_ref, kpl_ref, kone_ref, dup_ref, dupt_ref,
                   cosa_ref, sina_ref, cosb_ref, sinb_ref, qa_o, ka_o, va_o, qb_o, kb_o, vb_o, z_o, xbc_o, dt_o, *,
                   sub, nsub, ctx_subs, nbatch):
    m_lat, m_ctx = mod_ref[pl.program_id(0)], mod_ref[nbatch]
    hn = hn_ref[...]
    ga, gb = ga_ref[...], gb_ref[...]
    twice = lambda a: jnp.concatenate([a, a], axis=1)
    for k in range(nsub):
        rows = slice(k * sub, (k + 1) * sub)
        m = jnp.where(pl.program_id(1) * nsub + k < ctx_subs, m_ctx, m_lat)
        h = _rms_mod(x_ref[rows, :], nw_ref[...], m[3:4], m[4:5]).astype(BF16)
        p = jnp.dot(h, w_ref[...], preferred_element_type=F32)
        cosa, sina = twice(cosa_ref[rows, :]), twice(sina_ref[rows, :])
        cosb, sinb = twice(cosb_ref[rows, :]), twice(sinb_ref[rows, :])
        o = 0
        qa = _group_norm(p[:, o:o + A_QCOLS], ga, 1.0 / A_QK, hn[0:1]); o += A_QCOLS
        ka = _group_norm(p[:, o:o + A_QCOLS], ga, 1.0 / A_QK, hn[1:2]); o += A_QCOLS
        qa_o[:, rows] = (_rope(qa, cosa, sina, A_QK // 4) * (A_SCALE * LOG2E)).T.astype(BF16)
        ka16 = _rope(ka, cosa, sina, A_QK // 4).astype(BF16)
        ka_o[rows, :] = (jnp.dot(ka16, kpl_ref[...], preferred_element_type=F32) + kone_ref[...]).astype(BF16)
        va_o[:, rows] = p[:, o:o + A_HEADS * A_V].T.astype(BF16); o += A_HEADS * A_V
        qb = _group_norm(p[:, o:o + B_QCOLS], gb, 1.0 / B_DIM, hn[2:3]); o += B_QCOLS
        kb = _group_norm(p[:, o:o + B_KCOLS], gb[:B_KCOLS, :B_KCOLS], 1.0 / B_DIM, hn[3:4, :B_KCOLS]); o += B_KCOLS
        qb_o[rows, :] = (_rope(qb, cosb, sinb, B_DIM // 4) * B_SCALE).astype(BF16)
        kb16_t = _rope(kb, cosb[:, :B_KCOLS], sinb[:, :B_KCOLS], B_DIM // 4).T.astype(BF16)
        kb_o[:, rows] = jnp.dot(dupt_ref[...], kb16_t, preferred_element_type=F32).astype(BF16)
        vb_o[rows, :] = jnp.dot(p[:, o:o + B_KCOLS].astype(BF16), dup_ref[...],
                                preferred_element_type=F32).astype(BF16); o += B_KCOLS
        z_o[rows, :] = p[:, o:o + C_INNER]; o += C_INNER
        xbc_o[rows, :] = p[:, o:o + C_XBC]; o += C_XBC
        dt_o[rows, :] = p[:, o:o + DT_LANES]


def _inproj_call(xs, mod, nw, w_in_p, layer, ga, gb, hn, kpl, kone, dup, dupt, cosa, sina, cosb, sinb, *,
                 tm, ctx_rows, nbatch):
    b, t, d = xs.shape
    ntiles = t // tm
    tok = lambda w: pl.BlockSpec((None, tm, w), lambda bi, i: (bi, i, 0))
    tab = lambda w: pl.BlockSpec((tm, w), lambda bi, i: (i, 0))
    full = lambda a: pl.BlockSpec(a.shape, lambda bi, i: (0,) * a.ndim)
    widths = (A_QCOLS, A_HEADS * A_HLANES, A_HEADS * A_V, B_QCOLS, B_QCOLS, B_QCOLS, C_INNER, C_XBC, DT_LANES)
    transposed = (0, 2, 4)
    dtypes = (BF16,) * 6 + (F32,) * 3
    return pl.pallas_call(
        functools.partial(_inproj_kernel, sub=SUB_ROWS, nsub=tm // SUB_ROWS, ctx_subs=ctx_rows // SUB_ROWS,
                          nbatch=nbatch),
        grid=(b, ntiles),
        in_specs=[tok(d), full(mod), pl.BlockSpec((1, d), lambda bi, i: (0, 0)),
                  pl.BlockSpec((None,) + w_in_p.shape[1:], lambda bi, i: (layer, 0, 0), pipeline_mode=pl.Buffered(1)),
                  full(ga), full(gb), full(hn), full(kpl), full(kone), full(dup), full(dupt)] + [tab(ROPE_LANES)] * 4,
        out_specs=[pl.BlockSpec((None, w, tm), lambda bi, i: (bi, 0, i)) if k in transposed else tok(w)
                   for k, w in enumerate(widths)],
        out_shape=[jax.ShapeDtypeStruct((b, w, t) if k in transposed else (b, t, w), dt)
                   for k, (w, dt) in enumerate(zip(widths, dtypes))],
        compiler_params=_params(("parallel", "parallel")),
        name="in_proj_heads",
    )(xs, mod, nw.reshape(1, d), w_in_p, ga, gb, hn, kpl, kone, dup, dupt, cosa, sina, cosb, sinb)


def _attn_a_kernel(lamv_ref, subln_ref, qt_ref, kx_ref, vt_ref, o_ref, kmax_scr, *,
                   tq, tk, ctx, nk, q_tile0, lam_init):
    step = pl.program_id(1)

    @pl.when(step == 0)
    def _():
        r = lax.broadcasted_iota(jnp.int32, (A_HLANES, A_HLANES), 0)
        c = lax.broadcasted_iota(jnp.int32, (A_HLANES, A_HLANES), 1)
        in_comp = jnp.logical_and(r >= c * A_QK, r < (c + 1) * A_QK)
        sel = jnp.where(jnp.logical_and(in_comp, c < 2), 1.0, 0.0).astype(BF16)
        for h in range(A_HEADS):
            kk = kx_ref[:, h * A_HLANES:(h + 1) * A_HLANES].astype(F32)
            ksq = jnp.dot((kk * kk).astype(BF16), sel, preferred_element_type=F32)
            kmax_scr[h:h + 1, :] = jnp.sqrt(jnp.max(ksq, axis=0, keepdims=True))

    lv = lamv_ref[...]
    lam = (jnp.exp(jnp.sum(lv[0:1] * lv[1:2], axis=1, keepdims=True))
           - jnp.exp(jnp.sum(lv[2:3] * lv[3:4], axis=1, keepdims=True)) + lam_init)
    qt = qt_ref[...].astype(F32)
    qcs = [qt[hc * A_QK:(hc + 1) * A_QK, :] for hc in range(2 * A_HEADS)]
    shifts = [jnp.sqrt(jnp.sum(qc * qc, axis=0, keepdims=True)) * kmax_scr[hc // 2:hc // 2 + 1, hc % 2:hc % 2 + 1]
              for hc, qc in enumerate(qcs)]
    shift_max = jnp.max(functools.reduce(jnp.maximum, shifts))

    def q_ext(h, shifted):
        z = jnp.zeros((A_QK, tq), F32)
        top = jnp.concatenate([qcs[2 * h], z], axis=1)
        mid = jnp.concatenate([z, qcs[2 * h + 1]], axis=1)
        row = lax.broadcasted_iota(jnp.int32, (A_HLANES - 2 * A_QK, 2 * tq), 0)
        if shifted:
            bot = jnp.where(row == 0, -jnp.concatenate([shifts[2 * h], shifts[2 * h + 1]], axis=1), 0.0)
        else:
            bot = jnp.zeros(row.shape, F32)
        return jnp.concatenate([top, mid, bot], axis=0).astype(BF16)

    def scores(qx, h, start, size):
        return jnp.dot(kx_ref[pl.ds(start, size), h * A_HLANES:(h + 1) * A_HLANES], qx, preferred_element_type=F32)

    def v_ext(h, start, size):
        ones = jnp.ones((A_VROWS - A_V, size), BF16)
        return jnp.concatenate([vt_ref[h * A_V:(h + 1) * A_V, pl.ds(start, size)], ones], axis=0)

    def finish(accs):
        rows = []
        for acc in accs:
            o = (acc[:A_V, :tq] / acc[A_V:A_V + 1, :tq]) - lam * (acc[:A_V, tq:] / acc[A_V:A_V + 1, tq:])
            ms = jnp.mean(o * o, axis=0, keepdims=True)
            rows.append(o * lax.rsqrt(ms + EPS) * subln_ref[...] * (1.0 - lam_init))
        o_ref[...] = jnp.concatenate(rows, axis=0).T

    def attend(shifted):
        qx = [q_ext(h, shifted) for h in range(A_HEADS)]

        def head_step(h, start, carry):
            s = scores(qx[h], h, start, tk)
            ve = v_ext(h, start, tk)
            if shifted:
                return carry + jnp.dot(ve, jnp.exp2(s).astype(BF16), preferred_element_type=F32)
            m_run, acc = carry
            m_new = jnp.maximum(m_run, jnp.max(s, axis=0, keepdims=True))
            p = jnp.exp2(s - m_new).astype(BF16)
            return m_new, jnp.exp2(m_run - m_new) * acc + jnp.dot(ve, p, preferred_element_type=F32)

        def body(j, carries):
            start = pl.multiple_of(j * tk, tk)
            return tuple(head_step(h, start, carries[h]) for h in range(A_HEADS))

        acc0 = jnp.zeros((A_VROWS, 2 * tq), F32)
        if shifted:
            accs = lax.fori_loop(0, nk, body, (acc0,) * A_HEADS, unroll=True)
        else:
            init = (jnp.full((1, 2 * tq), NEG_INF, F32), acc0)
            accs = [c[1] for c in lax.fori_loop(0, nk, body, (init,) * A_HEADS)]
        finish(accs)

    def attend_ctx():
        accs = []
        for h in range(A_HEADS):
            s = scores(q_ext(h, False), h, 0, ctx)
            p = jnp.exp2(s - jnp.max(s, axis=0, keepdims=True)).astype(BF16)
            accs.append(jnp.dot(v_ext(h, 0, ctx), p, preferred_element_type=F32))
        finish(accs)

    safe = shift_max < A_SHIFT_CAP
    if q_tile0 * tq < ctx:
        is_ctx = (step + q_tile0) * tq < ctx
        pl.when(is_ctx)(attend_ctx)
        pl.when(jnp.logical_and(jnp.logical_not(is_ctx), safe))(lambda: attend(True))
        pl.when(jnp.logical_and(jnp.logical_not(is_ctx), jnp.logical_not(safe)))(lambda: attend(False))
    else:
        pl.when(safe)(lambda: attend(True))
        pl.when(jnp.logical_not(safe))(lambda: attend(False))


def _attn_a_call(lamv, subln_col, qa_t, ka_x, va_t, *, tq, tk, ctx, q_tile0, lam_init):
    b, w, t = qa_t.shape
    nq = t // tq - q_tile0
    return pl.pallas_call(
        functools.partial(_attn_a_kernel, tq=tq, tk=tk, ctx=ctx, nk=t // tk, q_tile0=q_tile0, lam_init=lam_init),
        grid=(b, nq),
        in_specs=[pl.BlockSpec(lamv.shape, lambda bi, qi: (0, 0)),
                  pl.BlockSpec(subln_col.shape, lambda bi, qi: (0, 0)),
                  pl.BlockSpec((None, w, tq), lambda bi, qi: (bi, 0, qi + q_tile0)),
                  pl.BlockSpec((None, t, A_HEADS * A_HLANES), lambda bi, qi: (bi, 0, 0)),
                  pl.BlockSpec((None, w, t), lambda bi, qi: (bi, 0, 0))],
        out_specs=pl.BlockSpec((None, tq, w), lambda bi, qi: (bi, qi, 0)),
        out_shape=jax.ShapeDtypeStruct((b, nq * tq, w), F32),
        scratch_shapes=[pltpu.VMEM((8, A_HLANES), F32)],
        compiler_params=_params(("parallel", "arbitrary")),
        name="diff_attention",
    )(lamv, subln_col, qa_t, ka_x, va_t)


def _attn_b_kernel(sink_ref, band_ref, q_ref, kp_ref, k0_ref, k1_ref, kn_ref, kx_ref,
                   vp_ref, v0_ref, v1_ref, vn_ref, vx_ref, o_ref, *, blk0, ctx_blocks, nblk):
    n0 = pl.program_id(1) * B_QBLOCKS + blk0
    nq = B_QBLOCKS * B_BLOCK
    k_all = jnp.concatenate([kp_ref[...], k0_ref[...], k1_ref[...], kn_ref[...], kx_ref[...]], axis=1)
    v_all = jnp.concatenate([vp_ref[...], v0_ref[...], v1_ref[...], vn_ref[...], vx_ref[...]], axis=0)
    lane = lax.broadcasted_iota(jnp.int32, (1, k_all.shape[1]), 1)
    lat = n0 >= ctx_blocks
    pen = jnp.zeros(lane.shape, F32)
    for d in range(B_QBLOCKS + 2):
        kb = n0 - 1 + d
        ok = jnp.logical_and(lat, jnp.logical_and(kb >= ctx_blocks, kb <= nblk - 1))
        in_blk = jnp.logical_and(lane >= d * B_BLOCK, lane < (d + 1) * B_BLOCK)
        pen = jnp.where(in_blk, jnp.where(ok, 0.0, NEG_INF), pen)
    bias = band_ref[...] + pen
    hpg = B_HEADS // B_KV_HEADS
    gl = hpg * B_DIM
    qlane = lax.broadcasted_iota(jnp.int32, (nq, gl), 1) // B_DIM
    outs = []
    for g in range(B_KV_HEADS):
        q_g = q_ref[:, g * gl:(g + 1) * gl]
        o_g = jnp.zeros((nq, gl), F32)
        for e in range(hpg):
            q_e = jnp.where(qlane == e, q_g, jnp.zeros_like(q_g))
            s = jnp.dot(q_e, k_all[g * gl:(g + 1) * gl, :], preferred_element_type=F32) + bias
            sk = sink_ref[g * hpg + e]
            m = jnp.maximum(jnp.max(s, axis=1, keepdims=True), sk)
            p = jnp.exp(s - m)
            den = jnp.sum(p, axis=1, keepdims=True) + jnp.exp(sk - m)
            pv = jnp.dot(p.astype(BF16), v_all[:, g * gl:(g + 1) * gl], preferred_element_type=F32)
            o_g = jnp.where(qlane == e, pv / den, o_g)
        outs.append(o_g)
    o_ref[...] = jnp.concatenate(outs, axis=1)


def _attn_b_call(sink, band, qb, kb_t, vb2, *, ctx, blk0):
    b, t, w = qb.shape
    nblk = t // B_BLOCK
    ctx_blocks = ctx // B_BLOCK
    nq = B_QBLOCKS * B_BLOCK
    first = lambda i: i * B_QBLOCKS + blk0
    clamp = lambda n: jnp.clip(n, 0, nblk - 1)
    kspec = lambda d: pl.BlockSpec((None, w, B_BLOCK), lambda bi, i: (bi, 0, clamp(first(i) - 1 + d)))
    vspec = lambda d: pl.BlockSpec((None, B_BLOCK, w), lambda bi, i: (bi, clamp(first(i) - 1 + d), 0))
    nwb = B_QBLOCKS + 2
    return pl.pallas_call(
        functools.partial(_attn_b_kernel, blk0=blk0, ctx_blocks=ctx_blocks, nblk=nblk),
        grid=(b, (nblk - blk0) // B_QBLOCKS),
        in_specs=[pl.BlockSpec(memory_space=pltpu.SMEM),
                  pl.BlockSpec(band.shape, lambda bi, i: (0, 0)),
                  pl.BlockSpec((None, nq, w), lambda bi, i: (bi, i + blk0 // B_QBLOCKS, 0))]
                 + [kspec(d) for d in range(nwb)] + [pl.BlockSpec((None, w, ctx), lambda bi, i: (bi, 0, 0))]
                 + [vspec(d) for d in range(nwb)] + [pl.BlockSpec((None, ctx, w), lambda bi, i: (bi, 0, 0))],
        out_specs=pl.BlockSpec((None, nq, w), lambda bi, i: (bi, i, 0)),
        out_shape=jax.ShapeDtypeStruct((b, (nblk - blk0) * B_BLOCK, w), F32),
        compiler_params=_params(("parallel", "parallel")),
        name="window_attention",
    )(sink, band, qb, *([kb_t] * (nwb + 1)), *([vb2] * (nwb + 1)))


def _conv_kernel(u_ref, up_ref, un_ref, w_ref, b_ref, xt_o, bn_o, *, tm, ctx_tiles, ntiles):
    i = pl.program_id(1)
    u = u_ref[...]
    has_prev = jnp.logical_and(i != 0, i != ctx_tiles)
    has_next = jnp.logical_and(i != ctx_tiles - 1, i != ntiles - 1)
    up = jnp.where(has_prev, up_ref[...], 0.0)
    un = jnp.where(has_next, un_ref[...], 0.0)
    full = jnp.concatenate([up, u, un], axis=0)
    w = w_ref[...]
    acc = b_ref[...] + w[0:1] * full[HALO - 2:HALO - 2 + tm]
    for k in range(1, C_CONV):
        acc = acc + w[k:k + 1] * full[HALO - 2 + k:HALO - 2 + k + tm]
    y = _silu(acc)
    xt_o[...] = y.T
    bn_o[...] = y[:, C_INNER:C_INNER + C_GN]


def _conv_call(xbc_raw, conv_w, conv_b, *, tm, ctx):
    b, t, ch = xbc_raw.shape
    ntiles = t // tm
    hb = tm // HALO
    nh = t // HALO
    w8 = jnp.zeros((8, ch), F32).at[:C_CONV].set(conv_w)
    return pl.pallas_call(
        functools.partial(_conv_kernel, tm=tm, ctx_tiles=ctx // tm, ntiles=ntiles),
        grid=(b, ntiles),
        in_specs=[pl.BlockSpec((None, tm, ch), lambda bi, i: (bi, i, 0)),
                  pl.BlockSpec((None, HALO, ch), lambda bi, i: (bi, jnp.maximum(i * hb - 1, 0), 0)),
                  pl.BlockSpec((None, HALO, ch), lambda bi, i: (bi, jnp.minimum((i + 1) * hb, nh - 1), 0)),
                  pl.BlockSpec((8, ch), lambda bi, i: (0, 0)),
                  pl.BlockSpec((1, ch), lambda bi, i: (0, 0))],
        out_specs=[pl.BlockSpec((None, ch, tm), lambda bi, i: (bi, 0, i)),
                   pl.BlockSpec((None, tm, C_GN), lambda bi, i: (bi, i, 0))],
        out_shape=[jax.ShapeDtypeStruct((b, ch, t), F32), jax.ShapeDtypeStruct((b, t, C_GN), F32)],
        compiler_params=_params(("parallel", "parallel")),
        name="ssm_conv",
    )(xbc_raw, xbc_raw, xbc_raw, w8, conv_b.reshape(1, ch))


def _softplus(v):
    return jnp.maximum(v, 0.0) + jnp.log1p(jnp.exp(-jnp.abs(v)))


def _split3(v):
    v1 = v.astype(BF16)
    r1 = v - v1.astype(F32)
    v2 = r1.astype(BF16)
    v3 = (r1 - v2.astype(F32)).astype(BF16)
    return v1, v2, v3


def _ssd_prepare(xt_all, bn, dt_raw, par, *, lane0, backward):
    c_t = xt_all[C_INNER + C_GN:]
    dt = _softplus(dt_raw + par[0:1])
    a = dt * par[1:2]
    s_idx = lax.broadcasted_iota(jnp.int32, (C_CHUNK, C_CHUNK), 0)
    l_idx = lax.broadcasted_iota(jnp.int32, (C_CHUNK, C_CHUNK), 1)
    tri = jnp.where(l_idx <= s_idx, 1.0, 0.0).astype(BF16)
    cs = sum(jnp.dot(tri, part, preferred_element_type=F32) for part in _split3(a))
    key = cs - a if backward else cs
    k1, k2, k3 = _split3(key)
    parts = (k1.astype(F32) + pltpu.roll(k2.astype(F32), 2 * C_HEADS, 1)
             + pltpu.roll(k3.astype(F32), 4 * C_HEADS, 1)).astype(BF16)
    src = lax.broadcasted_iota(jnp.int32, (DT_LANES, C_HEADS * C_CHUNK), 0)
    dst = lax.broadcasted_iota(jnp.int32, (DT_LANES, C_HEADS * C_CHUNK), 1)
    pick = jnp.logical_and(src < 6 * C_HEADS, (src & (2 * C_HEADS - 1)) == lane0 + dst // C_CHUNK)
    colb_all = jnp.dot(parts, jnp.where(pick, 1.0, 0.0).astype(BF16), preferred_element_type=F32)
    return dict(x_t=xt_all[:C_INNER], c_t=c_t, c16=c_t.astype(BF16), bn=bn, dt_t=dt.T, key_t=key.T, cs=cs,
                colb_all=colb_all, mask=(s_idx >= l_idx) if backward else (s_idx <= l_idx))


def _ssd_group(st, g, hs, *, lane0, backward):
    lane_g = lax.broadcasted_iota(jnp.int32, st["bn"].shape, 1) // C_STATE
    row_g = lax.broadcasted_iota(jnp.int32, st["c_t"].shape, 0) // C_STATE
    bn_g = jnp.where(lane_g == g, st["bn"], 0.0).astype(BF16)
    ct_g = jnp.where(row_g == g, st["c_t"], 0.0).astype(BF16)
    cb_t = jnp.dot(bn_g, st["c16"], preferred_element_type=F32)
    y_off = jnp.dot(hs.astype(BF16), ct_g, preferred_element_type=F32)
    y_g, xw_g, hs_new = [], [], []
    for r in range(C_HPG):
        h = g * C_HPG + r
        hl = lane0 + h
        colb = st["colb_all"][:, h * C_CHUNK:(h + 1) * C_CHUNK]
        row = st["key_t"][hl:hl + 1, :]
        tot_h = st["cs"][C_CHUNK - 1:C_CHUNK, hl:hl + 1]
        d = (colb - row) if backward else (row - colb)
        m_t = (cb_t * jnp.exp(jnp.where(st["mask"], d, NEG_INF))).astype(BF16)
        xdt = st["x_t"][h * C_HEAD_DIM:(h + 1) * C_HEAD_DIM] * st["dt_t"][hl:hl + 1, :]
        y_diag = jnp.dot(xdt.astype(BF16), m_t, preferred_element_type=F32)
        carry = jnp.exp(tot_h - row) if backward else jnp.exp(row)
        y_g.append(y_diag + y_off[r * C_HEAD_DIM:(r + 1) * C_HEAD_DIM] * carry)
        w_row = jnp.exp(row) if backward else jnp.exp(tot_h - row)
        xw_g.append((xdt * w_row).astype(BF16))
        hs_new.append(jnp.exp(tot_h) * hs[r * C_HEAD_DIM:(r + 1) * C_HEAD_DIM])
    state = jnp.concatenate(hs_new, axis=0) + jnp.dot(jnp.concatenate(xw_g, axis=0), bn_g, preferred_element_type=F32)
    return y_g, state


def _ssd_kernel(par_ref, dsk_ref, xf_ref, bf_ref, dtf_ref, xb_ref, bb_ref, dtb_ref, yf_ref, yb_ref,
                hf_scr, hb_scr, dsk_scr):
    @pl.when(pl.program_id(1) == 0)
    def _():
        hf_scr[...] = jnp.zeros(hf_scr.shape, F32)
        hb_scr[...] = jnp.zeros(hb_scr.shape, F32)
        dsk_scr[...] = jnp.broadcast_to(dsk_ref[...], dsk_scr.shape)

    par = par_ref[...]
    hf = [hf_scr[g] for g in range(C_GROUPS)]
    hb = [hb_scr[g] for g in range(C_GROUPS)]
    for k in range(C_STEP_CHUNKS):
        cf = slice(k * C_CHUNK, (k + 1) * C_CHUNK)
        cb = slice((C_STEP_CHUNKS - 1 - k) * C_CHUNK, (C_STEP_CHUNKS - k) * C_CHUNK)
        sf = _ssd_prepare(xf_ref[:, cf], bf_ref[cf, :], dtf_ref[cf, :], par, lane0=0, backward=False)
        sb = _ssd_prepare(xb_ref[:, cb], bb_ref[cb, :], dtb_ref[cb, :], par, lane0=C_HEADS, backward=True)
        rows_f, rows_b = [], []
        for g in range(C_GROUPS):
            y_g, hf[g] = _ssd_group(sf, g, hf[g], lane0=0, backward=False)
            rows_f += y_g
            y_g, hb[g] = _ssd_group(sb, g, hb[g], lane0=C_HEADS, backward=True)
            rows_b += y_g
        yf_ref[cf, :] = (jnp.concatenate(rows_f, axis=0) + dsk_scr[...] * sf["x_t"]).T
        yb_ref[cb, :] = jnp.concatenate(rows_b, axis=0).T
    for g in range(C_GROUPS):
        hf_scr[g] = hf[g]
        hb_scr[g] = hb[g]


def _ssd_call(par, dsk_col, xbc_t, bn, dt, *, ctx):
    b, ch, t = xbc_t.shape
    blk = C_STEP_CHUNKS * C_CHUNK
    nblk = t // blk
    ncb = ctx // blk
    bblock = lambda j: jnp.where(j < ncb, ncb - 1 - j, nblk - 1 + ncb - j)
    fwd = lambda bi, j: (bi, j, 0)
    bwd = lambda bi, j: (bi, bblock(j), 0)
    fwd_t = lambda bi, j: (bi, 0, j)
    bwd_t = lambda bi, j: (bi, 0, bblock(j))
    tspec = lambda im: pl.BlockSpec((None, ch, blk), im)
    nspec = lambda w, im: pl.BlockSpec((None, blk, w), im)
    return pl.pallas_call(
        _ssd_kernel,
        grid=(b, nblk),
        in_specs=[pl.BlockSpec((8, DT_LANES), lambda bi, j: (0, 0)),
                  pl.BlockSpec(dsk_col.shape, lambda bi, j: (0, 0)),
                  tspec(fwd_t), nspec(C_GN, fwd), nspec(DT_LANES, fwd),
                  tspec(bwd_t), nspec(C_GN, bwd), nspec(DT_LANES, bwd)],
        out_specs=[nspec(C_INNER, fwd), nspec(C_INNER, bwd)],
        out_shape=[jax.ShapeDtypeStruct((b, t, C_INNER), F32)] * 2,
        scratch_shapes=[pltpu.VMEM((C_GROUPS, C_HPG * C_HEAD_DIM, C_GN), F32)] * 2
                       + [pltpu.VMEM((C_INNER, C_CHUNK), F32)],
        compiler_params=_params(("parallel", "arbitrary")),
        name="ssd_scan",
    )(par, dsk_col, xbc_t, bn, dt, xbc_t, bn, dt)


def _outproj_kernel(x_ref, mod_ref, oa_ref, ob_ref, yf_ref, yb_ref, z_ref, gn_ref, w_ref, o_ref, *,
                    sub, nsub, sub0, ctx_subs, nbatch):
    m_lat, m_ctx = mod_ref[pl.program_id(0)], mod_ref[nbatch]
    gn = gn_ref[...]
    gw = C_INNER // C_GROUPS
    for k in range(nsub):
        rows = slice(k * sub, (k + 1) * sub)
        m = jnp.where(pl.program_id(1) * nsub + k + sub0 < ctx_subs, m_ctx, m_lat)
        y = (yf_ref[rows, :] + yb_ref[rows, :]) * _silu(z_ref[rows, :])
        oc = []
        for g in range(C_GROUPS):
            yg = y[:, g * gw:(g + 1) * gw]
            ms = jnp.mean(yg * yg, axis=1, keepdims=True)
            oc.append(yg * lax.rsqrt(ms + EPS) * gn[:, g * gw:(g + 1) * gw])
        mix = jnp.concatenate([oa_ref[rows, :], ob_ref[rows, :]] + oc, axis=1).astype(BF16)
        o_ref[rows, :] = x_ref[rows, :] + m[5:6] * jnp.dot(mix, w_ref[...], preferred_element_type=F32)


def _outproj_call(xs, mod, oa, ob, yf, yb, z, gn, w_out, layer, *, tm, ctx_rows, row0, nbatch):
    b, t, d = xs.shape
    assert row0 % tm == 0
    ntiles = (t - row0) // tm
    tile0 = row0 // tm
    tok = lambda w: pl.BlockSpec((None, tm, w), lambda bi, i: (bi, i + tile0, 0))
    tok0 = lambda w: pl.BlockSpec((None, tm, w), lambda bi, i: (bi, i, 0))
    return pl.pallas_call(
        functools.partial(_outproj_kernel, sub=SUB_ROWS, nsub=tm // SUB_ROWS, sub0=row0 // SUB_ROWS,
                          ctx_subs=ctx_rows // SUB_ROWS, nbatch=nbatch),
        grid=(b, ntiles),
        in_specs=[tok(d), pl.BlockSpec(mod.shape, lambda bi, i: (0, 0, 0)),
                  tok0(A_HEADS * A_V), tok0(B_QCOLS), tok(C_INNER), tok(C_INNER), tok(C_INNER),
                  pl.BlockSpec(gn.shape, lambda bi, i: (0, 0)),
                  pl.BlockSpec((None,) + w_out.shape[1:], lambda bi, i: (layer, 0, 0),
                               pipeline_mode=pl.Buffered(1))],
        out_specs=tok0(d),
        out_shape=jax.ShapeDtypeStruct((b, ntiles * tm, d), F32),
        compiler_params=_params(("parallel", "parallel")),
        name="mix_out_proj",
    )(xs, mod, oa, ob, yf, yb, z, gn, w_out)


def _rope_tables(seq, ctx, dim, reps):
    rows = seq // GRID_W
    row = jnp.repeat(jnp.arange(rows, dtype=F32), GRID_W)
    col = jnp.tile(jnp.arange(GRID_W, dtype=F32), rows)
    quarter = dim // 4
    inv_freq = ROPE_BASE ** (-jnp.arange(quarter, dtype=F32) / quarter)
    ar = row[:, None] * inv_freq
    ac = col[:, None] * inv_freq
    ang = jnp.concatenate([ar, ar, ac, ac], axis=-1)
    sign = jnp.where((jnp.arange(dim) % (dim // 2)) < quarter, -1.0, 1.0).astype(F32)
    cos = jnp.concatenate([jnp.ones((ctx, dim), F32), jnp.cos(ang)], axis=0)
    sin = jnp.concatenate([jnp.zeros((ctx, dim), F32), jnp.sin(ang) * sign], axis=0)
    return jnp.tile(cos, (1, reps)), jnp.tile(sin, (1, reps))


def _group_matrix(n, group):
    idx = jnp.arange(n) // group
    return (idx[:, None] == idx[None, :]).astype(BF16)


def kernel(x, c, ctx, c_ctx, w_mod, b_mod, norm_ffn1, ffn1_w13, ffn1_w2, norm_mix, w_in, w_out, qn_a, kn_a, lam_q1, lam_k1, lam_q2, lam_k2, subln_a, qn_b, kn_b, sink_b, conv_w, conv_b, dt_bias, a_log, d_skip, gnorm_c, norm_ffn2, ffn2_w13, ffn2_w2):
    nb, seq, d = x.shape
    nctx = ctx.shape[1]
    t = nctx + seq
    depth = w_mod.shape[0]
    tm = 256
    big = lambda rows: next(c for c in (768, 512, 256) if rows % c == 0)
    tm_all, tm_lat = big(t), big(seq)
    fc = 256
    tq, tk = 256, (768 if t % 768 == 0 and t > 768 else 256)
    assert nctx == tm == SUB_ROWS and nctx % (C_STEP_CHUNKS * C_CHUNK) == 0 and seq % GRID_W == 0 and nb < 8

    xs = x
    cvec = jnp.zeros((8, d), F32).at[:nb].set(c).at[nb].set(c_ctx)
    mod = _mod_call(cvec, w_mod, b_mod).reshape(depth, 8, N_MOD, d)

    cosa, sina = _rope_tables(seq, nctx, A_QK, ROPE_LANES // A_QK)
    cosb, sinb = _rope_tables(seq, nctx, B_DIM, ROPE_LANES // B_DIM)
    ga = _group_matrix(A_QCOLS, A_QK)
    gb = _group_matrix(B_QCOLS, B_DIM)
    src = jnp.arange(A_QCOLS)
    dst = (src // (2 * A_QK)) * A_HLANES + src % (2 * A_QK)
    kpl = (dst[:, None] == jnp.arange(A_HEADS * A_HLANES)[None, :]).astype(BF16)
    kone = (jnp.arange(A_HEADS * A_HLANES) % A_HLANES == 2 * A_QK).astype(F32).reshape(1, -1)
    dsrc = jnp.arange(B_KCOLS)
    dcol = jnp.arange(B_QCOLS)
    hpg = B_HEADS // B_KV_HEADS
    dup = jnp.logical_and(dcol[None, :] // (hpg * B_DIM) == dsrc[:, None] // B_DIM,
                          dcol[None, :] % B_DIM == dsrc[:, None] % B_DIM).astype(BF16)
    qi = jnp.arange(B_QBLOCKS * B_BLOCK)[:, None]
    kj = jnp.arange((B_QBLOCKS + 2) * B_BLOCK + nctx)[None, :]
    in_band = jnp.logical_and(kj - qi >= 0, kj - qi <= 2 * B_BLOCK)
    band = jnp.where(jnp.logical_or(in_band, kj >= (B_QBLOCKS + 2) * B_BLOCK), 0.0, NEG_INF).astype(F32)
    in_cols = w_in.shape[2]
    in_pad = (-in_cols) % DT_LANES
    w13_1, w2_1 = ffn1_w13.astype(BF16), ffn1_w2.astype(BF16)
    w13_2, w2_2 = ffn2_w13.astype(BF16), ffn2_w2.astype(BF16)
    w_in_p = jnp.pad(w_in, ((0, 0), (0, 0), (0, in_pad))).astype(BF16)
    w_out_b = w_out.astype(BF16)

    for l in range(depth):
        last = l == depth - 1
        lam_init = 0.8 - 0.6 * math.exp(-0.3 * l)
        hn = jnp.stack([jnp.tile(qn_a[l], A_QCOLS // A_QK), jnp.tile(kn_a[l], A_QCOLS // A_QK),
                        jnp.tile(qn_b[l], B_QCOLS // B_DIM), jnp.tile(kn_b[l], B_QCOLS // B_DIM)]).astype(F32)
        hn = jnp.zeros((8, A_QCOLS), F32).at[:4].set(hn)

        xs = _ffn_call(xs, mod[l], norm_ffn1[l], w13_1, w2_1, l, si=0, fc=fc,
                       tm=(SUB_ROWS if l == 0 else tm_all), ctx_rows=nctx, nbatch=nb, ctx_in=(ctx if l == 0 else None))

        qa_t, ka_x, va_t, qb, kb_t, vb2, z, xbc_raw, dt = _inproj_call(
            xs, mod[l], norm_mix[l], w_in_p, l, ga, gb, hn, kpl, kone, dup, dup.T, cosa, sina, cosb, sinb,
            tm=tm_all, ctx_rows=nctx, nbatch=nb)

        lamv = jnp.zeros((8, A_QK), F32).at[:4].set(jnp.stack([lam_q1[l], lam_k1[l], lam_q2[l], lam_k2[l]]))
        oa = _attn_a_call(lamv, subln_a[l].reshape(A_V, 1), qa_t, ka_x, va_t, tq=tq, tk=tk, ctx=nctx,
                          q_tile0=(nctx // tq if last else 0), lam_init=lam_init)

        ob = _attn_b_call(sink_b[l].astype(F32), band, qb, kb_t, vb2, ctx=nctx, blk0=(nctx // B_BLOCK if last else 0))

        xbc_t, bn = _conv_call(xbc_raw, conv_w[l], conv_b[l], tm=tm, ctx=nctx)
        par = (jnp.zeros((8, DT_LANES), F32)
               .at[0, :2 * C_HEADS].set(dt_bias[l].reshape(-1))
               .at[1, :2 * C_HEADS].set(-jnp.exp(a_log[l].astype(F32)).reshape(-1)))
        dsk_col = jnp.repeat(d_skip[l].astype(F32), C_HEAD_DIM).reshape(C_INNER, 1)
        yf, yb = _ssd_call(par, dsk_col, xbc_t, bn, dt, ctx=nctx)

        xs = _outproj_call(xs, mod[l], oa, ob, yf, yb, z, gnorm_c[l].reshape(1, C_INNER), w_out_b, l,
                           tm=(SUB_ROWS if last else tm_all), ctx_rows=nctx, row0=(nctx if last else 0), nbatch=nb)
        xs = _ffn_call(xs, mod[l], norm_ffn2[l], w13_2, w2_2, l, si=6, fc=fc,
                       tm=(tm_lat if last else tm_all), ctx_rows=(0 if last else nctx), nbatch=nb)
    return xs
```

```python
import functools
import math

import jax
import jax.numpy as jnp
from jax import lax
from jax.experimental import pallas as pl
from jax.experimental.pallas import tpu as pltpu

F32 = jnp.float32
BF16 = jnp.bfloat16
NEG_INF = float("-inf")

EPS = 1e-6
ROPE_BASE = 10000.0
GRID_W = 64
ROPE_LANES = 128
N_MOD = 9

A_HEADS, A_QK, A_V = 4, 32, 64
A_QCOLS = A_HEADS * 2 * A_QK
A_SCALE = A_QK ** -0.5
LOG2E = math.log2(math.e)
A_HLANES = 128
A_VROWS = A_V + 16
A_SHIFT_CAP = 48.0
B_HEADS, B_KV_HEADS, B_DIM = 4, 2, 64
B_QCOLS = B_HEADS * B_DIM
B_KCOLS = B_KV_HEADS * B_DIM
B_SCALE = B_DIM ** -0.5
B_BLOCK = 128
B_QBLOCKS = 2
C_HEADS, C_HEAD_DIM, C_GROUPS, C_STATE, C_CONV = 8, 64, 2, 64, 5
C_INNER = C_HEADS * C_HEAD_DIM
C_GN = C_GROUPS * C_STATE
C_XBC = C_INNER + 2 * C_GN
C_CHUNK = 128
C_STEP_CHUNKS = 2
C_HPG = C_HEADS // C_GROUPS
DT_LANES = 128
HALO = 8

SUB_ROWS = 256

V7X_VMEM_LIMIT = 56 * 1024 * 1024


def _params(sem, vmem=V7X_VMEM_LIMIT):
    return pltpu.CompilerParams(dimension_semantics=sem, vmem_limit_bytes=vmem)


def _rms_mod(x, nw, shift, scale):
    ms = jnp.mean(x * x, axis=-1, keepdims=True)
    y = x * lax.rsqrt(ms + EPS) * nw
    return y * (1.0 + scale) + shift


def _silu(v):
    return v * jax.nn.sigmoid(v)


def _mod_kernel(c_ref, w_ref, b_ref, o_ref):
    sc = _silu(c_ref[...])
    c1 = sc.astype(BF16)
    c2 = (sc - c1.astype(F32)).astype(BF16)
    w = w_ref[...]
    w1 = w.astype(BF16)
    w2 = (w - w1.astype(F32)).astype(BF16)
    dot = lambda a, bb: jnp.dot(a, bb, preferred_element_type=F32)
    o_ref[...] = dot(c1, w1) + (dot(c1, w2) + dot(c2, w1)) + b_ref[...]


def _mod_call(cvec, w_mod, b_mod, tn=1152):
    nl, d, n = w_mod.shape
    return pl.pallas_call(
        _mod_kernel,
        grid=(nl, n // tn),
        in_specs=[pl.BlockSpec((8, d), lambda l, j: (0, 0)),
                  pl.BlockSpec((None, d, tn), lambda l, j: (l, 0, j)),
                  pl.BlockSpec((None, 1, tn), lambda l, j: (l, 0, j))],
        out_specs=pl.BlockSpec((None, 8, tn), lambda l, j: (l, 0, j)),
        out_shape=jax.ShapeDtypeStruct((nl, 8, n), F32),
        compiler_params=_params(("parallel", "parallel")),
        name="adaln_mod",
    )(cvec, w_mod, b_mod.reshape(nl, 1, n))


def _ffn_kernel(*refs, si, fc, ff, sub, nsub, ctx_subs, nbatch, split_input):
    if split_input:
        c_ref, x_ref, mod_ref, nw_ref, w13_ref, w2_ref, o_ref, a_scr = refs
    else:
        x_ref, mod_ref, nw_ref, w13_ref, w2_ref, o_ref, a_scr = refs
    m_lat, m_ctx = mod_ref[pl.program_id(0)], mod_ref[nbatch]
    for k in range(nsub):
        rows = slice(k * sub, (k + 1) * sub)
        is_ctx = pl.program_id(1) * nsub + k < ctx_subs
        x = jnp.where(is_ctx, c_ref[...], x_ref[...]) if split_input else x_ref[rows, :]
        m = jnp.where(is_ctx, m_ctx, m_lat)
        h = _rms_mod(x, nw_ref[...], m[si:si + 1], m[si + 1:si + 2]).astype(BF16)
        for c in range(ff // fc):
            g = jnp.dot(h, w13_ref[:, c * fc:(c + 1) * fc], preferred_element_type=F32)
            u = jnp.dot(h, w13_ref[:, ff + c * fc:ff + (c + 1) * fc], preferred_element_type=F32)
            a_scr[k, :, c * fc:(c + 1) * fc] = (_silu(g) * u).astype(BF16)
        y = jnp.dot(a_scr[k], w2_ref[...], preferred_element_type=F32)
        o_ref[rows, :] = x + (0.5 * m[si + 2:si + 3]) * y


def _ffn_call(xs, mod, nw, w13, w2, layer, *, si, tm, fc, ctx_rows, nbatch, ctx_in=None):
    b, t, d = xs.shape
    ff = w2.shape[1]
    split = ctx_in is not None
    nsub = tm // SUB_ROWS
    ctx_subs = ctx_rows // SUB_ROWS
    assert not split or nsub == 1
    ntiles = t // tm + (ctx_subs if split else 0)
    if split:
        tok_specs = [pl.BlockSpec((None, tm, d), lambda bi, i: (bi, jnp.minimum(i, ctx_subs - 1), 0)),
                     pl.BlockSpec((None, tm, d), lambda bi, i: (bi, jnp.maximum(i - ctx_subs, 0), 0))]
        toks = (ctx_in, xs)
    else:
        tok_specs = [pl.BlockSpec((None, tm, d), lambda bi, i: (bi, i, 0))]
        toks = (xs,)
    return pl.pallas_call(
        functools.partial(_ffn_kernel, si=si, fc=fc, ff=ff, sub=SUB_ROWS, nsub=nsub, ctx_subs=ctx_subs,
                          nbatch=nbatch, split_input=split),
        grid=(b, ntiles),
        in_specs=tok_specs + [pl.BlockSpec(mod.shape, lambda bi, i: (0, 0, 0)),
                              pl.BlockSpec((1, d), lambda bi, i: (0, 0)),
                              pl.BlockSpec((None,) + w13.shape[1:], lambda bi, i: (layer, 0, 0),
                                           pipeline_mode=pl.Buffered(1)),
                              pl.BlockSpec((None,) + w2.shape[1:], lambda bi, i: (layer, 0, 0),
                                           pipeline_mode=pl.Buffered(1))],
        out_specs=pl.BlockSpec((None, tm, d), lambda bi, i: (bi, i, 0)),
        out_shape=jax.ShapeDtypeStruct((b, ntiles * tm, d), F32),
        scratch_shapes=[pltpu.VMEM((nsub, SUB_ROWS, ff), BF16)],
        compiler_params=_params(("parallel", "parallel")),
        name="swiglu_half",
    )(*toks, mod, nw.reshape(1, d), w13, w2)


def _rope(v, cos, sin_signed, quarter):
    n = v.shape[-1]
    lane = lax.broadcasted_iota(jnp.int32, v.shape, 1)
    first = (lane & (2 * quarter - 1)) < quarter
    vr = jnp.where(first, pltpu.roll(v, n - quarter, 1), pltpu.roll(v, quarter, 1))
    return v * cos + vr * sin_signed


def _group_norm(v, gmat, inv_n, w):
    sq = v * v
    hi = sq.astype(BF16)
    lo = (sq - hi.astype(F32)).astype(BF16)
    ms = (jnp.dot(hi, gmat, preferred_element_type=F32) + jnp.dot(lo, gmat, preferred_element_type=F32)) * inv_n
    return v * lax.rsqrt(ms + EPS) * w


def _inproj_kernel(x_ref, mod_ref, nw_ref, w_ref, ga_ref, gb_ref, hn_ref, kpl_ref, kone_ref, dup_ref, dupt_ref,
                   cosa_ref, sina_ref, cosb_ref, sinb_ref, qa_o, ka_o, va_o, qb_o, kb_o, vb_o, z_o, xbc_o, dt_o, *,
                   sub, nsub, ctx_subs, nbatch):
    m_lat, m_ctx = mod_ref[pl.program_id(0)], mod_ref[nbatch]
    hn = hn_ref[...]
    ga, gb = ga_ref[...], gb_ref[...]
    twice = lambda a: jnp.concatenate([a, a], axis=1)
    for k in range(nsub):
        rows = slice(k * sub, (k + 1) * sub)
        m = jnp.where(pl.program_id(1) * nsub + k < ctx_subs, m_ctx, m_lat)
        h = _rms_mod(x_ref[rows, :], nw_ref[...], m[3:4], m[4:5]).astype(BF16)
        p = jnp.dot(h, w_ref[...], preferred_element_type=F32)
        cosa, sina = twice(cosa_ref[rows, :]), twice(sina_ref[rows, :])
        cosb, sinb = twice(cosb_ref[rows, :]), twice(sinb_ref[rows, :])
        o = 0
        qa = _group_norm(p[:, o:o + A_QCOLS], ga, 1.0 / A_QK, hn[0:1]); o += A_QCOLS
        ka = _group_norm(p[:, o:o + A_QCOLS], ga, 1.0 / A_QK, hn[1:2]); o += A_QCOLS
        qa_o[:, rows] = (_rope(qa, cosa, sina, A_QK // 4) * (A_SCALE * LOG2E)).T.astype(BF16)
        ka16 = _rope(ka, cosa, sina, A_QK // 4).astype(BF16)
        ka_o[rows, :] = (jnp.dot(ka16, kpl_ref[...], preferred_element_type=F32) + kone_ref[...]).astype(BF16)
        va_o[:, rows] = p[:, o:o + A_HEADS * A_V].T.astype(BF16); o += A_HEADS * A_V
        qb = _group_norm(p[:, o:o + B_QCOLS], gb, 1.0 / B_DIM, hn[2:3]); o += B_QCOLS
        kb = _group_norm(p[:, o:o + B_KCOLS], gb[:B_KCOLS, :B_KCOLS], 1.0 / B_DIM, hn[3:4, :B_KCOLS]); o += B_KCOLS
        qb_o[rows, :] = (_rope(qb, cosb, sinb, B_DIM // 4) * B_SCALE).astype(BF16)
        kb16_t = _rope(kb, cosb[:, :B_KCOLS], sinb[:, :B_KCOLS], B_DIM // 4).T.astype(BF16)
        kb_o[:, rows] = jnp.dot(dupt_ref[...], kb16_t, preferred_element_type=F32).astype(BF16)
        vb_o[rows, :] = jnp.dot(p[:, o:o + B_KCOLS].astype(BF16), dup_ref[...],
                                preferred_element_type=F32).astype(BF16); o += B_KCOLS
        z_o[rows, :] = p[:, o:o + C_INNER].astype(BF16); o += C_INNER
        xbc_o[rows, :] = p[:, o:o + C_XBC]; o += C_XBC
        dt_o[rows, :] = p[:, o:o + DT_LANES]


def _inproj_call(xs, mod, nw, w_in_p, layer, ga, gb, hn, kpl, kone, dup, dupt, cosa, sina, cosb, sinb, *,
                 tm, ctx_rows, nbatch):
    b, t, d = xs.shape
    ntiles = t // tm
    tok = lambda w: pl.BlockSpec((None, tm, w), lambda bi, i: (bi, i, 0))
    tab = lambda w: pl.BlockSpec((tm, w), lambda bi, i: (i, 0))
    full = lambda a: pl.BlockSpec(a.shape, lambda bi, i: (0,) * a.ndim)
    widths = (A_QCOLS, A_HEADS * A_HLANES, A_HEADS * A_V, B_QCOLS, B_QCOLS, B_QCOLS, C_INNER, C_XBC, DT_LANES)
    transposed = (0, 2, 4)
    dtypes = (BF16,) * 7 + (F32,) * 2
    return pl.pallas_call(
        functools.partial(_inproj_kernel, sub=SUB_ROWS, nsub=tm // SUB_ROWS, ctx_subs=ctx_rows // SUB_ROWS,
                          nbatch=nbatch),
        grid=(b, ntiles),
        in_specs=[tok(d), full(mod), pl.BlockSpec((1, d), lambda bi, i: (0, 0)),
                  pl.BlockSpec((None,) + w_in_p.shape[1:], lambda bi, i: (layer, 0, 0), pipeline_mode=pl.Buffered(1)),
                  full(ga), full(gb), full(hn), full(kpl), full(kone), full(dup), full(dupt)] + [tab(ROPE_LANES)] * 4,
        out_specs=[pl.BlockSpec((None, w, tm), lambda bi, i: (bi, 0, i)) if k in transposed else tok(w)
                   for k, w in enumerate(widths)],
        out_shape=[jax.ShapeDtypeStruct((b, w, t) if k in transposed else (b, t, w), dt)
                   for k, (w, dt) in enumerate(zip(widths, dtypes))],
        compiler_params=_params(("parallel", "parallel")),
        name="in_proj_heads",
    )(xs, mod, nw.reshape(1, d), w_in_p, ga, gb, hn, kpl, kone, dup, dupt, cosa, sina, cosb, sinb)


def _attn_a_kernel(lamv_ref, subln_ref, qt_ref, kx_ref, vt_ref, o_ref, kmax_scr, *,
                   tq, tk, ctx, nk, q_tile0, lam_init):
    step = pl.program_id(1)

    @pl.when(step == 0)
    def _():
        r = lax.broadcasted_iota(jnp.int32, (A_HLANES, A_HLANES), 0)
        c = lax.broadcasted_iota(jnp.int32, (A_HLANES, A_HLANES), 1)
        in_comp = jnp.logical_and(r >= c * A_QK, r < (c + 1) * A_QK)
        sel = jnp.where(jnp.logical_and(in_comp, c < 2), 1.0, 0.0).astype(BF16)
        for h in range(A_HEADS):
            kk = kx_ref[:, h * A_HLANES:(h + 1) * A_HLANES].astype(F32)
            ksq = jnp.dot((kk * kk).astype(BF16), sel, preferred_element_type=F32)
            kmax_scr[h:h + 1, :] = jnp.sqrt(jnp.max(ksq, axis=0, keepdims=True))

    lv = lamv_ref[...]
    lam = (jnp.exp(jnp.sum(lv[0:1] * lv[1:2], axis=1, keepdims=True))
           - jnp.exp(jnp.sum(lv[2:3] * lv[3:4], axis=1, keepdims=True)) + lam_init)
    qt = qt_ref[...].astype(F32)
    qcs = [qt[hc * A_QK:(hc + 1) * A_QK, :] for hc in range(2 * A_HEADS)]
    shifts = [jnp.sqrt(jnp.sum(qc * qc, axis=0, keepdims=True)) * kmax_scr[hc // 2:hc // 2 + 1, hc % 2:hc % 2 + 1]
              for hc, qc in enumerate(qcs)]
    shift_max = jnp.max(functools.reduce(jnp.maximum, shifts))

    def q_ext(h, shifted):
        z = jnp.zeros((A_QK, tq), F32)
        top = jnp.concatenate([qcs[2 * h], z], axis=1)
        mid = jnp.concatenate([z, qcs[2 * h + 1]], axis=1)
        row = lax.broadcasted_iota(jnp.int32, (A_HLANES - 2 * A_QK, 2 * tq), 0)
        if shifted:
            bot = jnp.where(row == 0, -jnp.concatenate([shifts[2 * h], shifts[2 * h + 1]], axis=1), 0.0)
        else:
            bot = jnp.zeros(row.shape, F32)
        return jnp.concatenate([top, mid, bot], axis=0).astype(BF16)

    def scores(qx, h, start, size):
        return jnp.dot(kx_ref[pl.ds(start, size), h * A_HLANES:(h + 1) * A_HLANES], qx, preferred_element_type=F32)

    def v_ext(h, start, size):
        ones = jnp.ones((A_VROWS - A_V, size), BF16)
        return jnp.concatenate([vt_ref[h * A_V:(h + 1) * A_V, pl.ds(start, size)], ones], axis=0)

    def finish(accs):
        rows = []
        for acc in accs:
            o = (acc[:A_V, :tq] / acc[A_V:A_V + 1, :tq]) - lam * (acc[:A_V, tq:] / acc[A_V:A_V + 1, tq:])
            ms = jnp.mean(o * o, axis=0, keepdims=True)
            rows.append(o * lax.rsqrt(ms + EPS) * subln_ref[...] * (1.0 - lam_init))
        o_ref[...] = jnp.concatenate(rows, axis=0).T.astype(o_ref.dtype)

    def attend(shifted):
        qx = [q_ext(h, shifted) for h in range(A_HEADS)]

        def head_step(h, start, carry):
            s = scores(qx[h], h, start, tk)
            ve = v_ext(h, start, tk)
            if shifted:
                return carry + jnp.dot(ve, jnp.exp2(s).astype(BF16), preferred_element_type=F32)
            m_run, acc = carry
            m_new = jnp.maximum(m_run, jnp.max(s, axis=0, keepdims=True))
            p = jnp.exp2(s - m_new).astype(BF16)
            return m_new, jnp.exp2(m_run - m_new) * acc + jnp.dot(ve, p, preferred_element_type=F32)

        def body(j, carries):
            start = pl.multiple_of(j * tk, tk)
            return tuple(head_step(h, start, carries[h]) for h in range(A_HEADS))

        acc0 = jnp.zeros((A_VROWS, 2 * tq), F32)
        if shifted:
            accs = lax.fori_loop(0, nk, body, (acc0,) * A_HEADS, unroll=True)
        else:
            init = (jnp.full((1, 2 * tq), NEG_INF, F32), acc0)
            accs = [c[1] for c in lax.fori_loop(0, nk, body, (init,) * A_HEADS)]
        finish(accs)

    def attend_ctx():
        accs = []
        for h in range(A_HEADS):
            s = scores(q_ext(h, False), h, 0, ctx)
            p = jnp.exp2(s - jnp.max(s, axis=0, keepdims=True)).astype(BF16)
            accs.append(jnp.dot(v_ext(h, 0, ctx), p, preferred_element_type=F32))
        finish(accs)

    safe = shift_max < A_SHIFT_CAP
    if q_tile0 * tq < ctx:
        is_ctx = (step + q_tile0) * tq < ctx
        pl.when(is_ctx)(attend_ctx)
        pl.when(jnp.logical_and(jnp.logical_not(is_ctx), safe))(lambda: attend(True))
        pl.when(jnp.logical_and(jnp.logical_not(is_ctx), jnp.logical_not(safe)))(lambda: attend(False))
    else:
        pl.when(safe)(lambda: attend(True))
        pl.when(jnp.logical_not(safe))(lambda: attend(False))


def _attn_a_call(lamv, subln_col, qa_t, ka_x, va_t, *, tq, tk, ctx, q_tile0, lam_init):
    b, w, t = qa_t.shape
    nq = t // tq - q_tile0
    return pl.pallas_call(
        functools.partial(_attn_a_kernel, tq=tq, tk=tk, ctx=ctx, nk=t // tk, q_tile0=q_tile0, lam_init=lam_init),
        grid=(b, nq),
        in_specs=[pl.BlockSpec(lamv.shape, lambda bi, qi: (0, 0)),
                  pl.BlockSpec(subln_col.shape, lambda bi, qi: (0, 0)),
                  pl.BlockSpec((None, w, tq), lambda bi, qi: (bi, 0, qi + q_tile0)),
                  pl.BlockSpec((None, t, A_HEADS * A_HLANES), lambda bi, qi: (bi, 0, 0)),
                  pl.BlockSpec((None, w, t), lambda bi, qi: (bi, 0, 0))],
        out_specs=pl.BlockSpec((None, tq, w), lambda bi, qi: (bi, qi, 0)),
        out_shape=jax.ShapeDtypeStruct((b, nq * tq, w), BF16),
        scratch_shapes=[pltpu.VMEM((8, A_HLANES), F32)],
        compiler_params=_params(("parallel", "arbitrary")),
        name="diff_attention",
    )(lamv, subln_col, qa_t, ka_x, va_t)


def _attn_b_kernel(sink_ref, band_ref, q_ref, kp_ref, k0_ref, k1_ref, kn_ref, kx_ref,
                   vp_ref, v0_ref, v1_ref, vn_ref, vx_ref, o_ref, *, blk0, ctx_blocks, nblk):
    n0 = pl.program_id(1) * B_QBLOCKS + blk0
    nq = B_QBLOCKS * B_BLOCK
    k_all = jnp.concatenate([kp_ref[...], k0_ref[...], k1_ref[...], kn_ref[...], kx_ref[...]], axis=1)
    v_all = jnp.concatenate([vp_ref[...], v0_ref[...], v1_ref[...], vn_ref[...], vx_ref[...]], axis=0)
    lane = lax.broadcasted_iota(jnp.int32, (1, k_all.shape[1]), 1)
    lat = n0 >= ctx_blocks
    pen = jnp.zeros(lane.shape, F32)
    for d in range(B_QBLOCKS + 2):
        kb = n0 - 1 + d
        ok = jnp.logical_and(lat, jnp.logical_and(kb >= ctx_blocks, kb <= nblk - 1))
        in_blk = jnp.logical_and(lane >= d * B_BLOCK, lane < (d + 1) * B_BLOCK)
        pen = jnp.where(in_blk, jnp.where(ok, 0.0, NEG_INF), pen)
    bias = band_ref[...] + pen
    hpg = B_HEADS // B_KV_HEADS
    gl = hpg * B_DIM
    qlane = lax.broadcasted_iota(jnp.int32, (nq, gl), 1) // B_DIM
    outs = []
    for g in range(B_KV_HEADS):
        q_g = q_ref[:, g * gl:(g + 1) * gl]
        o_g = jnp.zeros((nq, gl), F32)
        for e in range(hpg):
            q_e = jnp.where(qlane == e, q_g, jnp.zeros_like(q_g))
            s = jnp.dot(q_e, k_all[g * gl:(g + 1) * gl, :], preferred_element_type=F32) + bias
            sk = sink_ref[g * hpg + e]
            m = jnp.maximum(jnp.max(s, axis=1, keepdims=True), sk)
            p = jnp.exp(s - m)
            den = jnp.sum(p, axis=1, keepdims=True) + jnp.exp(sk - m)
            pv = jnp.dot(p.astype(BF16), v_all[:, g * gl:(g + 1) * gl], preferred_element_type=F32)
            o_g = jnp.where(qlane == e, pv / den, o_g)
        outs.append(o_g)
    o_ref[...] = jnp.concatenate(outs, axis=1).astype(o_ref.dtype)


def _attn_b_call(sink, band, qb, kb_t, vb2, *, ctx, blk0):
    b, t, w = qb.shape
    nblk = t // B_BLOCK
    ctx_blocks = ctx // B_BLOCK
    nq = B_QBLOCKS * B_BLOCK
    first = lambda i: i * B_QBLOCKS + blk0
    clamp = lambda n: jnp.clip(n, 0, nblk - 1)
    kspec = lambda d: pl.BlockSpec((None, w, B_BLOCK), lambda bi, i: (bi, 0, clamp(first(i) - 1 + d)))
    vspec = lambda d: pl.BlockSpec((None, B_BLOCK, w), lambda bi, i: (bi, clamp(first(i) - 1 + d), 0))
    nwb = B_QBLOCKS + 2
    return pl.pallas_call(
        functools.partial(_attn_b_kernel, blk0=blk0, ctx_blocks=ctx_blocks, nblk=nblk),
        grid=(b, (nblk - blk0) // B_QBLOCKS),
        in_specs=[pl.BlockSpec(memory_space=pltpu.SMEM),
                  pl.BlockSpec(band.shape, lambda bi, i: (0, 0)),
                  pl.BlockSpec((None, nq, w), lambda bi, i: (bi, i + blk0 // B_QBLOCKS, 0))]
                 + [kspec(d) for d in range(nwb)] + [pl.BlockSpec((None, w, ctx), lambda bi, i: (bi, 0, 0))]
                 + [vspec(d) for d in range(nwb)] + [pl.BlockSpec((None, ctx, w), lambda bi, i: (bi, 0, 0))],
        out_specs=pl.BlockSpec((None, nq, w), lambda bi, i: (bi, i, 0)),
        out_shape=jax.ShapeDtypeStruct((b, (nblk - blk0) * B_BLOCK, w), BF16),
        compiler_params=_params(("parallel", "parallel")),
        name="window_attention",
    )(sink, band, qb, *([kb_t] * (nwb + 1)), *([vb2] * (nwb + 1)))


def _conv_kernel(u_ref, up_ref, un_ref, w_ref, b_ref, xt_o, bn_o, *, tm, ctx_tiles, ntiles):
    i = pl.program_id(1)
    u = u_ref[...]
    has_prev = jnp.logical_and(i != 0, i != ctx_tiles)
    has_next = jnp.logical_and(i != ctx_tiles - 1, i != ntiles - 1)
    up = jnp.where(has_prev, up_ref[...], 0.0)
    un = jnp.where(has_next, un_ref[...], 0.0)
    full = jnp.concatenate([up, u, un], axis=0)
    w = w_ref[...]
    acc = b_ref[...] + w[0:1] * full[HALO - 2:HALO - 2 + tm]
    for k in range(1, C_CONV):
        acc = acc + w[k:k + 1] * full[HALO - 2 + k:HALO - 2 + k + tm]
    y = _silu(acc)
    xt_o[...] = y.T
    bn_o[...] = y[:, C_INNER:C_INNER + C_GN]


def _conv_call(xbc_raw, conv_w, conv_b, *, tm, ctx):
    b, t, ch = xbc_raw.shape
    ntiles = t // tm
    hb = tm // HALO
    nh = t // HALO
    w8 = jnp.zeros((8, ch), F32).at[:C_CONV].set(conv_w)
    return pl.pallas_call(
        functools.partial(_conv_kernel, tm=tm, ctx_tiles=ctx // tm, ntiles=ntiles),
        grid=(b, ntiles),
        in_specs=[pl.BlockSpec((None, tm, ch), lambda bi, i: (bi, i, 0)),
                  pl.BlockSpec((None, HALO, ch), lambda bi, i: (bi, jnp.maximum(i * hb - 1, 0), 0)),
                  pl.BlockSpec((None, HALO, ch), lambda bi, i: (bi, jnp.minimum((i + 1) * hb, nh - 1), 0)),
                  pl.BlockSpec((8, ch), lambda bi, i: (0, 0)),
                  pl.BlockSpec((1, ch), lambda bi, i: (0, 0))],
        out_specs=[pl.BlockSpec((None, ch, tm), lambda bi, i: (bi, 0, i)),
                   pl.BlockSpec((None, tm, C_GN), lambda bi, i: (bi, i, 0))],
        out_shape=[jax.ShapeDtypeStruct((b, ch, t), F32), jax.ShapeDtypeStruct((b, t, C_GN), F32)],
        compiler_params=_params(("parallel", "parallel")),
        name="ssm_conv",
    )(xbc_raw, xbc_raw, xbc_raw, w8, conv_b.reshape(1, ch))


def _softplus(v):
    return jnp.maximum(v, 0.0) + jnp.log1p(jnp.exp(-jnp.abs(v)))


def _split3(v):
    v1 = v.astype(BF16)
    r1 = v - v1.astype(F32)
    v2 = r1.astype(BF16)
    v3 = (r1 - v2.astype(F32)).astype(BF16)
    return v1, v2, v3


def _ssd_prepare(xt_all, bn, dt_raw, par, *, lane0, backward):
    c_t = xt_all[C_INNER + C_GN:]
    dt = _softplus(dt_raw + par[0:1])
    a = dt * par[1:2]
    s_idx = lax.broadcasted_iota(jnp.int32, (C_CHUNK, C_CHUNK), 0)
    l_idx = lax.broadcasted_iota(jnp.int32, (C_CHUNK, C_CHUNK), 1)
    tri = jnp.where(l_idx <= s_idx, 1.0, 0.0).astype(BF16)
    cs3 = jnp.dot(tri, jnp.concatenate(_split3(a), axis=1), preferred_element_type=F32)
    cs = cs3[:, :DT_LANES] + cs3[:, DT_LANES:2 * DT_LANES] + cs3[:, 2 * DT_LANES:]
    key = cs - a if backward else cs
    k1, k2, k3 = _split3(key)
    parts = (k1.astype(F32) + pltpu.roll(k2.astype(F32), 2 * C_HEADS, 1)
             + pltpu.roll(k3.astype(F32), 4 * C_HEADS, 1)).astype(BF16)
    src = lax.broadcasted_iota(jnp.int32, (DT_LANES, C_HEADS * C_CHUNK), 0)
    dst = lax.broadcasted_iota(jnp.int32, (DT_LANES, C_HEADS * C_CHUNK), 1)
    pick = jnp.logical_and(src < 6 * C_HEADS, (src & (2 * C_HEADS - 1)) == lane0 + dst // C_CHUNK)
    colb_all = jnp.dot(parts, jnp.where(pick, 1.0, 0.0).astype(BF16), preferred_element_type=F32)
    return dict(x_t=xt_all[:C_INNER], c_t=c_t, c16=c_t.astype(BF16), bn=bn, dt_t=dt.T, key_t=key.T, cs=cs,
                colb_all=colb_all, mask=(s_idx >= l_idx) if backward else (s_idx <= l_idx))


def _ssd_group(st, g, hs, *, lane0, backward):
    lane_g = lax.broadcasted_iota(jnp.int32, st["bn"].shape, 1) // C_STATE
    row_g = lax.broadcasted_iota(jnp.int32, st["c_t"].shape, 0) // C_STATE
    bn_g = jnp.where(lane_g == g, st["bn"], 0.0).astype(BF16)
    ct_g = jnp.where(row_g == g, st["c_t"], 0.0).astype(BF16)
    cb_t = jnp.dot(bn_g, st["c16"], preferred_element_type=F32)
    y_off = jnp.dot(hs.astype(BF16), ct_g, preferred_element_type=F32)
    y_g, xw_g, hs_new, xdts, m_ts, carries = [], [], [], [], [], []
    for r in range(C_HPG):
        h = g * C_HPG + r
        hl = lane0 + h
        colb = st["colb_all"][:, h * C_CHUNK:(h + 1) * C_CHUNK]
        row = st["key_t"][hl:hl + 1, :]
        tot_h = st["cs"][C_CHUNK - 1:C_CHUNK, hl:hl + 1]
        d = (colb - row) if backward else (row - colb)
        m_ts.append((cb_t * jnp.exp(jnp.where(st["mask"], d, NEG_INF))).astype(BF16))
        xdt = st["x_t"][h * C_HEAD_DIM:(h + 1) * C_HEAD_DIM] * st["dt_t"][hl:hl + 1, :]
        xdts.append(xdt.astype(BF16))
        carries.append(jnp.exp(tot_h - row) if backward else jnp.exp(row))
        w_row = jnp.exp(row) if backward else jnp.exp(tot_h - row)
        xw_g.append((xdt * w_row).astype(BF16))
        hs_new.append(jnp.exp(tot_h) * hs[r * C_HEAD_DIM:(r + 1) * C_HEAD_DIM])
    for r in range(0, C_HPG, 2):
        both = jnp.dot(jnp.concatenate(xdts[r:r + 2], axis=0), jnp.concatenate(m_ts[r:r + 2], axis=1),
                       preferred_element_type=F32)
        for e in range(2):
            y_diag = both[e * C_HEAD_DIM:(e + 1) * C_HEAD_DIM, e * C_CHUNK:(e + 1) * C_CHUNK]
            y_g.append(y_diag + y_off[(r + e) * C_HEAD_DIM:(r + e + 1) * C_HEAD_DIM] * carries[r + e])
    state = jnp.concatenate(hs_new, axis=0) + jnp.dot(jnp.concatenate(xw_g, axis=0), bn_g, preferred_element_type=F32)
    return y_g, state


def _ssd_kernel(par_ref, dsk_ref, xf_ref, bf_ref, dtf_ref, xb_ref, bb_ref, dtb_ref, yf_ref, yb_ref,
                hf_scr, hb_scr, dsk_scr):
    @pl.when(pl.program_id(1) == 0)
    def _():
        hf_scr[...] = jnp.zeros(hf_scr.shape, F32)
        hb_scr[...] = jnp.zeros(hb_scr.shape, F32)
        dsk_scr[...] = jnp.broadcast_to(dsk_ref[...], dsk_scr.shape)

    par = par_ref[...]
    hf = [hf_scr[g] for g in range(C_GROUPS)]
    hb = [hb_scr[g] for g in range(C_GROUPS)]
    for k in range(C_STEP_CHUNKS):
        cf = slice(k * C_CHUNK, (k + 1) * C_CHUNK)
        cb = slice((C_STEP_CHUNKS - 1 - k) * C_CHUNK, (C_STEP_CHUNKS - k) * C_CHUNK)
        sf = _ssd_prepare(xf_ref[:, cf], bf_ref[cf, :], dtf_ref[cf, :], par, lane0=0, backward=False)
        sb = _ssd_prepare(xb_ref[:, cb], bb_ref[cb, :], dtb_ref[cb, :], par, lane0=C_HEADS, backward=True)
        rows_f, rows_b = [], []
        for g in range(C_GROUPS):
            y_g, hf[g] = _ssd_group(sf, g, hf[g], lane0=0, backward=False)
            rows_f += y_g
            y_g, hb[g] = _ssd_group(sb, g, hb[g], lane0=C_HEADS, backward=True)
            rows_b += y_g
        yf_ref[cf, :] = (jnp.concatenate(rows_f, axis=0) + dsk_scr[...] * sf["x_t"]).T.astype(yf_ref.dtype)
        yb_ref[cb, :] = jnp.concatenate(rows_b, axis=0).T.astype(yb_ref.dtype)
    for g in range(C_GROUPS):
        hf_scr[g] = hf[g]
        hb_scr[g] = hb[g]


def _ssd_call(par, dsk_col, xbc_t, bn, dt, *, ctx):
    b, ch, t = xbc_t.shape
    blk = C_STEP_CHUNKS * C_CHUNK
    nblk = t // blk
    ncb = ctx // blk
    bblock = lambda j: jnp.where(j < ncb, ncb - 1 - j, nblk - 1 + ncb - j)
    fwd = lambda bi, j: (bi, j, 0)
    bwd = lambda bi, j: (bi, bblock(j), 0)
    fwd_t = lambda bi, j: (bi, 0, j)
    bwd_t = lambda bi, j: (bi, 0, bblock(j))
    tspec = lambda im: pl.BlockSpec((None, ch, blk), im)
    nspec = lambda w, im: pl.BlockSpec((None, blk, w), im)
    return pl.pallas_call(
        _ssd_kernel,
        grid=(b, nblk),
        in_specs=[pl.BlockSpec((8, DT_LANES), lambda bi, j: (0, 0)),
                  pl.BlockSpec(dsk_col.shape, lambda bi, j: (0, 0)),
                  tspec(fwd_t), nspec(C_GN, fwd), nspec(DT_LANES, fwd),
                  tspec(bwd_t), nspec(C_GN, bwd), nspec(DT_LANES, bwd)],
        out_specs=[nspec(C_INNER, fwd), nspec(C_INNER, bwd)],
        out_shape=[jax.ShapeDtypeStruct((b, t, C_INNER), BF16)] * 2,
        scratch_shapes=[pltpu.VMEM((C_GROUPS, C_HPG * C_HEAD_DIM, C_GN), F32)] * 2
                       + [pltpu.VMEM((C_INNER, C_CHUNK), F32)],
        compiler_params=_params(("parallel", "arbitrary")),
        name="ssd_scan",
    )(par, dsk_col, xbc_t, bn, dt, xbc_t, bn, dt)


def _outproj_kernel(x_ref, mod_ref, oa_ref, ob_ref, yf_ref, yb_ref, z_ref, gn_ref, w_ref, o_ref, *,
                    sub, nsub, sub0, ctx_subs, nbatch):
    m_lat, m_ctx = mod_ref[pl.program_id(0)], mod_ref[nbatch]
    gn = gn_ref[...]
    gw = C_INNER // C_GROUPS
    for k in range(nsub):
        rows = slice(k * sub, (k + 1) * sub)
        m = jnp.where(pl.program_id(1) * nsub + k + sub0 < ctx_subs, m_ctx, m_lat)
        y = (yf_ref[rows, :].astype(F32) + yb_ref[rows, :].astype(F32)) * _silu(z_ref[rows, :].astype(F32))
        oc = []
        for g in range(C_GROUPS):
            yg = y[:, g * gw:(g + 1) * gw]
            ms = jnp.mean(yg * yg, axis=1, keepdims=True)
            oc.append(yg * lax.rsqrt(ms + EPS) * gn[:, g * gw:(g + 1) * gw])
        mix = jnp.concatenate([oa_ref[rows, :], ob_ref[rows, :]] + [v.astype(BF16) for v in oc], axis=1)
        o_ref[rows, :] = x_ref[rows, :] + m[5:6] * jnp.dot(mix, w_ref[...], preferred_element_type=F32)


def _outproj_call(xs, mod, oa, ob, yf, yb, z, gn, w_out, layer, *, tm, ctx_rows, row0, nbatch):
    b, t, d = xs.shape
    assert row0 % tm == 0
    ntiles = (t - row0) // tm
    tile0 = row0 // tm
    tok = lambda w: pl.BlockSpec((None, tm, w), lambda bi, i: (bi, i + tile0, 0))
    tok0 = lambda w: pl.BlockSpec((None, tm, w), lambda bi, i: (bi, i, 0))
    return pl.pallas_call(
        functools.partial(_outproj_kernel, sub=SUB_ROWS, nsub=tm // SUB_ROWS, sub0=row0 // SUB_ROWS,
                          ctx_subs=ctx_rows // SUB_ROWS, nbatch=nbatch),
        grid=(b, ntiles),
        in_specs=[tok(d), pl.BlockSpec(mod.shape, lambda bi, i: (0, 0, 0)),
                  tok0(A_HEADS * A_V), tok0(B_QCOLS), tok(C_INNER), tok(C_INNER), tok(C_INNER),
                  pl.BlockSpec(gn.shape, lambda bi, i: (0, 0)),
                  pl.BlockSpec((None,) + w_out.shape[1:], lambda bi, i: (layer, 0, 0),
                               pipeline_mode=pl.Buffered(1))],
        out_specs=tok0(d),
        out_shape=jax.ShapeDtypeStruct((b, ntiles * tm, d), F32),
        compiler_params=_params(("parallel", "parallel")),
        name="mix_out_proj",
    )(xs, mod, oa, ob, yf, yb, z, gn, w_out)


def _rope_tables(seq, ctx, dim, reps):
    rows = seq // GRID_W
    row = jnp.repeat(jnp.arange(rows, dtype=F32), GRID_W)
    col = jnp.tile(jnp.arange(GRID_W, dtype=F32), rows)
    quarter = dim // 4
    inv_freq = ROPE_BASE ** (-jnp.arange(quarter, dtype=F32) / quarter)
    ar = row[:, None] * inv_freq
    ac = col[:, None] * inv_freq
    ang = jnp.concatenate([ar, ar, ac, ac], axis=-1)
    sign = jnp.where((jnp.arange(dim) % (dim // 2)) < quarter, -1.0, 1.0).astype(F32)
    cos = jnp.concatenate([jnp.ones((ctx, dim), F32), jnp.cos(ang)], axis=0)
    sin = jnp.concatenate([jnp.zeros((ctx, dim), F32), jnp.sin(ang) * sign], axis=0)
    return jnp.tile(cos, (1, reps)), jnp.tile(sin, (1, reps))


def _group_matrix(n, group):
    idx = jnp.arange(n) // group
    return (idx[:, None] == idx[None, :]).astype(BF16)


def kernel(x, c, ctx, c_ctx, w_mod, b_mod, norm_ffn1, ffn1_w13, ffn1_w2, norm_mix, w_in, w_out, qn_a, kn_a, lam_q1, lam_k1, lam_q2, lam_k2, subln_a, qn_b, kn_b, sink_b, conv_w, conv_b, dt_bias, a_log, d_skip, gnorm_c, norm_ffn2, ffn2_w13, ffn2_w2):
    nb, seq, d = x.shape
    nctx = ctx.shape[1]
    t = nctx + seq
    depth = w_mod.shape[0]
    tm = 256
    big = lambda rows: next(c for c in (768, 512, 256) if rows % c == 0)
    tm_all, tm_lat = big(t), big(seq)
    fc = 256
    tq, tk = 256, (768 if t % 768 == 0 and t > 768 else 256)
    assert nctx == tm == SUB_ROWS and nctx % (C_STEP_CHUNKS * C_CHUNK) == 0 and seq % GRID_W == 0 and nb < 8

    xs = x
    cvec = jnp.zeros((8, d), F32).at[:nb].set(c).at[nb].set(c_ctx)
    mod = _mod_call(cvec, w_mod, b_mod).reshape(depth, 8, N_MOD, d)

    cosa, sina = _rope_tables(seq, nctx, A_QK, ROPE_LANES // A_QK)
    cosb, sinb = _rope_tables(seq, nctx, B_DIM, ROPE_LANES // B_DIM)
    ga = _group_matrix(A_QCOLS, A_QK)
    gb = _group_matrix(B_QCOLS, B_DIM)
    src = jnp.arange(A_QCOLS)
    dst = (src // (2 * A_QK)) * A_HLANES + src % (2 * A_QK)
    kpl = (dst[:, None] == jnp.arange(A_HEADS * A_HLANES)[None, :]).astype(BF16)
    kone = (jnp.arange(A_HEADS * A_HLANES) % A_HLANES == 2 * A_QK).astype(F32).reshape(1, -1)
    dsrc = jnp.arange(B_KCOLS)
    dcol = jnp.arange(B_QCOLS)
    hpg = B_HEADS // B_KV_HEADS
    dup = jnp.logical_and(dcol[None, :] // (hpg * B_DIM) == dsrc[:, None] // B_DIM,
                          dcol[None, :] % B_DIM == dsrc[:, None] % B_DIM).astype(BF16)
    qi = jnp.arange(B_QBLOCKS * B_BLOCK)[:, None]
    kj = jnp.arange((B_QBLOCKS + 2) * B_BLOCK + nctx)[None, :]
    in_band = jnp.logical_and(kj - qi >= 0, kj - qi <= 2 * B_BLOCK)
    band = jnp.where(jnp.logical_or(in_band, kj >= (B_QBLOCKS + 2) * B_BLOCK), 0.0, NEG_INF).astype(F32)
    in_cols = w_in.shape[2]
    in_pad = (-in_cols) % DT_LANES
    w13_1, w2_1 = ffn1_w13.astype(BF16), ffn1_w2.astype(BF16)
    w13_2, w2_2 = ffn2_w13.astype(BF16), ffn2_w2.astype(BF16)
    w_in_p = jnp.pad(w_in, ((0, 0), (0, 0), (0, in_pad))).astype(BF16)
    w_out_b = w_out.astype(BF16)

    for l in range(depth):
        last = l == depth - 1
        lam_init = 0.8 - 0.6 * math.exp(-0.3 * l)
        hn = jnp.stack([jnp.tile(qn_a[l], A_QCOLS // A_QK), jnp.tile(kn_a[l], A_QCOLS // A_QK),
                        jnp.tile(qn_b[l], B_QCOLS // B_DIM), jnp.tile(kn_b[l], B_QCOLS // B_DIM)]).astype(F32)
        hn = jnp.zeros((8, A_QCOLS), F32).at[:4].set(hn)

        xs = _ffn_call(xs, mod[l], norm_ffn1[l], w13_1, w2_1, l, si=0, fc=fc,
                       tm=(SUB_ROWS if l == 0 else tm_all), ctx_rows=nctx, nbatch=nb, ctx_in=(ctx if l == 0 else None))

        qa_t, ka_x, va_t, qb, kb_t, vb2, z, xbc_raw, dt = _inproj_call(
            xs, mod[l], norm_mix[l], w_in_p, l, ga, gb, hn, kpl, kone, dup, dup.T, cosa, sina, cosb, sinb,
            tm=tm_all, ctx_rows=nctx, nbatch=nb)

        lamv = jnp.zeros((8, A_QK), F32).at[:4].set(jnp.stack([lam_q1[l], lam_k1[l], lam_q2[l], lam_k2[l]]))
        oa = _attn_a_call(lamv, subln_a[l].reshape(A_V, 1), qa_t, ka_x, va_t, tq=tq, tk=tk, ctx=nctx,
                          q_tile0=(nctx // tq if last else 0), lam_init=lam_init)

        ob = _attn_b_call(sink_b[l].astype(F32), band, qb, kb_t, vb2, ctx=nctx, blk0=(nctx // B_BLOCK if last else 0))

        xbc_t, bn = _conv_call(xbc_raw, conv_w[l], conv_b[l], tm=tm, ctx=nctx)
        par = (jnp.zeros((8, DT_LANES), F32)
               .at[0, :2 * C_HEADS].set(dt_bias[l].reshape(-1))
               .at[1, :2 * C_HEADS].set(-jnp.exp(a_log[l].astype(F32)).reshape(-1)))
        dsk_col = jnp.repeat(d_skip[l].astype(F32), C_HEAD_DIM).reshape(C_INNER, 1)
        yf, yb = _ssd_call(par, dsk_col, xbc_t, bn, dt, ctx=nctx)

        xs = _outproj_call(xs, mod[l], oa, ob, yf, yb, z, gnorm_c[l].reshape(1, C_INNER), w_out_b, l,
                           tm=(SUB_ROWS if last else tm_all), ctx_rows=nctx, row0=(nctx if last else 0), nbatch=nb)
        xs = _ffn_call(xs, mod[l], norm_ffn2[l], w13_2, w2_2, l, si=6, fc=fc,
                       tm=(tm_lat if last else tm_all), ctx_rows=(0 if last else nctx), nbatch=nb)
    return xs
```

```python
import functools
import math

import jax
import jax.numpy as jnp
from jax import lax
from jax.experimental import pallas as pl
from jax.experimental.pallas import tpu as pltpu

F32 = jnp.float32
BF16 = jnp.bfloat16
NEG_INF = float("-inf")

EPS = 1e-6
ROPE_BASE = 10000.0
GRID_W = 64
ROPE_LANES = 128
N_MOD = 9

A_HEADS, A_QK, A_V = 4, 32, 64
A_QCOLS = A_HEADS * 2 * A_QK
A_SCALE = A_QK ** -0.5
LOG2E = math.log2(math.e)
A_HLANES = 128
A_VROWS = A_V + 16
A_SHIFT_CAP = 48.0
B_HEADS, B_KV_HEADS, B_DIM = 4, 2, 64
B_QCOLS = B_HEADS * B_DIM
B_KCOLS = B_KV_HEADS * B_DIM
B_SCALE = B_DIM ** -0.5
B_BLOCK = 128
B_QBLOCKS = 2
C_HEADS, C_HEAD_DIM, C_GROUPS, C_STATE, C_CONV = 8, 64, 2, 64, 5
C_INNER = C_HEADS * C_HEAD_DIM
C_GN = C_GROUPS * C_STATE
C_XBC = C_INNER + 2 * C_GN
C_CHUNK = 128
C_STEP_CHUNKS = 2
C_HPG = C_HEADS // C_GROUPS
DT_LANES = 128
HALO = 8

SUB_ROWS = 256

V7X_VMEM_LIMIT = 56 * 1024 * 1024


def _params(sem, vmem=V7X_VMEM_LIMIT):
    return pltpu.CompilerParams(dimension_semantics=sem, vmem_limit_bytes=vmem)


def _rms_mod(x, nw, shift, scale):
    ms = jnp.mean(x * x, axis=-1, keepdims=True)
    y = x * lax.rsqrt(ms + EPS) * nw
    return y * (1.0 + scale) + shift


def _silu(v):
    return v * jax.nn.sigmoid(v)


def _mod_kernel(c_ref, w_ref, b_ref, o_ref):
    sc = _silu(c_ref[...])
    c1 = sc.astype(BF16)
    c2 = (sc - c1.astype(F32)).astype(BF16)
    w = w_ref[...]
    w1 = w.astype(BF16)
    w2 = (w - w1.astype(F32)).astype(BF16)
    dot = lambda a, bb: jnp.dot(a, bb, preferred_element_type=F32)
    o_ref[...] = dot(c1, w1) + (dot(c1, w2) + dot(c2, w1)) + b_ref[...]


def _mod_call(cvec, w_mod, b_mod, tn=1152):
    nl, d, n = w_mod.shape
    return pl.pallas_call(
        _mod_kernel,
        grid=(nl, n // tn),
        in_specs=[pl.BlockSpec((8, d), lambda l, j: (0, 0)),
                  pl.BlockSpec((None, d, tn), lambda l, j: (l, 0, j)),
                  pl.BlockSpec((None, 1, tn), lambda l, j: (l, 0, j))],
        out_specs=pl.BlockSpec((None, 8, tn), lambda l, j: (l, 0, j)),
        out_shape=jax.ShapeDtypeStruct((nl, 8, n), F32),
        compiler_params=_params(("parallel", "parallel")),
        name="adaln_mod",
    )(cvec, w_mod, b_mod.reshape(nl, 1, n))


def _ffn_kernel(*refs, si, fc, ff, sub, nsub, ctx_subs, nbatch, split_input):
    if split_input:
        c_ref, x_ref, mod_ref, nw_ref, w13_ref, w2_ref, o_ref, a_scr = refs
    else:
        x_ref, mod_ref, nw_ref, w13_ref, w2_ref, o_ref, a_scr = refs
    m_lat, m_ctx = mod_ref[pl.program_id(0)], mod_ref[nbatch]
    for k in range(nsub):
        rows = slice(k * sub, (k + 1) * sub)
        is_ctx = pl.program_id(1) * nsub + k < ctx_subs
        x = jnp.where(is_ctx, c_ref[...], x_ref[...]) if split_input else x_ref[rows, :]
        m = jnp.where(is_ctx, m_ctx, m_lat)
        h = _rms_mod(x, nw_ref[...], m[si:si + 1], m[si + 1:si + 2]).astype(BF16)
        for c in range(ff // fc):
            g = jnp.dot(h, w13_ref[:, c * fc:(c + 1) * fc], preferred_element_type=F32)
            u = jnp.dot(h, w13_ref[:, ff + c * fc:ff + (c + 1) * fc], preferred_element_type=F32)
            a_scr[k, :, c * fc:(c + 1) * fc] = (_silu(g) * u).astype(BF16)
        y = jnp.dot(a_scr[k], w2_ref[...], preferred_element_type=F32)
        o_ref[rows, :] = x + (0.5 * m[si + 2:si + 3]) * y


def _ffn_call(xs, mod, nw, w13, w2, layer, *, si, tm, fc, ctx_rows, nbatch, ctx_in=None):
    b, t, d = xs.shape
    ff = w2.shape[1]
    split = ctx_in is not None
    nsub = tm // SUB_ROWS
    ctx_subs = ctx_rows // SUB_ROWS
    assert not split or nsub == 1
    ntiles = t // tm + (ctx_subs if split else 0)
    if split:
        tok_specs = [pl.BlockSpec((None, tm, d), lambda bi, i: (bi, jnp.minimum(i, ctx_subs - 1), 0)),
                     pl.BlockSpec((None, tm, d), lambda bi, i: (bi, jnp.maximum(i - ctx_subs, 0), 0))]
        toks = (ctx_in, xs)
    else:
        tok_specs = [pl.BlockSpec((None, tm, d), lambda bi, i: (bi, i, 0))]
        toks = (xs,)
    return pl.pallas_call(
        functools.partial(_ffn_kernel, si=si, fc=fc, ff=ff, sub=SUB_ROWS, nsub=nsub, ctx_subs=ctx_subs,
                          nbatch=nbatch, split_input=split),
        grid=(b, ntiles),
        in_specs=tok_specs + [pl.BlockSpec(mod.shape, lambda bi, i: (0, 0, 0)),
                              pl.BlockSpec((1, d), lambda bi, i: (0, 0)),
                              pl.BlockSpec((None,) + w13.shape[1:], lambda bi, i: (layer, 0, 0),
                                           pipeline_mode=pl.Buffered(1)),
                              pl.BlockSpec((None,) + w2.shape[1:], lambda bi, i: (layer, 0, 0),
                                           pipeline_mode=pl.Buffered(1))],
        out_specs=pl.BlockSpec((None, tm, d), lambda bi, i: (bi, i, 0)),
        out_shape=jax.ShapeDtypeStruct((b, ntiles * tm, d), F32),
        scratch_shapes=[pltpu.VMEM((nsub, SUB_ROWS, ff), BF16)],
        compiler_params=_params(("parallel", "parallel")),
        name="swiglu_half",
    )(*toks, mod, nw.reshape(1, d), w13, w2)


def _rope(v, cos, sin_signed, quarter):
    n = v.shape[-1]
    lane = lax.broadcasted_iota(jnp.int32, v.shape, 1)
    first = (lane & (2 * quarter - 1)) < quarter
    vr = jnp.where(first, pltpu.roll(v, n - quarter, 1), pltpu.roll(v, quarter, 1))
    return v * cos + vr * sin_signed


def _group_norm(v, gmat, inv_n, w):
    sq = v * v
    hi = sq.astype(BF16)
    lo = (sq - hi.astype(F32)).astype(BF16)
    ms = (jnp.dot(hi, gmat, preferred_element_type=F32) + jnp.dot(lo, gmat, preferred_element_type=F32)) * inv_n
    return v * lax.rsqrt(ms + EPS) * w


def _inproj_kernel(x_ref, mod_ref, nw_ref, w_ref, ga_ref, gb_ref, hn_ref, kpl_ref, kone_ref, dup_ref, dupt_ref,
                   cosa_ref, sina_ref, cosb_ref, sinb_ref, qa_o, ka_o, va_o, qb_o, kb_o, vb_o, z_o, xbc_o, dt_o, *,
                   sub, nsub, ctx_subs, nbatch):
    m_lat, m_ctx = mod_ref[pl.program_id(0)], mod_ref[nbatch]
    hn = hn_ref[...]
    ga, gb = ga_ref[...], gb_ref[...]
    twice = lambda a: jnp.concatenate([a, a], axis=1)
    for k in range(nsub):
        rows = slice(k * sub, (k + 1) * sub)
        m = jnp.where(pl.program_id(1) * nsub + k < ctx_subs, m_ctx, m_lat)
        h = _rms_mod(x_ref[rows, :], nw_ref[...], m[3:4], m[4:5]).astype(BF16)
        p = jnp.dot(h, w_ref[...], preferred_element_type=F32)
        cosa, sina = twice(cosa_ref[rows, :]), twice(sina_ref[rows, :])
        cosb, sinb = twice(cosb_ref[rows, :]), twice(sinb_ref[rows, :])
        o = 0
        qa = _group_norm(p[:, o:o + A_QCOLS], ga, 1.0 / A_QK, hn[0:1]); o += A_QCOLS
        ka = _group_norm(p[:, o:o + A_QCOLS], ga, 1.0 / A_QK, hn[1:2]); o += A_QCOLS
        qa_o[:, rows] = (_rope(qa, cosa, sina, A_QK // 4) * (A_SCALE * LOG2E)).T.astype(BF16)
        ka16 = _rope(ka, cosa, sina, A_QK // 4).astype(BF16)
        ka_o[rows, :] = (jnp.dot(ka16, kpl_ref[...], preferred_element_type=F32) + kone_ref[...]).astype(BF16)
        va_o[:, rows] = p[:, o:o + A_HEADS * A_V].T.astype(BF16); o += A_HEADS * A_V
        qb = _group_norm(p[:, o:o + B_QCOLS], gb, 1.0 / B_DIM, hn[2:3]); o += B_QCOLS
        kb = _group_norm(p[:, o:o + B_KCOLS], gb[:B_KCOLS, :B_KCOLS], 1.0 / B_DIM, hn[3:4, :B_KCOLS]); o += B_KCOLS
        qb_o[rows, :] = (_rope(qb, cosb, sinb, B_DIM // 4) * B_SCALE).astype(BF16)
        kb16_t = _rope(kb, cosb[:, :B_KCOLS], sinb[:, :B_KCOLS], B_DIM // 4).T.astype(BF16)
        kb_o[:, rows] = jnp.dot(dupt_ref[...], kb16_t, preferred_element_type=F32).astype(BF16)
        vb_o[rows, :] = jnp.dot(p[:, o:o + B_KCOLS].astype(BF16), dup_ref[...],
                                preferred_element_type=F32).astype(BF16); o += B_KCOLS
        z_o[rows, :] = p[:, o:o + C_INNER].astype(BF16); o += C_INNER
        xbc_o[rows, :] = p[:, o:o + C_XBC]; o += C_XBC
        dt_o[rows, :] = p[:, o:o + DT_LANES]


def _inproj_call(xs, mod, nw, w_in_p, layer, ga, gb, hn, kpl, kone, dup, dupt, cosa, sina, cosb, sinb, *,
                 tm, ctx_rows, nbatch):
    b, t, d = xs.shape
    ntiles = t // tm
    tok = lambda w: pl.BlockSpec((None, tm, w), lambda bi, i: (bi, i, 0))
    tab = lambda w: pl.BlockSpec((tm, w), lambda bi, i: (i, 0))
    full = lambda a: pl.BlockSpec(a.shape, lambda bi, i: (0,) * a.ndim)
    widths = (A_QCOLS, A_HEADS * A_HLANES, A_HEADS * A_V, B_QCOLS, B_QCOLS, B_QCOLS, C_INNER, C_XBC, DT_LANES)
    transposed = (0, 2, 4)
    dtypes = (BF16,) * 7 + (F32,) * 2
    return pl.pallas_call(
        functools.partial(_inproj_kernel, sub=SUB_ROWS, nsub=tm // SUB_ROWS, ctx_subs=ctx_rows // SUB_ROWS,
                          nbatch=nbatch),
        grid=(b, ntiles),
        in_specs=[tok(d), full(mod), pl.BlockSpec((1, d), lambda bi, i: (0, 0)),
                  pl.BlockSpec((None,) + w_in_p.shape[1:], lambda bi, i: (layer, 0, 0), pipeline_mode=pl.Buffered(1)),
                  full(ga), full(gb), full(hn), full(kpl), full(kone), full(dup), full(dupt)] + [tab(ROPE_LANES)] * 4,
        out_specs=[pl.BlockSpec((None, w, tm), lambda bi, i: (bi, 0, i)) if k in transposed else tok(w)
                   for k, w in enumerate(widths)],
        out_shape=[jax.ShapeDtypeStruct((b, w, t) if k in transposed else (b, t, w), dt)
                   for k, (w, dt) in enumerate(zip(widths, dtypes))],
        compiler_params=_params(("parallel", "parallel")),
        name="in_proj_heads",
    )(xs, mod, nw.reshape(1, d), w_in_p, ga, gb, hn, kpl, kone, dup, dupt, cosa, sina, cosb, sinb)


def _attn_a_kernel(lamv_ref, subln_ref, qt_ref, kx_ref, vt_ref, o_ref, kmax_scr, *,
                   tq, tk, ctx, nk, q_tile0, lam_init):
    step = pl.program_id(1)

    @pl.when(step == 0)
    def _():
        r = lax.broadcasted_iota(jnp.int32, (A_HLANES, A_HLANES), 0)
        c = lax.broadcasted_iota(jnp.int32, (A_HLANES, A_HLANES), 1)
        in_comp = jnp.logical_and(r >= c * A_QK, r < (c + 1) * A_QK)
        sel = jnp.where(jnp.logical_and(in_comp, c < 2), 1.0, 0.0).astype(BF16)
        for h in range(A_HEADS):
            kk = kx_ref[:, h * A_HLANES:(h + 1) * A_HLANES].astype(F32)
            ksq = jnp.dot((kk * kk).astype(BF16), sel, preferred_element_type=F32)
            kmax_scr[h:h + 1, :] = jnp.sqrt(jnp.max(ksq, axis=0, keepdims=True))

    lv = lamv_ref[...]
    lam = (jnp.exp(jnp.sum(lv[0:1] * lv[1:2], axis=1, keepdims=True))
           - jnp.exp(jnp.sum(lv[2:3] * lv[3:4], axis=1, keepdims=True)) + lam_init)
    qt = qt_ref[...].astype(F32)
    qcs = [qt[hc * A_QK:(hc + 1) * A_QK, :] for hc in range(2 * A_HEADS)]
    shifts = [jnp.sqrt(jnp.sum(qc * qc, axis=0, keepdims=True)) * kmax_scr[hc // 2:hc // 2 + 1, hc % 2:hc % 2 + 1]
              for hc, qc in enumerate(qcs)]
    shift_max = jnp.max(functools.reduce(jnp.maximum, shifts))

    def q_ext(h, shifted):
        z = jnp.zeros((A_QK, tq), F32)
        top = jnp.concatenate([qcs[2 * h], z], axis=1)
        mid = jnp.concatenate([z, qcs[2 * h + 1]], axis=1)
        row = lax.broadcasted_iota(jnp.int32, (A_HLANES - 2 * A_QK, 2 * tq), 0)
        if shifted:
            bot = jnp.where(row == 0, -jnp.concatenate([shifts[2 * h], shifts[2 * h + 1]], axis=1), 0.0)
        else:
            bot = jnp.zeros(row.shape, F32)
        return jnp.concatenate([top, mid, bot], axis=0).astype(BF16)

    def scores(qx, h, start, size):
        return jnp.dot(kx_ref[pl.ds(start, size), h * A_HLANES:(h + 1) * A_HLANES], qx, preferred_element_type=F32)

    def v_ext(h, start, size):
        ones = jnp.ones((A_VROWS - A_V, size), BF16)
        return jnp.concatenate([vt_ref[h * A_V:(h + 1) * A_V, pl.ds(start, size)], ones], axis=0)

    def finish(accs):
        rows = []
        for acc in accs:
            o = (acc[:A_V, :tq] / acc[A_V:A_V + 1, :tq]) - lam * (acc[:A_V, tq:] / acc[A_V:A_V + 1, tq:])
            ms = jnp.mean(o * o, axis=0, keepdims=True)
            rows.append(o * lax.rsqrt(ms + EPS) * subln_ref[...] * (1.0 - lam_init))
        o_ref[...] = jnp.concatenate(rows, axis=0).T.astype(o_ref.dtype)

    def attend(shifted):
        qx = [q_ext(h, shifted) for h in range(A_HEADS)]

        def head_step(h, start, carry):
            s = scores(qx[h], h, start, tk)
            ve = v_ext(h, start, tk)
            if shifted:
                return carry + jnp.dot(ve, jnp.exp2(s).astype(BF16), preferred_element_type=F32)
            m_run, acc = carry
            m_new = jnp.maximum(m_run, jnp.max(s, axis=0, keepdims=True))
            p = jnp.exp2(s - m_new).astype(BF16)
            return m_new, jnp.exp2(m_run - m_new) * acc + jnp.dot(ve, p, preferred_element_type=F32)

        def body(j, carries):
            start = pl.multiple_of(j * tk, tk)
            return tuple(head_step(h, start, carries[h]) for h in range(A_HEADS))

        acc0 = jnp.zeros((A_VROWS, 2 * tq), F32)
        if shifted:
            accs = lax.fori_loop(0, nk, body, (acc0,) * A_HEADS, unroll=True)
        else:
            init = (jnp.full((1, 2 * tq), NEG_INF, F32), acc0)
            accs = [c[1] for c in lax.fori_loop(0, nk, body, (init,) * A_HEADS)]
        finish(accs)

    def attend_ctx():
        accs = []
        for h in range(A_HEADS):
            s = scores(q_ext(h, False), h, 0, ctx)
            p = jnp.exp2(s - jnp.max(s, axis=0, keepdims=True)).astype(BF16)
            accs.append(jnp.dot(v_ext(h, 0, ctx), p, preferred_element_type=F32))
        finish(accs)

    safe = shift_max < A_SHIFT_CAP
    if q_tile0 * tq < ctx:
        is_ctx = (step + q_tile0) * tq < ctx
        pl.when(is_ctx)(attend_ctx)
        pl.when(jnp.logical_and(jnp.logical_not(is_ctx), safe))(lambda: attend(True))
        pl.when(jnp.logical_and(jnp.logical_not(is_ctx), jnp.logical_not(safe)))(lambda: attend(False))
    else:
        pl.when(safe)(lambda: attend(True))
        pl.when(jnp.logical_not(safe))(lambda: attend(False))


def _attn_a_call(lamv, subln_col, qa_t, ka_x, va_t, *, tq, tk, ctx, q_tile0, lam_init):
    b, w, t = qa_t.shape
    nq = t // tq - q_tile0
    return pl.pallas_call(
        functools.partial(_attn_a_kernel, tq=tq, tk=tk, ctx=ctx, nk=t // tk, q_tile0=q_tile0, lam_init=lam_init),
        grid=(b, nq),
        in_specs=[pl.BlockSpec(lamv.shape, lambda bi, qi: (0, 0)),
                  pl.BlockSpec(subln_col.shape, lambda bi, qi: (0, 0)),
                  pl.BlockSpec((None, w, tq), lambda bi, qi: (bi, 0, qi + q_tile0)),
                  pl.BlockSpec((None, t, A_HEADS * A_HLANES), lambda bi, qi: (bi, 0, 0)),
                  pl.BlockSpec((None, w, t), lambda bi, qi: (bi, 0, 0))],
        out_specs=pl.BlockSpec((None, tq, w), lambda bi, qi: (bi, qi, 0)),
        out_shape=jax.ShapeDtypeStruct((b, nq * tq, w), BF16),
        scratch_shapes=[pltpu.VMEM((8, A_HLANES), F32)],
        compiler_params=_params(("parallel", "arbitrary")),
        name="diff_attention",
    )(lamv, subln_col, qa_t, ka_x, va_t)


def _attn_b_kernel(sink_ref, band_ref, q_ref, kp_ref, k0_ref, k1_ref, kn_ref, kx_ref,
                   vp_ref, v0_ref, v1_ref, vn_ref, vx_ref, o_ref, *, blk0, ctx_blocks, nblk):
    n0 = pl.program_id(1) * B_QBLOCKS + blk0
    nq = B_QBLOCKS * B_BLOCK
    k_all = jnp.concatenate([kp_ref[...], k0_ref[...], k1_ref[...], kn_ref[...], kx_ref[...]], axis=1)
    v_all = jnp.concatenate([vp_ref[...], v0_ref[...], v1_ref[...], vn_ref[...], vx_ref[...]], axis=0)
    lane = lax.broadcasted_iota(jnp.int32, (1, k_all.shape[1]), 1)
    lat = n0 >= ctx_blocks
    pen = jnp.zeros(lane.shape, F32)
    for d in range(B_QBLOCKS + 2):
        kb = n0 - 1 + d
        ok = jnp.logical_and(lat, jnp.logical_and(kb >= ctx_blocks, kb <= nblk - 1))
        in_blk = jnp.logical_and(lane >= d * B_BLOCK, lane < (d + 1) * B_BLOCK)
        pen = jnp.where(in_blk, jnp.where(ok, 0.0, NEG_INF), pen)
    bias = band_ref[...] + pen
    hpg = B_HEADS // B_KV_HEADS
    gl = hpg * B_DIM
    qlane = lax.broadcasted_iota(jnp.int32, (nq, gl), 1) // B_DIM
    outs = []
    for g in range(B_KV_HEADS):
        q_g = q_ref[:, g * gl:(g + 1) * gl]
        o_g = jnp.zeros((nq, gl), F32)
        for e in range(hpg):
            q_e = jnp.where(qlane == e, q_g, jnp.zeros_like(q_g))
            s = jnp.dot(q_e, k_all[g * gl:(g + 1) * gl, :], preferred_element_type=F32) + bias
            sk = sink_ref[g * hpg + e]
            m = jnp.maximum(jnp.max(s, axis=1, keepdims=True), sk)
            p = jnp.exp(s - m)
            den = jnp.sum(p, axis=1, keepdims=True) + jnp.exp(sk - m)
            pv = jnp.dot(p.astype(BF16), v_all[:, g * gl:(g + 1) * gl], preferred_element_type=F32)
            o_g = jnp.where(qlane == e, pv / den, o_g)
        outs.append(o_g)
    o_ref[...] = jnp.concatenate(outs, axis=1).astype(o_ref.dtype)


def _attn_b_call(sink, band, qb, kb_t, vb2, *, ctx, blk0):
    b, t, w = qb.shape
    nblk = t // B_BLOCK
    ctx_blocks = ctx // B_BLOCK
    nq = B_QBLOCKS * B_BLOCK
    first = lambda i: i * B_QBLOCKS + blk0
    clamp = lambda n: jnp.clip(n, 0, nblk - 1)
    kspec = lambda d: pl.BlockSpec((None, w, B_BLOCK), lambda bi, i: (bi, 0, clamp(first(i) - 1 + d)))
    vspec = lambda d: pl.BlockSpec((None, B_BLOCK, w), lambda bi, i: (bi, clamp(first(i) - 1 + d), 0))
    nwb = B_QBLOCKS + 2
    return pl.pallas_call(
        functools.partial(_attn_b_kernel, blk0=blk0, ctx_blocks=ctx_blocks, nblk=nblk),
        grid=(b, (nblk - blk0) // B_QBLOCKS),
        in_specs=[pl.BlockSpec(memory_space=pltpu.SMEM),
                  pl.BlockSpec(band.shape, lambda bi, i: (0, 0)),
                  pl.BlockSpec((None, nq, w), lambda bi, i: (bi, i + blk0 // B_QBLOCKS, 0))]
                 + [kspec(d) for d in range(nwb)] + [pl.BlockSpec((None, w, ctx), lambda bi, i: (bi, 0, 0))]
                 + [vspec(d) for d in range(nwb)] + [pl.BlockSpec((None, ctx, w), lambda bi, i: (bi, 0, 0))],
        out_specs=pl.BlockSpec((None, nq, w), lambda bi, i: (bi, i, 0)),
        out_shape=jax.ShapeDtypeStruct((b, (nblk - blk0) * B_BLOCK, w), BF16),
        compiler_params=_params(("parallel", "parallel")),
        name="window_attention",
    )(sink, band, qb, *([kb_t] * (nwb + 1)), *([vb2] * (nwb + 1)))


def _conv_kernel(u_ref, up_ref, un_ref, w_ref, b_ref, xt_o, bn_o, *, tm, ctx_tiles, ntiles):
    i = pl.program_id(1)
    u = u_ref[...]
    has_prev = jnp.logical_and(i != 0, i != ctx_tiles)
    has_next = jnp.logical_and(i != ctx_tiles - 1, i != ntiles - 1)
    up = jnp.where(has_prev, up_ref[...], 0.0)
    un = jnp.where(has_next, un_ref[...], 0.0)
    full = jnp.concatenate([up, u, un], axis=0)
    w = w_ref[...]
    acc = b_ref[...] + w[0:1] * full[HALO - 2:HALO - 2 + tm]
    for k in range(1, C_CONV):
        acc = acc + w[k:k + 1] * full[HALO - 2 + k:HALO - 2 + k + tm]
    y = _silu(acc)
    xt_o[...] = y.T
    bn_o[...] = y[:, C_INNER:C_INNER + C_GN]


def _conv_call(xbc_raw, conv_w, conv_b, *, tm, ctx):
    b, t, ch = xbc_raw.shape
    ntiles = t // tm
    hb = tm // HALO
    nh = t // HALO
    w8 = jnp.zeros((8, ch), F32).at[:C_CONV].set(conv_w)
    return pl.pallas_call(
        functools.partial(_conv_kernel, tm=tm, ctx_tiles=ctx // tm, ntiles=ntiles),
        grid=(b, ntiles),
        in_specs=[pl.BlockSpec((None, tm, ch), lambda bi, i: (bi, i, 0)),
                  pl.BlockSpec((None, HALO, ch), lambda bi, i: (bi, jnp.maximum(i * hb - 1, 0), 0)),
                  pl.BlockSpec((None, HALO, ch), lambda bi, i: (bi, jnp.minimum((i + 1) * hb, nh - 1), 0)),
                  pl.BlockSpec((8, ch), lambda bi, i: (0, 0)),
                  pl.BlockSpec((1, ch), lambda bi, i: (0, 0))],
        out_specs=[pl.BlockSpec((None, ch, tm), lambda bi, i: (bi, 0, i)),
                   pl.BlockSpec((None, tm, C_GN), lambda bi, i: (bi, i, 0))],
        out_shape=[jax.ShapeDtypeStruct((b, ch, t), F32), jax.ShapeDtypeStruct((b, t, C_GN), F32)],
        compiler_params=_params(("parallel", "parallel")),
        name="ssm_conv",
    )(xbc_raw, xbc_raw, xbc_raw, w8, conv_b.reshape(1, ch))


def _softplus(v):
    return jnp.maximum(v, 0.0) + jnp.log1p(jnp.exp(-jnp.abs(v)))


def _split3(v):
    v1 = v.astype(BF16)
    r1 = v - v1.astype(F32)
    v2 = r1.astype(BF16)
    v3 = (r1 - v2.astype(F32)).astype(BF16)
    return v1, v2, v3


def _ssd_prepare(xt_all, bn, dt_raw, par, *, lane0, backward):
    c_t = xt_all[C_INNER + C_GN:]
    dt = _softplus(dt_raw + par[0:1])
    a = dt * par[1:2]
    s_idx = lax.broadcasted_iota(jnp.int32, (C_CHUNK, C_CHUNK), 0)
    l_idx = lax.broadcasted_iota(jnp.int32, (C_CHUNK, C_CHUNK), 1)
    tri = jnp.where(l_idx <= s_idx, 1.0, 0.0).astype(BF16)
    cs3 = jnp.dot(tri, jnp.concatenate(_split3(a), axis=1), preferred_element_type=F32)
    cs = cs3[:, :DT_LANES] + cs3[:, DT_LANES:2 * DT_LANES] + cs3[:, 2 * DT_LANES:]
    key = cs - a if backward else cs
    k1, k2, k3 = _split3(key)
    parts = (k1.astype(F32) + pltpu.roll(k2.astype(F32), 2 * C_HEADS, 1)
             + pltpu.roll(k3.astype(F32), 4 * C_HEADS, 1)).astype(BF16)
    src = lax.broadcasted_iota(jnp.int32, (DT_LANES, C_HEADS * C_CHUNK), 0)
    dst = lax.broadcasted_iota(jnp.int32, (DT_LANES, C_HEADS * C_CHUNK), 1)
    pick = jnp.logical_and(src < 6 * C_HEADS, (src & (2 * C_HEADS - 1)) == lane0 + dst // C_CHUNK)
    colb_all = jnp.dot(parts, jnp.where(pick, 1.0, 0.0).astype(BF16), preferred_element_type=F32)
    return dict(x_t=xt_all[:C_INNER], c_t=c_t, c16=c_t.astype(BF16), bn=bn, dt_t=dt.T, key_t=key.T, cs=cs,
                colb_all=colb_all, mask=(s_idx >= l_idx) if backward else (s_idx <= l_idx))


def _ssd_group(st, g, hs, *, lane0, backward):
    lane_g = lax.broadcasted_iota(jnp.int32, st["bn"].shape, 1) // C_STATE
    row_g = lax.broadcasted_iota(jnp.int32, st["c_t"].shape, 0) // C_STATE
    bn_g = jnp.where(lane_g == g, st["bn"], 0.0).astype(BF16)
    ct_g = jnp.where(row_g == g, st["c_t"], 0.0).astype(BF16)
    cb_t = jnp.dot(bn_g, st["c16"], preferred_element_type=F32)
    y_off = jnp.dot(hs.astype(BF16), ct_g, preferred_element_type=F32)
    y_g, xw_g, hs_new, xdts, m_ts, carries = [], [], [], [], [], []
    for r in range(C_HPG):
        h = g * C_HPG + r
        hl = lane0 + h
        colb = st["colb_all"][:, h * C_CHUNK:(h + 1) * C_CHUNK]
        row = st["key_t"][hl:hl + 1, :]
        tot_h = st["cs"][C_CHUNK - 1:C_CHUNK, hl:hl + 1]
        d = (colb - row) if backward else (row - colb)
        m_ts.append((cb_t * jnp.exp(jnp.where(st["mask"], d, NEG_INF))).astype(BF16))
        xdt = st["x_t"][h * C_HEAD_DIM:(h + 1) * C_HEAD_DIM] * st["dt_t"][hl:hl + 1, :]
        xdts.append(xdt.astype(BF16))
        carries.append(jnp.exp(tot_h - row) if backward else jnp.exp(row))
        w_row = jnp.exp(row) if backward else jnp.exp(tot_h - row)
        xw_g.append((xdt * w_row).astype(BF16))
        hs_new.append(jnp.exp(tot_h) * hs[r * C_HEAD_DIM:(r + 1) * C_HEAD_DIM])
    for r in range(0, C_HPG, 2):
        both = jnp.dot(jnp.concatenate(xdts[r:r + 2], axis=0), jnp.concatenate(m_ts[r:r + 2], axis=1),
                       preferred_element_type=F32)
        for e in range(2):
            y_diag = both[e * C_HEAD_DIM:(e + 1) * C_HEAD_DIM, e * C_CHUNK:(e + 1) * C_CHUNK]
            y_g.append(y_diag + y_off[(r + e) * C_HEAD_DIM:(r + e + 1) * C_HEAD_DIM] * carries[r + e])
    state = jnp.concatenate(hs_new, axis=0) + jnp.dot(jnp.concatenate(xw_g, axis=0), bn_g, preferred_element_type=F32)
    return y_g, state


def _ssd_kernel(par_ref, dsk_ref, xf_ref, bf_ref, dtf_ref, xb_ref, bb_ref, dtb_ref, yf_ref, yb_ref,
                hf_scr, hb_scr, dsk_scr):
    @pl.when(pl.program_id(1) == 0)
    def _():
        hf_scr[...] = jnp.zeros(hf_scr.shape, F32)
        hb_scr[...] = jnp.zeros(hb_scr.shape, F32)
        dsk_scr[...] = jnp.broadcast_to(dsk_ref[...], dsk_scr.shape)

    par = par_ref[...]
    hf = [hf_scr[g] for g in range(C_GROUPS)]
    hb = [hb_scr[g] for g in range(C_GROUPS)]
    for k in range(C_STEP_CHUNKS):
        cf = slice(k * C_CHUNK, (k + 1) * C_CHUNK)
        cb = slice((C_STEP_CHUNKS - 1 - k) * C_CHUNK, (C_STEP_CHUNKS - k) * C_CHUNK)
        sf = _ssd_prepare(xf_ref[:, cf], bf_ref[cf, :], dtf_ref[cf, :], par, lane0=0, backward=False)
        sb = _ssd_prepare(xb_ref[:, cb], bb_ref[cb, :], dtb_ref[cb, :], par, lane0=C_HEADS, backward=True)
        rows_f, rows_b = [], []
        for g in range(C_GROUPS):
            y_g, hf[g] = _ssd_group(sf, g, hf[g], lane0=0, backward=False)
            rows_f += y_g
            y_g, hb[g] = _ssd_group(sb, g, hb[g], lane0=C_HEADS, backward=True)
            rows_b += y_g
        yf_ref[cf, :] = (jnp.concatenate(rows_f, axis=0) + dsk_scr[...] * sf["x_t"]).T.astype(yf_ref.dtype)
        yb_ref[cb, :] = jnp.concatenate(rows_b, axis=0).T.astype(yb_ref.dtype)
    for g in range(C_GROUPS):
        hf_scr[g] = hf[g]
        hb_scr[g] = hb[g]


def _ssd_call(par, dsk_col, xbc_t, bn, dt, *, ctx):
    b, ch, t = xbc_t.shape
    blk = C_STEP_CHUNKS * C_CHUNK
    nblk = t // blk
    ncb = ctx // blk
    bblock = lambda j: jnp.where(j < ncb, ncb - 1 - j, nblk - 1 + ncb - j)
    fwd = lambda bi, j: (bi, j, 0)
    bwd = lambda bi, j: (bi, bblock(j), 0)
    fwd_t = lambda bi, j: (bi, 0, j)
    bwd_t = lambda bi, j: (bi, 0, bblock(j))
    tspec = lambda im: pl.BlockSpec((None, ch, blk), im)
    nspec = lambda w, im: pl.BlockSpec((None, blk, w), im)
    return pl.pallas_call(
        _ssd_kernel,
        grid=(b, nblk),
        in_specs=[pl.BlockSpec((8, DT_LANES), lambda bi, j: (0, 0)),
                  pl.BlockSpec(dsk_col.shape, lambda bi, j: (0, 0)),
                  tspec(fwd_t), nspec(C_GN, fwd), nspec(DT_LANES, fwd),
                  tspec(bwd_t), nspec(C_GN, bwd), nspec(DT_LANES, bwd)],
        out_specs=[nspec(C_INNER, fwd), nspec(C_INNER, bwd)],
        out_shape=[jax.ShapeDtypeStruct((b, t, C_INNER), BF16)] * 2,
        scratch_shapes=[pltpu.VMEM((C_GROUPS, C_HPG * C_HEAD_DIM, C_GN), F32)] * 2
                       + [pltpu.VMEM((C_INNER, C_CHUNK), F32)],
        compiler_params=_params(("parallel", "arbitrary")),
        name="ssd_scan",
    )(par, dsk_col, xbc_t, bn, dt, xbc_t, bn, dt)


def _outproj_kernel(x_ref, mod_ref, oa_ref, ob_ref, yf_ref, yb_ref, z_ref, gn_ref, w_ref, o_ref, *,
                    sub, nsub, sub0, ctx_subs, nbatch):
    m_lat, m_ctx = mod_ref[pl.program_id(0)], mod_ref[nbatch]
    gn = gn_ref[...]
    gw = C_INNER // C_GROUPS
    for k in range(nsub):
        rows = slice(k * sub, (k + 1) * sub)
        m = jnp.where(pl.program_id(1) * nsub + k + sub0 < ctx_subs, m_ctx, m_lat)
        y = (yf_ref[rows, :].astype(F32) + yb_ref[rows, :].astype(F32)) * _silu(z_ref[rows, :].astype(F32))
        oc = []
        for g in range(C_GROUPS):
            yg = y[:, g * gw:(g + 1) * gw]
            ms = jnp.mean(yg * yg, axis=1, keepdims=True)
            oc.append(yg * lax.rsqrt(ms + EPS) * gn[:, g * gw:(g + 1) * gw])
        mix = jnp.concatenate([oa_ref[rows, :], ob_ref[rows, :]] + [v.astype(BF16) for v in oc], axis=1)
        o_ref[rows, :] = x_ref[rows, :] + m[5:6] * jnp.dot(mix, w_ref[...], preferred_element_type=F32)


def _outproj_call(xs, mod, oa, ob, yf, yb, z, gn, w_out, layer, *, tm, ctx_rows, row0, nbatch):
    b, t, d = xs.shape
    assert row0 % tm == 0
    ntiles = (t - row0) // tm
    tile0 = row0 // tm
    tok = lambda w: pl.BlockSpec((None, tm, w), lambda bi, i: (bi, i + tile0, 0))
    tok0 = lambda w: pl.BlockSpec((None, tm, w), lambda bi, i: (bi, i, 0))
    return pl.pallas_call(
        functools.partial(_outproj_kernel, sub=SUB_ROWS, nsub=tm // SUB_ROWS, sub0=row0 // SUB_ROWS,
                          ctx_subs=ctx_rows // SUB_ROWS, nbatch=nbatch),
        grid=(b, ntiles),
        in_specs=[tok(d), pl.BlockSpec(mod.shape, lambda bi, i: (0, 0, 0)),
                  tok0(A_HEADS * A_V), tok0(B_QCOLS), tok(C_INNER), tok(C_INNER), tok(C_INNER),
                  pl.BlockSpec(gn.shape, lambda bi, i: (0, 0)),
                  pl.BlockSpec((None,) + w_out.shape[1:], lambda bi, i: (layer, 0, 0),
                               pipeline_mode=pl.Buffered(1))],
        out_specs=tok0(d),
        out_shape=jax.ShapeDtypeStruct((b, ntiles * tm, d), F32),
        compiler_params=_params(("parallel", "parallel")),
        name="mix_out_proj",
    )(xs, mod, oa, ob, yf, yb, z, gn, w_out)


def _rope_tables(seq, ctx, dim, reps):
    rows = seq // GRID_W
    row = jnp.repeat(jnp.arange(rows, dtype=F32), GRID_W)
    col = jnp.tile(jnp.arange(GRID_W, dtype=F32), rows)
    quarter = dim // 4
    inv_freq = ROPE_BASE ** (-jnp.arange(quarter, dtype=F32) / quarter)
    ar = row[:, None] * inv_freq
    ac = col[:, None] * inv_freq
    ang = jnp.concatenate([ar, ar, ac, ac], axis=-1)
    sign = jnp.where((jnp.arange(dim) % (dim // 2)) < quarter, -1.0, 1.0).astype(F32)
    cos = jnp.concatenate([jnp.ones((ctx, dim), F32), jnp.cos(ang)], axis=0)
    sin = jnp.concatenate([jnp.zeros((ctx, dim), F32), jnp.sin(ang) * sign], axis=0)
    return jnp.tile(cos, (1, reps)), jnp.tile(sin, (1, reps))


def _group_matrix(n, group):
    idx = jnp.arange(n) // group
    return (idx[:, None] == idx[None, :]).astype(BF16)


def kernel(x, c, ctx, c_ctx, w_mod, b_mod, norm_ffn1, ffn1_w13, ffn1_w2, norm_mix, w_in, w_out, qn_a, kn_a, lam_q1, lam_k1, lam_q2, lam_k2, subln_a, qn_b, kn_b, sink_b, conv_w, conv_b, dt_bias, a_log, d_skip, gnorm_c, norm_ffn2, ffn2_w13, ffn2_w2):
    nb, seq, d = x.shape
    nctx = ctx.shape[1]
    t = nctx + seq
    depth = w_mod.shape[0]
    tm = 256
    big = lambda rows: next(c for c in (768, 512, 256) if rows % c == 0)
    tm_all, tm_lat = big(t), big(seq)
    fc = 256
    tq = 256
    tk = next(c for c in (1408, 768, 256) if t % c == 0 and (t > c or c == 256))
    assert nctx == tm == SUB_ROWS and nctx % (C_STEP_CHUNKS * C_CHUNK) == 0 and seq % GRID_W == 0 and nb < 8

    xs = x
    cvec = jnp.zeros((8, d), F32).at[:nb].set(c).at[nb].set(c_ctx)
    mod = _mod_call(cvec, w_mod, b_mod).reshape(depth, 8, N_MOD, d)

    cosa, sina = _rope_tables(seq, nctx, A_QK, ROPE_LANES // A_QK)
    cosb, sinb = _rope_tables(seq, nctx, B_DIM, ROPE_LANES // B_DIM)
    ga = _group_matrix(A_QCOLS, A_QK)
    gb = _group_matrix(B_QCOLS, B_DIM)
    src = jnp.arange(A_QCOLS)
    dst = (src // (2 * A_QK)) * A_HLANES + src % (2 * A_QK)
    kpl = (dst[:, None] == jnp.arange(A_HEADS * A_HLANES)[None, :]).astype(BF16)
    kone = (jnp.arange(A_HEADS * A_HLANES) % A_HLANES == 2 * A_QK).astype(F32).reshape(1, -1)
    dsrc = jnp.arange(B_KCOLS)
    dcol = jnp.arange(B_QCOLS)
    hpg = B_HEADS // B_KV_HEADS
    dup = jnp.logical_and(dcol[None, :] // (hpg * B_DIM) == dsrc[:, None] // B_DIM,
                          dcol[None, :] % B_DIM == dsrc[:, None] % B_DIM).astype(BF16)
    qi = jnp.arange(B_QBLOCKS * B_BLOCK)[:, None]
    kj = jnp.arange((B_QBLOCKS + 2) * B_BLOCK + nctx)[None, :]
    in_band = jnp.logical_and(kj - qi >= 0, kj - qi <= 2 * B_BLOCK)
    band = jnp.where(jnp.logical_or(in_band, kj >= (B_QBLOCKS + 2) * B_BLOCK), 0.0, NEG_INF).astype(F32)
    in_cols = w_in.shape[2]
    in_pad = (-in_cols) % DT_LANES
    w13_1, w2_1 = ffn1_w13.astype(BF16), ffn1_w2.astype(BF16)
    w13_2, w2_2 = ffn2_w13.astype(BF16), ffn2_w2.astype(BF16)
    w_in_p = jnp.pad(w_in, ((0, 0), (0, 0), (0, in_pad))).astype(BF16)
    w_out_b = w_out.astype(BF16)

    for l in range(depth):
        last = l == depth - 1
        lam_init = 0.8 - 0.6 * math.exp(-0.3 * l)
        hn = jnp.stack([jnp.tile(qn_a[l], A_QCOLS // A_QK), jnp.tile(kn_a[l], A_QCOLS // A_QK),
                        jnp.tile(qn_b[l], B_QCOLS // B_DIM), jnp.tile(kn_b[l], B_QCOLS // B_DIM)]).astype(F32)
        hn = jnp.zeros((8, A_QCOLS), F32).at[:4].set(hn)

        xs = _ffn_call(xs, mod[l], norm_ffn1[l], w13_1, w2_1, l, si=0, fc=fc,
                       tm=(SUB_ROWS if l == 0 else tm_all), ctx_rows=nctx, nbatch=nb, ctx_in=(ctx if l == 0 else None))

        qa_t, ka_x, va_t, qb, kb_t, vb2, z, xbc_raw, dt = _inproj_call(
            xs, mod[l], norm_mix[l], w_in_p, l, ga, gb, hn, kpl, kone, dup, dup.T, cosa, sina, cosb, sinb,
            tm=tm_all, ctx_rows=nctx, nbatch=nb)

        lamv = jnp.zeros((8, A_QK), F32).at[:4].set(jnp.stack([lam_q1[l], lam_k1[l], lam_q2[l], lam_k2[l]]))
        oa = _attn_a_call(lamv, subln_a[l].reshape(A_V, 1), qa_t, ka_x, va_t, tq=tq, tk=tk, ctx=nctx,
                          q_tile0=(nctx // tq if last else 0), lam_init=lam_init)

        ob = _attn_b_call(sink_b[l].astype(F32), band, qb, kb_t, vb2, ctx=nctx, blk0=(nctx // B_BLOCK if last else 0))

        xbc_t, bn = _conv_call(xbc_raw, conv_w[l], conv_b[l], tm=tm, ctx=nctx)
        par = (jnp.zeros((8, DT_LANES), F32)
               .at[0, :2 * C_HEADS].set(dt_bias[l].reshape(-1))
               .at[1, :2 * C_HEADS].set(-jnp.exp(a_log[l].astype(F32)).reshape(-1)))
        dsk_col = jnp.repeat(d_skip[l].astype(F32), C_HEAD_DIM).reshape(C_INNER, 1)
        yf, yb = _ssd_call(par, dsk_col, xbc_t, bn, dt, ctx=nctx)

        xs = _outproj_call(xs, mod[l], oa, ob, yf, yb, z, gnorm_c[l].reshape(1, C_INNER), w_out_b, l,
                           tm=(SUB_ROWS if last else tm_all), ctx_rows=nctx, row0=(nctx if last else 0), nbatch=nb)
        xs = _ffn_call(xs, mod[l], norm_ffn2[l], w13_2, w2_2, l, si=6, fc=fc,
                       tm=(tm_lat if last else tm_all), ctx_rows=(0 if last else nctx), nbatch=nb)
    return xs
```

```python
import functools
import math

import jax
import jax.numpy as jnp
from jax import lax
from jax.experimental import pallas as pl
from jax.experimental.pallas import tpu as pltpu

F32 = jnp.float32
BF16 = jnp.bfloat16
NEG_INF = float("-inf")

EPS = 1e-6
ROPE_BASE = 10000.0
GRID_W = 64
ROPE_LANES = 128
N_MOD = 9

A_HEADS, A_QK, A_V = 4, 32, 64
A_QCOLS = A_HEADS * 2 * A_QK
A_SCALE = A_QK ** -0.5
LOG2E = math.log2(math.e)
A_HLANES = 128
A_VROWS = A_V + 16
A_SHIFT_CAP = 48.0
B_HEADS, B_KV_HEADS, B_DIM = 4, 2, 64
B_QCOLS = B_HEADS * B_DIM
B_KCOLS = B_KV_HEADS * B_DIM
B_SCALE = B_DIM ** -0.5
B_BLOCK = 128
B_QBLOCKS = 2
C_HEADS, C_HEAD_DIM, C_GROUPS, C_STATE, C_CONV = 8, 64, 2, 64, 5
C_INNER = C_HEADS * C_HEAD_DIM
C_GN = C_GROUPS * C_STATE
C_XBC = C_INNER + 2 * C_GN
C_CHUNK = 128
C_STEP_CHUNKS = 2
C_HPG = C_HEADS // C_GROUPS
DT_LANES = 128
HALO = 8

MXU_WIDTH = 256
SUB_ROWS = 256
TOKEN_TILES = (768, 512, 256)
KEY_TILES = (1408, 768, 256)

V7X_VMEM_LIMIT = 56 * 1024 * 1024


def _params(sem, vmem=V7X_VMEM_LIMIT):
    return pltpu.CompilerParams(dimension_semantics=sem, vmem_limit_bytes=vmem)


def _rms_mod(x, nw, shift, scale):
    ms = jnp.mean(x * x, axis=-1, keepdims=True)
    y = x * lax.rsqrt(ms + EPS) * nw
    return y * (1.0 + scale) + shift


def _silu(v):
    return v * jax.nn.sigmoid(v)


def _mod_kernel(c_ref, w_ref, b_ref, o_ref):
    sc = _silu(c_ref[...])
    c1 = sc.astype(BF16)
    c2 = (sc - c1.astype(F32)).astype(BF16)
    w = w_ref[...]
    w1 = w.astype(BF16)
    w2 = (w - w1.astype(F32)).astype(BF16)
    dot = lambda a, bb: jnp.dot(a, bb, preferred_element_type=F32)
    o_ref[...] = dot(c1, w1) + (dot(c1, w2) + dot(c2, w1)) + b_ref[...]


def _mod_call(cvec, w_mod, b_mod, tn=1152):
    nl, d, n = w_mod.shape
    return pl.pallas_call(
        _mod_kernel,
        grid=(nl, n // tn),
        in_specs=[pl.BlockSpec((8, d), lambda l, j: (0, 0)),
                  pl.BlockSpec((None, d, tn), lambda l, j: (l, 0, j)),
                  pl.BlockSpec((None, 1, tn), lambda l, j: (l, 0, j))],
        out_specs=pl.BlockSpec((None, 8, tn), lambda l, j: (l, 0, j)),
        out_shape=jax.ShapeDtypeStruct((nl, 8, n), F32),
        compiler_params=_params(("parallel", "parallel")),
        name="adaln_mod",
    )(cvec, w_mod, b_mod.reshape(nl, 1, n))


def _ffn_kernel(*refs, si, fc, ff, sub, nsub, ctx_subs, nbatch, split_input):
    if split_input:
        c_ref, x_ref, mod_ref, nw_ref, w13_ref, w2_ref, o_ref, a_scr = refs
    else:
        x_ref, mod_ref, nw_ref, w13_ref, w2_ref, o_ref, a_scr = refs
    m_lat, m_ctx = mod_ref[pl.program_id(0)], mod_ref[nbatch]
    for k in range(nsub):
        rows = slice(k * sub, (k + 1) * sub)
        is_ctx = pl.program_id(1) * nsub + k < ctx_subs
        x = jnp.where(is_ctx, c_ref[...], x_ref[...]) if split_input else x_ref[rows, :]
        m = jnp.where(is_ctx, m_ctx, m_lat)
        h = _rms_mod(x, nw_ref[...], m[si:si + 1], m[si + 1:si + 2]).astype(BF16)
        for c in range(ff // fc):
            g = jnp.dot(h, w13_ref[:, c * fc:(c + 1) * fc], preferred_element_type=F32)
            u = jnp.dot(h, w13_ref[:, ff + c * fc:ff + (c + 1) * fc], preferred_element_type=F32)
            a_scr[k, :, c * fc:(c + 1) * fc] = (_silu(g) * u).astype(BF16)
        y = jnp.dot(a_scr[k], w2_ref[...], preferred_element_type=F32)
        o_ref[rows, :] = x + (0.5 * m[si + 2:si + 3]) * y


def _ffn_call(xs, mod, nw, w13, w2, layer, *, si, tm, fc, ctx_rows, nbatch, ctx_in=None):
    b, t, d = xs.shape
    ff = w2.shape[1]
    split = ctx_in is not None
    nsub = tm // SUB_ROWS
    ctx_subs = ctx_rows // SUB_ROWS
    assert not split or nsub == 1
    ntiles = t // tm + (ctx_subs if split else 0)
    if split:
        tok_specs = [pl.BlockSpec((None, tm, d), lambda bi, i: (bi, jnp.minimum(i, ctx_subs - 1), 0)),
                     pl.BlockSpec((None, tm, d), lambda bi, i: (bi, jnp.maximum(i - ctx_subs, 0), 0))]
        toks = (ctx_in, xs)
    else:
        tok_specs = [pl.BlockSpec((None, tm, d), lambda bi, i: (bi, i, 0))]
        toks = (xs,)
    return pl.pallas_call(
        functools.partial(_ffn_kernel, si=si, fc=fc, ff=ff, sub=SUB_ROWS, nsub=nsub, ctx_subs=ctx_subs,
                          nbatch=nbatch, split_input=split),
        grid=(b, ntiles),
        in_specs=tok_specs + [pl.BlockSpec(mod.shape, lambda bi, i: (0, 0, 0)),
                              pl.BlockSpec((1, d), lambda bi, i: (0, 0)),
                              pl.BlockSpec((None,) + w13.shape[1:], lambda bi, i: (layer, 0, 0),
                                           pipeline_mode=pl.Buffered(1)),
                              pl.BlockSpec((None,) + w2.shape[1:], lambda bi, i: (layer, 0, 0),
                                           pipeline_mode=pl.Buffered(1))],
        out_specs=pl.BlockSpec((None, tm, d), lambda bi, i: (bi, i, 0)),
        out_shape=jax.ShapeDtypeStruct((b, ntiles * tm, d), F32),
        scratch_shapes=[pltpu.VMEM((nsub, SUB_ROWS, ff), BF16)],
        compiler_params=_params(("parallel", "parallel")),
        name="swiglu_half",
    )(*toks, mod, nw.reshape(1, d), w13, w2)


def _rope(v, cos, sin_signed, quarter):
    n = v.shape[-1]
    lane = lax.broadcasted_iota(jnp.int32, v.shape, 1)
    first = (lane & (2 * quarter - 1)) < quarter
    vr = jnp.where(first, pltpu.roll(v, n - quarter, 1), pltpu.roll(v, quarter, 1))
    return v * cos + vr * sin_signed


def _group_norm(v, gmat, inv_n, w):
    sq = v * v
    hi = sq.astype(BF16)
    lo = (sq - hi.astype(F32)).astype(BF16)
    ms = (jnp.dot(hi, gmat, preferred_element_type=F32) + jnp.dot(lo, gmat, preferred_element_type=F32)) * inv_n
    return v * lax.rsqrt(ms + EPS) * w


def _inproj_kernel(x_ref, mod_ref, nw_ref, w_ref, ga_ref, gb_ref, hn_ref, kpl_ref, kone_ref, dup_ref, dupt_ref,
                   cosa_ref, sina_ref, cosb_ref, sinb_ref, qa_o, ka_o, va_o, qb_o, kb_o, vb_o, z_o, xbc_o, dt_o, *,
                   sub, nsub, ctx_subs, nbatch):
    m_lat, m_ctx = mod_ref[pl.program_id(0)], mod_ref[nbatch]
    hn = hn_ref[...]
    ga, gb = ga_ref[...], gb_ref[...]
    twice = lambda a: jnp.concatenate([a, a], axis=1)
    for k in range(nsub):
        rows = slice(k * sub, (k + 1) * sub)
        m = jnp.where(pl.program_id(1) * nsub + k < ctx_subs, m_ctx, m_lat)
        h = _rms_mod(x_ref[rows, :], nw_ref[...], m[3:4], m[4:5]).astype(BF16)
        p = jnp.dot(h, w_ref[...], preferred_element_type=F32)
        cosa, sina = twice(cosa_ref[rows, :]), twice(sina_ref[rows, :])
        cosb, sinb = twice(cosb_ref[rows, :]), twice(sinb_ref[rows, :])
        o = 0
        qa = _group_norm(p[:, o:o + A_QCOLS], ga, 1.0 / A_QK, hn[0:1]); o += A_QCOLS
        ka = _group_norm(p[:, o:o + A_QCOLS], ga, 1.0 / A_QK, hn[1:2]); o += A_QCOLS
        qa_o[:, rows] = (_rope(qa, cosa, sina, A_QK // 4) * (A_SCALE * LOG2E)).T.astype(BF16)
        ka16 = _rope(ka, cosa, sina, A_QK // 4).astype(BF16)
        ka_o[rows, :] = (jnp.dot(ka16, kpl_ref[...], preferred_element_type=F32) + kone_ref[...]).astype(BF16)
        va_o[:, rows] = p[:, o:o + A_HEADS * A_V].T.astype(BF16); o += A_HEADS * A_V
        qb = _group_norm(p[:, o:o + B_QCOLS], gb, 1.0 / B_DIM, hn[2:3]); o += B_QCOLS
        kb = _group_norm(p[:, o:o + B_KCOLS], gb[:B_KCOLS, :B_KCOLS], 1.0 / B_DIM, hn[3:4, :B_KCOLS]); o += B_KCOLS
        qb_o[rows, :] = (_rope(qb, cosb, sinb, B_DIM // 4) * B_SCALE).astype(BF16)
        kb16_t = _rope(kb, cosb[:, :B_KCOLS], sinb[:, :B_KCOLS], B_DIM // 4).T.astype(BF16)
        kb_o[:, rows] = jnp.dot(dupt_ref[...], kb16_t, preferred_element_type=F32).astype(BF16)
        vb_o[rows, :] = jnp.dot(p[:, o:o + B_KCOLS].astype(BF16), dup_ref[...],
                                preferred_element_type=F32).astype(BF16); o += B_KCOLS
        z_o[rows, :] = p[:, o:o + C_INNER].astype(BF16); o += C_INNER
        xbc_o[rows, :] = p[:, o:o + C_XBC]; o += C_XBC
        dt_o[rows, :] = p[:, o:o + DT_LANES]


def _inproj_call(xs, mod, nw, w_in_p, layer, ga, gb, hn, kpl, kone, dup, dupt, cosa, sina, cosb, sinb, *,
                 tm, ctx_rows, nbatch):
    b, t, d = xs.shape
    ntiles = t // tm
    tok = lambda w: pl.BlockSpec((None, tm, w), lambda bi, i: (bi, i, 0))
    tab = lambda w: pl.BlockSpec((tm, w), lambda bi, i: (i, 0))
    full = lambda a: pl.BlockSpec(a.shape, lambda bi, i: (0,) * a.ndim)
    widths = (A_QCOLS, A_HEADS * A_HLANES, A_HEADS * A_V, B_QCOLS, B_QCOLS, B_QCOLS, C_INNER, C_XBC, DT_LANES)
    transposed = (0, 2, 4)
    dtypes = (BF16,) * 7 + (F32,) * 2
    return pl.pallas_call(
        functools.partial(_inproj_kernel, sub=SUB_ROWS, nsub=tm // SUB_ROWS, ctx_subs=ctx_rows // SUB_ROWS,
                          nbatch=nbatch),
        grid=(b, ntiles),
        in_specs=[tok(d), full(mod), pl.BlockSpec((1, d), lambda bi, i: (0, 0)),
                  pl.BlockSpec((None,) + w_in_p.shape[1:], lambda bi, i: (layer, 0, 0), pipeline_mode=pl.Buffered(1)),
                  full(ga), full(gb), full(hn), full(kpl), full(kone), full(dup), full(dupt)] + [tab(ROPE_LANES)] * 4,
        out_specs=[pl.BlockSpec((None, w, tm), lambda bi, i: (bi, 0, i)) if k in transposed else tok(w)
                   for k, w in enumerate(widths)],
        out_shape=[jax.ShapeDtypeStruct((b, w, t) if k in transposed else (b, t, w), dt)
                   for k, (w, dt) in enumerate(zip(widths, dtypes))],
        compiler_params=_params(("parallel", "parallel")),
        name="in_proj_heads",
    )(xs, mod, nw.reshape(1, d), w_in_p, ga, gb, hn, kpl, kone, dup, dupt, cosa, sina, cosb, sinb)


def _attn_a_kernel(lamv_ref, subln_ref, qt_ref, kx_ref, vt_ref, o_ref, kmax_scr, *,
                   tq, tk, ctx, nk, q_tile0, lam_init):
    step = pl.program_id(1)

    @pl.when(step == 0)
    def _():
        r = lax.broadcasted_iota(jnp.int32, (A_HLANES, A_HLANES), 0)
        c = lax.broadcasted_iota(jnp.int32, (A_HLANES, A_HLANES), 1)
        in_comp = jnp.logical_and(r >= c * A_QK, r < (c + 1) * A_QK)
        sel = jnp.where(jnp.logical_and(in_comp, c < 2), 1.0, 0.0).astype(BF16)
        for h in range(A_HEADS):
            kk = kx_ref[:, h * A_HLANES:(h + 1) * A_HLANES].astype(F32)
            ksq = jnp.dot((kk * kk).astype(BF16), sel, preferred_element_type=F32)
            kmax_scr[h:h + 1, :] = jnp.sqrt(jnp.max(ksq, axis=0, keepdims=True))

    lv = lamv_ref[...]
    lam = (jnp.exp(jnp.sum(lv[0:1] * lv[1:2], axis=1, keepdims=True))
           - jnp.exp(jnp.sum(lv[2:3] * lv[3:4], axis=1, keepdims=True)) + lam_init)
    qt = qt_ref[...].astype(F32)
    qcs = [qt[hc * A_QK:(hc + 1) * A_QK, :] for hc in range(2 * A_HEADS)]
    shifts = [jnp.sqrt(jnp.sum(qc * qc, axis=0, keepdims=True)) * kmax_scr[hc // 2:hc // 2 + 1, hc % 2:hc % 2 + 1]
              for hc, qc in enumerate(qcs)]
    shift_max = jnp.max(functools.reduce(jnp.maximum, shifts))

    def q_ext(h, shifted):
        z = jnp.zeros((A_QK, tq), F32)
        top = jnp.concatenate([qcs[2 * h], z], axis=1)
        mid = jnp.concatenate([z, qcs[2 * h + 1]], axis=1)
        row = lax.broadcasted_iota(jnp.int32, (A_HLANES - 2 * A_QK, 2 * tq), 0)
        if shifted:
            bot = jnp.where(row == 0, -jnp.concatenate([shifts[2 * h], shifts[2 * h + 1]], axis=1), 0.0)
        else:
            bot = jnp.zeros(row.shape, F32)
        return jnp.concatenate([top, mid, bot], axis=0).astype(BF16)

    def scores(qx, h, start, size):
        return jnp.dot(kx_ref[pl.ds(start, size), h * A_HLANES:(h + 1) * A_HLANES], qx, preferred_element_type=F32)

    def v_ext(h, start, size):
        ones = jnp.ones((A_VROWS - A_V, size), BF16)
        return jnp.concatenate([vt_ref[h * A_V:(h + 1) * A_V, pl.ds(start, size)], ones], axis=0)

    def finish(accs):
        rows = []
        for acc in accs:
            o = (acc[:A_V, :tq] / acc[A_V:A_V + 1, :tq]) - lam * (acc[:A_V, tq:] / acc[A_V:A_V + 1, tq:])
            ms = jnp.mean(o * o, axis=0, keepdims=True)
            rows.append(o * lax.rsqrt(ms + EPS) * subln_ref[...] * (1.0 - lam_init))
        o_ref[...] = jnp.concatenate(rows, axis=0).T.astype(o_ref.dtype)

    def attend(shifted):
        qx = [q_ext(h, shifted) for h in range(A_HEADS)]

        def head_step(h, start, carry):
            s = scores(qx[h], h, start, tk)
            ve = v_ext(h, start, tk)
            if shifted:
                return carry + jnp.dot(ve, jnp.exp2(s).astype(BF16), preferred_element_type=F32)
            m_run, acc = carry
            m_new = jnp.maximum(m_run, jnp.max(s, axis=0, keepdims=True))
            p = jnp.exp2(s - m_new).astype(BF16)
            return m_new, jnp.exp2(m_run - m_new) * acc + jnp.dot(ve, p, preferred_element_type=F32)

        def body(j, carries):
            start = pl.multiple_of(j * tk, tk)
            return tuple(head_step(h, start, carries[h]) for h in range(A_HEADS))

        acc0 = jnp.zeros((A_VROWS, 2 * tq), F32)
        if shifted:
            accs = lax.fori_loop(0, nk, body, (acc0,) * A_HEADS, unroll=True)
        else:
            init = (jnp.full((1, 2 * tq), NEG_INF, F32), acc0)
            accs = [c[1] for c in lax.fori_loop(0, nk, body, (init,) * A_HEADS)]
        finish(accs)

    def attend_ctx():
        accs = []
        for h in range(A_HEADS):
            s = scores(q_ext(h, False), h, 0, ctx)
            p = jnp.exp2(s - jnp.max(s, axis=0, keepdims=True)).astype(BF16)
            accs.append(jnp.dot(v_ext(h, 0, ctx), p, preferred_element_type=F32))
        finish(accs)

    safe = shift_max < A_SHIFT_CAP
    if q_tile0 * tq < ctx:
        is_ctx = (step + q_tile0) * tq < ctx
        pl.when(is_ctx)(attend_ctx)
        pl.when(jnp.logical_and(jnp.logical_not(is_ctx), safe))(lambda: attend(True))
        pl.when(jnp.logical_and(jnp.logical_not(is_ctx), jnp.logical_not(safe)))(lambda: attend(False))
    else:
        pl.when(safe)(lambda: attend(True))
        pl.when(jnp.logical_not(safe))(lambda: attend(False))


def _attn_a_call(lamv, subln_col, qa_t, ka_x, va_t, *, tq, tk, ctx, q_tile0, lam_init):
    b, w, t = qa_t.shape
    nq = t // tq - q_tile0
    return pl.pallas_call(
        functools.partial(_attn_a_kernel, tq=tq, tk=tk, ctx=ctx, nk=t // tk, q_tile0=q_tile0, lam_init=lam_init),
        grid=(b, nq),
        in_specs=[pl.BlockSpec(lamv.shape, lambda bi, qi: (0, 0)),
                  pl.BlockSpec(subln_col.shape, lambda bi, qi: (0, 0)),
                  pl.BlockSpec((None, w, tq), lambda bi, qi: (bi, 0, qi + q_tile0)),
                  pl.BlockSpec((None, t, A_HEADS * A_HLANES), lambda bi, qi: (bi, 0, 0)),
                  pl.BlockSpec((None, w, t), lambda bi, qi: (bi, 0, 0))],
        out_specs=pl.BlockSpec((None, tq, w), lambda bi, qi: (bi, qi, 0)),
        out_shape=jax.ShapeDtypeStruct((b, nq * tq, w), BF16),
        scratch_shapes=[pltpu.VMEM((8, A_HLANES), F32)],
        compiler_params=_params(("parallel", "arbitrary")),
        name="diff_attention",
    )(lamv, subln_col, qa_t, ka_x, va_t)


def _attn_b_kernel(sink_ref, band_ref, q_ref, kp_ref, k0_ref, k1_ref, kn_ref, kx_ref,
                   vp_ref, v0_ref, v1_ref, vn_ref, vx_ref, o_ref, *, blk0, ctx_blocks, nblk):
    n0 = pl.program_id(1) * B_QBLOCKS + blk0
    nq = B_QBLOCKS * B_BLOCK
    k_all = jnp.concatenate([kp_ref[...], k0_ref[...], k1_ref[...], kn_ref[...], kx_ref[...]], axis=1)
    v_all = jnp.concatenate([vp_ref[...], v0_ref[...], v1_ref[...], vn_ref[...], vx_ref[...]], axis=0)
    lane = lax.broadcasted_iota(jnp.int32, (1, k_all.shape[1]), 1)
    lat = n0 >= ctx_blocks
    pen = jnp.zeros(lane.shape, F32)
    for d in range(B_QBLOCKS + 2):
        kb = n0 - 1 + d
        ok = jnp.logical_and(lat, jnp.logical_and(kb >= ctx_blocks, kb <= nblk - 1))
        in_blk = jnp.logical_and(lane >= d * B_BLOCK, lane < (d + 1) * B_BLOCK)
        pen = jnp.where(in_blk, jnp.where(ok, 0.0, NEG_INF), pen)
    bias = band_ref[...] + pen
    hpg = B_HEADS // B_KV_HEADS
    gl = hpg * B_DIM
    qlane = lax.broadcasted_iota(jnp.int32, (nq, gl), 1) // B_DIM
    outs = []
    for g in range(B_KV_HEADS):
        q_g = q_ref[:, g * gl:(g + 1) * gl]
        o_g = jnp.zeros((nq, gl), F32)
        for e in range(hpg):
            q_e = jnp.where(qlane == e, q_g, jnp.zeros_like(q_g))
            s = jnp.dot(q_e, k_all[g * gl:(g + 1) * gl, :], preferred_element_type=F32) + bias
            sk = sink_ref[g * hpg + e]
            m = jnp.maximum(jnp.max(s, axis=1, keepdims=True), sk)
            p = jnp.exp(s - m)
            den = jnp.sum(p, axis=1, keepdims=True) + jnp.exp(sk - m)
            pv = jnp.dot(p.astype(BF16), v_all[:, g * gl:(g + 1) * gl], preferred_element_type=F32)
            o_g = jnp.where(qlane == e, pv / den, o_g)
        outs.append(o_g)
    o_ref[...] = jnp.concatenate(outs, axis=1).astype(o_ref.dtype)


def _attn_b_call(sink, band, qb, kb_t, vb2, *, ctx, blk0):
    b, t, w = qb.shape
    nblk = t // B_BLOCK
    ctx_blocks = ctx // B_BLOCK
    nq = B_QBLOCKS * B_BLOCK
    first = lambda i: i * B_QBLOCKS + blk0
    clamp = lambda n: jnp.clip(n, 0, nblk - 1)
    kspec = lambda d: pl.BlockSpec((None, w, B_BLOCK), lambda bi, i: (bi, 0, clamp(first(i) - 1 + d)))
    vspec = lambda d: pl.BlockSpec((None, B_BLOCK, w), lambda bi, i: (bi, clamp(first(i) - 1 + d), 0))
    nwb = B_QBLOCKS + 2
    return pl.pallas_call(
        functools.partial(_attn_b_kernel, blk0=blk0, ctx_blocks=ctx_blocks, nblk=nblk),
        grid=(b, (nblk - blk0) // B_QBLOCKS),
        in_specs=[pl.BlockSpec(memory_space=pltpu.SMEM),
                  pl.BlockSpec(band.shape, lambda bi, i: (0, 0)),
                  pl.BlockSpec((None, nq, w), lambda bi, i: (bi, i + blk0 // B_QBLOCKS, 0))]
                 + [kspec(d) for d in range(nwb)] + [pl.BlockSpec((None, w, ctx), lambda bi, i: (bi, 0, 0))]
                 + [vspec(d) for d in range(nwb)] + [pl.BlockSpec((None, ctx, w), lambda bi, i: (bi, 0, 0))],
        out_specs=pl.BlockSpec((None, nq, w), lambda bi, i: (bi, i, 0)),
        out_shape=jax.ShapeDtypeStruct((b, (nblk - blk0) * B_BLOCK, w), BF16),
        compiler_params=_params(("parallel", "parallel")),
        name="window_attention",
    )(sink, band, qb, *([kb_t] * (nwb + 1)), *([vb2] * (nwb + 1)))


def _conv_kernel(u_ref, up_ref, un_ref, w_ref, b_ref, xt_o, bn_o, *, tm, ctx_tiles, ntiles):
    i = pl.program_id(1)
    u = u_ref[...]
    has_prev = jnp.logical_and(i != 0, i != ctx_tiles)
    has_next = jnp.logical_and(i != ctx_tiles - 1, i != ntiles - 1)
    up = jnp.where(has_prev, up_ref[...], 0.0)
    un = jnp.where(has_next, un_ref[...], 0.0)
    full = jnp.concatenate([up, u, un], axis=0)
    w = w_ref[...]
    acc = b_ref[...] + w[0:1] * full[HALO - 2:HALO - 2 + tm]
    for k in range(1, C_CONV):
        acc = acc + w[k:k + 1] * full[HALO - 2 + k:HALO - 2 + k + tm]
    y = _silu(acc)
    xt_o[...] = y.T
    bn_o[...] = y[:, C_INNER:C_INNER + C_GN]


def _conv_call(xbc_raw, conv_w, conv_b, *, tm, ctx):
    b, t, ch = xbc_raw.shape
    ntiles = t // tm
    hb = tm // HALO
    nh = t // HALO
    w8 = jnp.zeros((8, ch), F32).at[:C_CONV].set(conv_w)
    return pl.pallas_call(
        functools.partial(_conv_kernel, tm=tm, ctx_tiles=ctx // tm, ntiles=ntiles),
        grid=(b, ntiles),
        in_specs=[pl.BlockSpec((None, tm, ch), lambda bi, i: (bi, i, 0)),
                  pl.BlockSpec((None, HALO, ch), lambda bi, i: (bi, jnp.maximum(i * hb - 1, 0), 0)),
                  pl.BlockSpec((None, HALO, ch), lambda bi, i: (bi, jnp.minimum((i + 1) * hb, nh - 1), 0)),
                  pl.BlockSpec((8, ch), lambda bi, i: (0, 0)),
                  pl.BlockSpec((1, ch), lambda bi, i: (0, 0))],
        out_specs=[pl.BlockSpec((None, ch, tm), lambda bi, i: (bi, 0, i)),
                   pl.BlockSpec((None, tm, C_GN), lambda bi, i: (bi, i, 0))],
        out_shape=[jax.ShapeDtypeStruct((b, ch, t), F32), jax.ShapeDtypeStruct((b, t, C_GN), F32)],
        compiler_params=_params(("parallel", "parallel")),
        name="ssm_conv",
    )(xbc_raw, xbc_raw, xbc_raw, w8, conv_b.reshape(1, ch))


def _softplus(v):
    return jnp.maximum(v, 0.0) + jnp.log1p(jnp.exp(-jnp.abs(v)))


def _split3(v):
    v1 = v.astype(BF16)
    r1 = v - v1.astype(F32)
    v2 = r1.astype(BF16)
    v3 = (r1 - v2.astype(F32)).astype(BF16)
    return v1, v2, v3


def _ssd_prepare(xt_all, bn, dt_raw, par, *, lane0, backward):
    c_t = xt_all[C_INNER + C_GN:]
    dt = _softplus(dt_raw + par[0:1])
    a = dt * par[1:2]
    s_idx = lax.broadcasted_iota(jnp.int32, (C_CHUNK, C_CHUNK), 0)
    l_idx = lax.broadcasted_iota(jnp.int32, (C_CHUNK, C_CHUNK), 1)
    tri = jnp.where(l_idx <= s_idx, 1.0, 0.0).astype(BF16)
    cs3 = jnp.dot(tri, jnp.concatenate(_split3(a), axis=1), preferred_element_type=F32)
    cs = cs3[:, :DT_LANES] + cs3[:, DT_LANES:2 * DT_LANES] + cs3[:, 2 * DT_LANES:]
    key = cs - a if backward else cs
    k1, k2, k3 = _split3(key)
    parts = (k1.astype(F32) + pltpu.roll(k2.astype(F32), 2 * C_HEADS, 1)
             + pltpu.roll(k3.astype(F32), 4 * C_HEADS, 1)).astype(BF16)
    src = lax.broadcasted_iota(jnp.int32, (DT_LANES, C_HEADS * C_CHUNK), 0)
    dst = lax.broadcasted_iota(jnp.int32, (DT_LANES, C_HEADS * C_CHUNK), 1)
    pick = jnp.logical_and(src < 6 * C_HEADS, (src & (2 * C_HEADS - 1)) == lane0 + dst // C_CHUNK)
    colb_all = jnp.dot(parts, jnp.where(pick, 1.0, 0.0).astype(BF16), preferred_element_type=F32)
    return dict(x_t=xt_all[:C_INNER], c_t=c_t, c16=c_t.astype(BF16), bn=bn, dt_t=dt.T, key_t=key.T, cs=cs,
                colb_all=colb_all, mask=(s_idx >= l_idx) if backward else (s_idx <= l_idx))


def _ssd_group(st, g, hs, *, lane0, backward):
    lane_g = lax.broadcasted_iota(jnp.int32, st["bn"].shape, 1) // C_STATE
    row_g = lax.broadcasted_iota(jnp.int32, st["c_t"].shape, 0) // C_STATE
    bn_g = jnp.where(lane_g == g, st["bn"], 0.0).astype(BF16)
    ct_g = jnp.where(row_g == g, st["c_t"], 0.0).astype(BF16)
    cb_t = jnp.dot(bn_g, st["c16"], preferred_element_type=F32)
    y_off = jnp.dot(hs.astype(BF16), ct_g, preferred_element_type=F32)
    y_g, xw_g, hs_new, xdts, m_ts, carries = [], [], [], [], [], []
    for r in range(C_HPG):
        h = g * C_HPG + r
        hl = lane0 + h
        colb = st["colb_all"][:, h * C_CHUNK:(h + 1) * C_CHUNK]
        row = st["key_t"][hl:hl + 1, :]
        tot_h = st["cs"][C_CHUNK - 1:C_CHUNK, hl:hl + 1]
        d = (colb - row) if backward else (row - colb)
        m_ts.append((cb_t * jnp.exp(jnp.where(st["mask"], d, NEG_INF))).astype(BF16))
        xdt = st["x_t"][h * C_HEAD_DIM:(h + 1) * C_HEAD_DIM] * st["dt_t"][hl:hl + 1, :]
        xdts.append(xdt.astype(BF16))
        carries.append(jnp.exp(tot_h - row) if backward else jnp.exp(row))
        w_row = jnp.exp(row) if backward else jnp.exp(tot_h - row)
        xw_g.append((xdt * w_row).astype(BF16))
        hs_new.append(jnp.exp(tot_h) * hs[r * C_HEAD_DIM:(r + 1) * C_HEAD_DIM])
    for r in range(0, C_HPG, 2):
        both = jnp.dot(jnp.concatenate(xdts[r:r + 2], axis=0), jnp.concatenate(m_ts[r:r + 2], axis=1),
                       preferred_element_type=F32)
        for e in range(2):
            y_diag = both[e * C_HEAD_DIM:(e + 1) * C_HEAD_DIM, e * C_CHUNK:(e + 1) * C_CHUNK]
            y_g.append(y_diag + y_off[(r + e) * C_HEAD_DIM:(r + e + 1) * C_HEAD_DIM] * carries[r + e])
    state = jnp.concatenate(hs_new, axis=0) + jnp.dot(jnp.concatenate(xw_g, axis=0), bn_g, preferred_element_type=F32)
    return y_g, state


def _ssd_kernel(par_ref, dsk_ref, xf_ref, bf_ref, dtf_ref, xb_ref, bb_ref, dtb_ref, yf_ref, yb_ref,
                hf_scr, hb_scr, dsk_scr):
    @pl.when(pl.program_id(1) == 0)
    def _():
        hf_scr[...] = jnp.zeros(hf_scr.shape, F32)
        hb_scr[...] = jnp.zeros(hb_scr.shape, F32)
        dsk_scr[...] = jnp.broadcast_to(dsk_ref[...], dsk_scr.shape)

    par = par_ref[...]
    hf = [hf_scr[g] for g in range(C_GROUPS)]
    hb = [hb_scr[g] for g in range(C_GROUPS)]
    cfs = [slice(k * C_CHUNK, (k + 1) * C_CHUNK) for k in range(C_STEP_CHUNKS)]
    cbs = cfs[::-1]
    sfs = [_ssd_prepare(xf_ref[:, c], bf_ref[c, :], dtf_ref[c, :], par, lane0=0, backward=False) for c in cfs]
    sbs = [_ssd_prepare(xb_ref[:, c], bb_ref[c, :], dtb_ref[c, :], par, lane0=C_HEADS, backward=True) for c in cbs]
    for cf, cb, sf, sb in zip(cfs, cbs, sfs, sbs):
        rows_f, rows_b = [], []
        for g in range(C_GROUPS):
            y_g, hf[g] = _ssd_group(sf, g, hf[g], lane0=0, backward=False)
            rows_f += y_g
            y_g, hb[g] = _ssd_group(sb, g, hb[g], lane0=C_HEADS, backward=True)
            rows_b += y_g
        yf_ref[cf, :] = (jnp.concatenate(rows_f, axis=0) + dsk_scr[...] * sf["x_t"]).T.astype(yf_ref.dtype)
        yb_ref[cb, :] = jnp.concatenate(rows_b, axis=0).T.astype(yb_ref.dtype)
    for g in range(C_GROUPS):
        hf_scr[g] = hf[g]
        hb_scr[g] = hb[g]


def _ssd_call(par, dsk_col, xbc_t, bn, dt, *, ctx):
    b, ch, t = xbc_t.shape
    blk = C_STEP_CHUNKS * C_CHUNK
    nblk = t // blk
    ncb = ctx // blk
    bblock = lambda j: jnp.where(j < ncb, ncb - 1 - j, nblk - 1 + ncb - j)
    fwd = lambda bi, j: (bi, j, 0)
    bwd = lambda bi, j: (bi, bblock(j), 0)
    fwd_t = lambda bi, j: (bi, 0, j)
    bwd_t = lambda bi, j: (bi, 0, bblock(j))
    tspec = lambda im: pl.BlockSpec((None, ch, blk), im)
    nspec = lambda w, im: pl.BlockSpec((None, blk, w), im)
    return pl.pallas_call(
        _ssd_kernel,
        grid=(b, nblk),
        in_specs=[pl.BlockSpec((8, DT_LANES), lambda bi, j: (0, 0)),
                  pl.BlockSpec(dsk_col.shape, lambda bi, j: (0, 0)),
                  tspec(fwd_t), nspec(C_GN, fwd), nspec(DT_LANES, fwd),
                  tspec(bwd_t), nspec(C_GN, bwd), nspec(DT_LANES, bwd)],
        out_specs=[nspec(C_INNER, fwd), nspec(C_INNER, bwd)],
        out_shape=[jax.ShapeDtypeStruct((b, t, C_INNER), BF16)] * 2,
        scratch_shapes=[pltpu.VMEM((C_GROUPS, C_HPG * C_HEAD_DIM, C_GN), F32)] * 2
                       + [pltpu.VMEM((C_INNER, C_CHUNK), F32)],
        compiler_params=_params(("parallel", "arbitrary")),
        name="ssd_scan",
    )(par, dsk_col, xbc_t, bn, dt, xbc_t, bn, dt)


def _outproj_kernel(x_ref, mod_ref, oa_ref, ob_ref, yf_ref, yb_ref, z_ref, gn_ref, w_ref, o_ref, *,
                    sub, nsub, sub0, ctx_subs, nbatch):
    m_lat, m_ctx = mod_ref[pl.program_id(0)], mod_ref[nbatch]
    gn = gn_ref[...]
    gw = C_INNER // C_GROUPS
    for k in range(nsub):
        rows = slice(k * sub, (k + 1) * sub)
        m = jnp.where(pl.program_id(1) * nsub + k + sub0 < ctx_subs, m_ctx, m_lat)
        y = (yf_ref[rows, :].astype(F32) + yb_ref[rows, :].astype(F32)) * _silu(z_ref[rows, :].astype(F32))
        oc = []
        for g in range(C_GROUPS):
            yg = y[:, g * gw:(g + 1) * gw]
            ms = jnp.mean(yg * yg, axis=1, keepdims=True)
            oc.append(yg * lax.rsqrt(ms + EPS) * gn[:, g * gw:(g + 1) * gw])
        mix = jnp.concatenate([oa_ref[rows, :], ob_ref[rows, :]] + [v.astype(BF16) for v in oc], axis=1)
        o_ref[rows, :] = x_ref[rows, :] + m[5:6] * jnp.dot(mix, w_ref[...], preferred_element_type=F32)


def _outproj_call(xs, mod, oa, ob, yf, yb, z, gn, w_out, layer, *, tm, ctx_rows, row0, nbatch):
    b, t, d = xs.shape
    assert row0 % tm == 0
    ntiles = (t - row0) // tm
    tile0 = row0 // tm
    tok = lambda w: pl.BlockSpec((None, tm, w), lambda bi, i: (bi, i + tile0, 0))
    tok0 = lambda w: pl.BlockSpec((None, tm, w), lambda bi, i: (bi, i, 0))
    return pl.pallas_call(
        functools.partial(_outproj_kernel, sub=SUB_ROWS, nsub=tm // SUB_ROWS, sub0=row0 // SUB_ROWS,
                          ctx_subs=ctx_rows // SUB_ROWS, nbatch=nbatch),
        grid=(b, ntiles),
        in_specs=[tok(d), pl.BlockSpec(mod.shape, lambda bi, i: (0, 0, 0)),
                  tok0(A_HEADS * A_V), tok0(B_QCOLS), tok(C_INNER), tok(C_INNER), tok(C_INNER),
                  pl.BlockSpec(gn.shape, lambda bi, i: (0, 0)),
                  pl.BlockSpec((None,) + w_out.shape[1:], lambda bi, i: (layer, 0, 0),
                               pipeline_mode=pl.Buffered(1))],
        out_specs=tok0(d),
        out_shape=jax.ShapeDtypeStruct((b, ntiles * tm, d), F32),
        compiler_params=_params(("parallel", "parallel")),
        name="mix_out_proj",
    )(xs, mod, oa, ob, yf, yb, z, gn, w_out)


def _rope_tables(seq, ctx, dim, reps):
    rows = seq // GRID_W
    row = jnp.repeat(jnp.arange(rows, dtype=F32), GRID_W)
    col = jnp.tile(jnp.arange(GRID_W, dtype=F32), rows)
    quarter = dim // 4
    inv_freq = ROPE_BASE ** (-jnp.arange(quarter, dtype=F32) / quarter)
    ar = row[:, None] * inv_freq
    ac = col[:, None] * inv_freq
    ang = jnp.concatenate([ar, ar, ac, ac], axis=-1)
    sign = jnp.where((jnp.arange(dim) % (dim // 2)) < quarter, -1.0, 1.0).astype(F32)
    cos = jnp.concatenate([jnp.ones((ctx, dim), F32), jnp.cos(ang)], axis=0)
    sin = jnp.concatenate([jnp.zeros((ctx, dim), F32), jnp.sin(ang) * sign], axis=0)
    return jnp.tile(cos, (1, reps)), jnp.tile(sin, (1, reps))


def _group_matrix(n, group):
    idx = jnp.arange(n) // group
    return (idx[:, None] == idx[None, :]).astype(BF16)


def _tiling(nctx, seq):
    t = nctx + seq
    token_tile = lambda rows: next(c for c in TOKEN_TILES if rows % c == 0)
    key_tile = next(c for c in KEY_TILES if t % c == 0 and (t > c or c == KEY_TILES[-1]))
    return SUB_ROWS, token_tile(t), token_tile(seq), MXU_WIDTH, SUB_ROWS, key_tile


def kernel(x, c, ctx, c_ctx, w_mod, b_mod, norm_ffn1, ffn1_w13, ffn1_w2, norm_mix, w_in, w_out, qn_a, kn_a, lam_q1, lam_k1, lam_q2, lam_k2, subln_a, qn_b, kn_b, sink_b, conv_w, conv_b, dt_bias, a_log, d_skip, gnorm_c, norm_ffn2, ffn2_w13, ffn2_w2):
    nb, seq, d = x.shape
    nctx = ctx.shape[1]
    t = nctx + seq
    depth = w_mod.shape[0]
    tm, tm_all, tm_lat, fc, tq, tk = _tiling(nctx, seq)
    assert nctx == tm == SUB_ROWS and nctx % (C_STEP_CHUNKS * C_CHUNK) == 0 and seq % GRID_W == 0 and nb < 8

    xs = x
    cvec = jnp.zeros((8, d), F32).at[:nb].set(c).at[nb].set(c_ctx)
    mod = _mod_call(cvec, w_mod, b_mod).reshape(depth, 8, N_MOD, d)

    cosa, sina = _rope_tables(seq, nctx, A_QK, ROPE_LANES // A_QK)
    cosb, sinb = _rope_tables(seq, nctx, B_DIM, ROPE_LANES // B_DIM)
    ga = _group_matrix(A_QCOLS, A_QK)
    gb = _group_matrix(B_QCOLS, B_DIM)
    src = jnp.arange(A_QCOLS)
    dst = (src // (2 * A_QK)) * A_HLANES + src % (2 * A_QK)
    kpl = (dst[:, None] == jnp.arange(A_HEADS * A_HLANES)[None, :]).astype(BF16)
    kone = (jnp.arange(A_HEADS * A_HLANES) % A_HLANES == 2 * A_QK).astype(F32).reshape(1, -1)
    dsrc = jnp.arange(B_KCOLS)
    dcol = jnp.arange(B_QCOLS)
    hpg = B_HEADS // B_KV_HEADS
    dup = jnp.logical_and(dcol[None, :] // (hpg * B_DIM) == dsrc[:, None] // B_DIM,
                          dcol[None, :] % B_DIM == dsrc[:, None] % B_DIM).astype(BF16)
    qi = jnp.arange(B_QBLOCKS * B_BLOCK)[:, None]
    kj = jnp.arange((B_QBLOCKS + 2) * B_BLOCK + nctx)[None, :]
    in_band = jnp.logical_and(kj - qi >= 0, kj - qi <= 2 * B_BLOCK)
    band = jnp.where(jnp.logical_or(in_band, kj >= (B_QBLOCKS + 2) * B_BLOCK), 0.0, NEG_INF).astype(F32)
    in_cols = w_in.shape[2]
    in_pad = (-in_cols) % DT_LANES
    w13_1, w2_1 = ffn1_w13.astype(BF16), ffn1_w2.astype(BF16)
    w13_2, w2_2 = ffn2_w13.astype(BF16), ffn2_w2.astype(BF16)
    w_in_p = jnp.pad(w_in, ((0, 0), (0, 0), (0, in_pad))).astype(BF16)
    w_out_b = w_out.astype(BF16)

    for l in range(depth):
        last = l == depth - 1
        lam_init = 0.8 - 0.6 * math.exp(-0.3 * l)
        hn = jnp.stack([jnp.tile(qn_a[l], A_QCOLS // A_QK), jnp.tile(kn_a[l], A_QCOLS // A_QK),
                        jnp.tile(qn_b[l], B_QCOLS // B_DIM), jnp.tile(kn_b[l], B_QCOLS // B_DIM)]).astype(F32)
        hn = jnp.zeros((8, A_QCOLS), F32).at[:4].set(hn)

        xs = _ffn_call(xs, mod[l], norm_ffn1[l], w13_1, w2_1, l, si=0, fc=fc,
                       tm=(SUB_ROWS if l == 0 else tm_all), ctx_rows=nctx, nbatch=nb, ctx_in=(ctx if l == 0 else None))

        qa_t, ka_x, va_t, qb, kb_t, vb2, z, xbc_raw, dt = _inproj_call(
            xs, mod[l], norm_mix[l], w_in_p, l, ga, gb, hn, kpl, kone, dup, dup.T, cosa, sina, cosb, sinb,
            tm=tm_all, ctx_rows=nctx, nbatch=nb)

        lamv = jnp.zeros((8, A_QK), F32).at[:4].set(jnp.stack([lam_q1[l], lam_k1[l], lam_q2[l], lam_k2[l]]))
        oa = _attn_a_call(lamv, subln_a[l].reshape(A_V, 1), qa_t, ka_x, va_t, tq=tq, tk=tk, ctx=nctx,
                          q_tile0=(nctx // tq if last else 0), lam_init=lam_init)

        ob = _attn_b_call(sink_b[l].astype(F32), band, qb, kb_t, vb2, ctx=nctx, blk0=(nctx // B_BLOCK if last else 0))

        xbc_t, bn = _conv_call(xbc_raw, conv_w[l], conv_b[l], tm=tm, ctx=nctx)
        par = (jnp.zeros((8, DT_LANES), F32)
               .at[0, :2 * C_HEADS].set(dt_bias[l].reshape(-1))
               .at[1, :2 * C_HEADS].set(-jnp.exp(a_log[l].astype(F32)).reshape(-1)))
        dsk_col = jnp.repeat(d_skip[l].astype(F32), C_HEAD_DIM).reshape(C_INNER, 1)
        yf, yb = _ssd_call(par, dsk_col, xbc_t, bn, dt, ctx=nctx)

        xs = _outproj_call(xs, mod[l], oa, ob, yf, yb, z, gnorm_c[l].reshape(1, C_INNER), w_out_b, l,
                           tm=(SUB_ROWS if last else tm_all), ctx_rows=nctx, row0=(nctx if last else 0), nbatch=nb)
        xs = _ffn_call(xs, mod[l], norm_ffn2[l], w13_2, w2_2, l, si=6, fc=fc,
                       tm=(tm_lat if last else tm_all), ctx_rows=(0 if last else nctx), nbatch=nb)
    return xs
```

```python
import functools
import math

import jax
import jax.numpy as jnp
from jax import lax
from jax.experimental import pallas as pl
from jax.experimental.pallas import tpu as pltpu

F32 = jnp.float32
BF16 = jnp.bfloat16
NEG_INF = float("-inf")

EPS = 1e-6
ROPE_BASE = 10000.0
GRID_W = 64
ROPE_LANES = 128
N_MOD = 9

A_HEADS, A_QK, A_V = 4, 32, 64
A_QCOLS = A_HEADS * 2 * A_QK
A_SCALE = A_QK ** -0.5
LOG2E = math.log2(math.e)
A_HLANES = 128
A_VROWS = A_V + 16
A_SHIFT_CAP = 48.0
B_HEADS, B_KV_HEADS, B_DIM = 4, 2, 64
B_QCOLS = B_HEADS * B_DIM
B_KCOLS = B_KV_HEADS * B_DIM
B_SCALE = B_DIM ** -0.5
B_BLOCK = 128
B_QBLOCKS = 2
C_HEADS, C_HEAD_DIM, C_GROUPS, C_STATE, C_CONV = 8, 64, 2, 64, 5
C_INNER = C_HEADS * C_HEAD_DIM
C_GN = C_GROUPS * C_STATE
C_XBC = C_INNER + 2 * C_GN
C_CHUNK = 128
C_STEP_CHUNKS = 2
C_HPG = C_HEADS // C_GROUPS
DT_LANES = 128
HALO = 8

MXU_WIDTH = 256
SUB_ROWS = 256
TOKEN_TILES = (768, 512, 256)
KEY_TILES = (1408, 768, 256)

V7X_VMEM_LIMIT = 56 * 1024 * 1024


def _params(sem, vmem=V7X_VMEM_LIMIT):
    return pltpu.CompilerParams(dimension_semantics=sem, vmem_limit_bytes=vmem)


def _rms_mod(x, nw, shift, scale):
    ms = jnp.mean(x * x, axis=-1, keepdims=True)
    y = x * lax.rsqrt(ms + EPS) * nw
    return y * (1.0 + scale) + shift


def _silu(v):
    return v * jax.nn.sigmoid(v)


def _mod_kernel(c_ref, w_ref, b_ref, o_ref):
    sc = _silu(c_ref[...])
    c1 = sc.astype(BF16)
    c2 = (sc - c1.astype(F32)).astype(BF16)
    w = w_ref[...]
    w1 = w.astype(BF16)
    w2 = (w - w1.astype(F32)).astype(BF16)
    dot = lambda a, bb: jnp.dot(a, bb, preferred_element_type=F32)
    o_ref[...] = dot(c1, w1) + (dot(c1, w2) + dot(c2, w1)) + b_ref[...]


def _mod_call(cvec, w_mod, b_mod, tn=1152):
    nl, d, n = w_mod.shape
    return pl.pallas_call(
        _mod_kernel,
        grid=(nl, n // tn),
        in_specs=[pl.BlockSpec((8, d), lambda l, j: (0, 0)),
                  pl.BlockSpec((None, d, tn), lambda l, j: (l, 0, j)),
                  pl.BlockSpec((None, 1, tn), lambda l, j: (l, 0, j))],
        out_specs=pl.BlockSpec((None, 8, tn), lambda l, j: (l, 0, j)),
        out_shape=jax.ShapeDtypeStruct((nl, 8, n), F32),
        compiler_params=_params(("parallel", "parallel")),
        name="adaln_mod",
    )(cvec, w_mod, b_mod.reshape(nl, 1, n))


def _gated_mix(oa, ob, yf, yb, z, gn):
    y = (yf.astype(F32) + yb.astype(F32)) * _silu(z.astype(F32))
    gw = C_INNER // C_GROUPS
    oc = []
    for g in range(C_GROUPS):
        yg = y[:, g * gw:(g + 1) * gw]
        ms = jnp.mean(yg * yg, axis=1, keepdims=True)
        oc.append((yg * lax.rsqrt(ms + EPS) * gn[:, g * gw:(g + 1) * gw]).astype(BF16))
    return jnp.concatenate([oa, ob] + oc, axis=1)


def _ffn_kernel(*refs, si, fc, ff, sub, nsub, sub0, ctx_subs, nbatch, split_input, pre_mix):
    refs = list(refs)
    c_ref = refs.pop(0) if split_input else None
    x_ref, mod_ref, nw_ref, w13_ref, w2_ref = refs[:5]
    if pre_mix:
        oa_ref, ob_ref, yf_ref, yb_ref, z_ref, gn_ref, wo_ref = refs[5:12]
    o_ref, a_scr = refs[-2:]
    m_lat, m_ctx = mod_ref[pl.program_id(0)], mod_ref[nbatch]
    for k in range(nsub):
        rows = slice(k * sub, (k + 1) * sub)
        is_ctx = pl.program_id(1) * nsub + k + sub0 < ctx_subs
        x = jnp.where(is_ctx, c_ref[...], x_ref[...]) if split_input else x_ref[rows, :]
        m = jnp.where(is_ctx, m_ctx, m_lat)
        if pre_mix:
            mix = _gated_mix(oa_ref[rows, :], ob_ref[rows, :], yf_ref[rows, :], yb_ref[rows, :], z_ref[rows, :],
                             gn_ref[...])
            x = x + m[5:6] * jnp.dot(mix, wo_ref[...], preferred_element_type=F32)
        h = _rms_mod(x, nw_ref[...], m[si:si + 1], m[si + 1:si + 2]).astype(BF16)
        for c in range(ff // fc):
            g = jnp.dot(h, w13_ref[:, c * fc:(c + 1) * fc], preferred_element_type=F32)
            u = jnp.dot(h, w13_ref[:, ff + c * fc:ff + (c + 1) * fc], preferred_element_type=F32)
            a_scr[k, :, c * fc:(c + 1) * fc] = (_silu(g) * u).astype(BF16)
        y = jnp.dot(a_scr[k], w2_ref[...], preferred_element_type=F32)
        o_ref[rows, :] = x + (0.5 * m[si + 2:si + 3]) * y


def _ffn_call(xs, mod, nw, w13, w2, layer, *, si, tm, fc, ctx_rows, nbatch, ctx_in=None, mix=None, row0=0):
    b, t, d = xs.shape
    ff = w2.shape[1]
    split = ctx_in is not None
    nsub = tm // SUB_ROWS
    ctx_subs = ctx_rows // SUB_ROWS
    assert (not split or nsub == 1) and row0 % tm == 0
    tile0 = row0 // tm
    ntiles = (t - row0) // tm + (ctx_subs if split else 0)
    tok = lambda w: pl.BlockSpec((None, tm, w), lambda bi, i: (bi, i + tile0, 0))
    tok0 = lambda w: pl.BlockSpec((None, tm, w), lambda bi, i: (bi, i, 0))
    slab = lambda a: pl.BlockSpec((None,) + a.shape[1:], lambda bi, i: (layer, 0, 0), pipeline_mode=pl.Buffered(1))
    if split:
        tok_specs = [pl.BlockSpec((None, tm, d), lambda bi, i: (bi, jnp.minimum(i, ctx_subs - 1), 0)),
                     pl.BlockSpec((None, tm, d), lambda bi, i: (bi, jnp.maximum(i - ctx_subs, 0), 0))]
        toks = (ctx_in, xs)
    else:
        tok_specs = [tok(d)]
        toks = (xs,)
    mix_specs, mix_args = [], ()
    if mix is not None:
        oa, ob, yf, yb, z, gn, w_out = mix
        mix_specs = [tok0(oa.shape[2]), tok0(ob.shape[2]), tok(C_INNER), tok(C_INNER), tok(C_INNER),
                     pl.BlockSpec(gn.shape, lambda bi, i: (0, 0)), slab(w_out)]
        mix_args = mix
    return pl.pallas_call(
        functools.partial(_ffn_kernel, si=si, fc=fc, ff=ff, sub=SUB_ROWS, nsub=nsub, sub0=row0 // SUB_ROWS,
                          ctx_subs=ctx_subs, nbatch=nbatch, split_input=split, pre_mix=mix is not None),
        grid=(b, ntiles),
        in_specs=tok_specs + [pl.BlockSpec(mod.shape, lambda bi, i: (0, 0, 0)),
                              pl.BlockSpec((1, d), lambda bi, i: (0, 0)), slab(w13), slab(w2)] + mix_specs,
        out_specs=tok0(d),
        out_shape=jax.ShapeDtypeStruct((b, ntiles * tm, d), F32),
        scratch_shapes=[pltpu.VMEM((nsub, SUB_ROWS, ff), BF16)],
        compiler_params=_params(("parallel", "parallel")),
        name="swiglu_half",
    )(*toks, mod, nw.reshape(1, d), w13, w2, *mix_args)


def _rope(v, cos, sin_signed, quarter):
    n = v.shape[-1]
    lane = lax.broadcasted_iota(jnp.int32, v.shape, 1)
    first = (lane & (2 * quarter - 1)) < quarter
    vr = jnp.where(first, pltpu.roll(v, n - quarter, 1), pltpu.roll(v, quarter, 1))
    return v * cos + vr * sin_signed


def _group_norm(v, gmat, inv_n, w):
    sq = v * v
    hi = sq.astype(BF16)
    lo = (sq - hi.astype(F32)).astype(BF16)
    ms = (jnp.dot(hi, gmat, preferred_element_type=F32) + jnp.dot(lo, gmat, preferred_element_type=F32)) * inv_n
    return v * lax.rsqrt(ms + EPS) * w


def _inproj_kernel(x_ref, mod_ref, nw_ref, w_ref, ga_ref, gb_ref, hn_ref, kpl_ref, kone_ref, dup_ref, dupt_ref,
                   cosa_ref, sina_ref, cosb_ref, sinb_ref, qa_o, ka_o, va_o, qb_o, kb_o, vb_o, z_o, xbc_o, dt_o, *,
                   sub, nsub, ctx_subs, nbatch):
    m_lat, m_ctx = mod_ref[pl.program_id(0)], mod_ref[nbatch]
    hn = hn_ref[...]
    ga, gb = ga_ref[...], gb_ref[...]
    twice = lambda a: jnp.concatenate([a, a], axis=1)
    for k in range(nsub):
        rows = slice(k * sub, (k + 1) * sub)
        m = jnp.where(pl.program_id(1) * nsub + k < ctx_subs, m_ctx, m_lat)
        h = _rms_mod(x_ref[rows, :], nw_ref[...], m[3:4], m[4:5]).astype(BF16)
        p = jnp.dot(h, w_ref[...], preferred_element_type=F32)
        cosa, sina = twice(cosa_ref[rows, :]), twice(sina_ref[rows, :])
        cosb, sinb = twice(cosb_ref[rows, :]), twice(sinb_ref[rows, :])
        o = 0
        qa = _group_norm(p[:, o:o + A_QCOLS], ga, 1.0 / A_QK, hn[0:1]); o += A_QCOLS
        ka = _group_norm(p[:, o:o + A_QCOLS], ga, 1.0 / A_QK, hn[1:2]); o += A_QCOLS
        qa_o[:, rows] = (_rope(qa, cosa, sina, A_QK // 4) * (A_SCALE * LOG2E)).T.astype(BF16)
        ka16 = _rope(ka, cosa, sina, A_QK // 4).astype(BF16)
        ka_o[rows, :] = (jnp.dot(ka16, kpl_ref[...], preferred_element_type=F32) + kone_ref[...]).astype(BF16)
        va_o[:, rows] = p[:, o:o + A_HEADS * A_V].T.astype(BF16); o += A_HEADS * A_V
        qb = _group_norm(p[:, o:o + B_QCOLS], gb, 1.0 / B_DIM, hn[2:3]); o += B_QCOLS
        kb = _group_norm(p[:, o:o + B_KCOLS], gb[:B_KCOLS, :B_KCOLS], 1.0 / B_DIM, hn[3:4, :B_KCOLS]); o += B_KCOLS
        qb_o[rows, :] = (_rope(qb, cosb, sinb, B_DIM // 4) * B_SCALE).astype(BF16)
        kb16_t = _rope(kb, cosb[:, :B_KCOLS], sinb[:, :B_KCOLS], B_DIM // 4).T.astype(BF16)
        kb_o[:, rows] = jnp.dot(dupt_ref[...], kb16_t, preferred_element_type=F32).astype(BF16)
        vb_o[rows, :] = jnp.dot(p[:, o:o + B_KCOLS].astype(BF16), dup_ref[...],
                                preferred_element_type=F32).astype(BF16); o += B_KCOLS
        z_o[rows, :] = p[:, o:o + C_INNER].astype(BF16); o += C_INNER
        xbc_o[rows, :] = p[:, o:o + C_XBC]; o += C_XBC
        dt_o[rows, :] = p[:, o:o + DT_LANES]


def _inproj_call(xs, mod, nw, w_in_p, layer, ga, gb, hn, kpl, kone, dup, dupt, cosa, sina, cosb, sinb, *,
                 tm, ctx_rows, nbatch):
    b, t, d = xs.shape
    ntiles = t // tm
    tok = lambda w: pl.BlockSpec((None, tm, w), lambda bi, i: (bi, i, 0))
    tab = lambda w: pl.BlockSpec((tm, w), lambda bi, i: (i, 0))
    full = lambda a: pl.BlockSpec(a.shape, lambda bi, i: (0,) * a.ndim)
    widths = (A_QCOLS, A_HEADS * A_HLANES, A_HEADS * A_V, B_QCOLS, B_QCOLS, B_QCOLS, C_INNER, C_XBC, DT_LANES)
    transposed = (0, 2, 4)
    dtypes = (BF16,) * 7 + (F32,) * 2
    return pl.pallas_call(
        functools.partial(_inproj_kernel, sub=SUB_ROWS, nsub=tm // SUB_ROWS, ctx_subs=ctx_rows // SUB_ROWS,
                          nbatch=nbatch),
        grid=(b, ntiles),
        in_specs=[tok(d), full(mod), pl.BlockSpec((1, d), lambda bi, i: (0, 0)),
                  pl.BlockSpec((None,) + w_in_p.shape[1:], lambda bi, i: (layer, 0, 0), pipeline_mode=pl.Buffered(1)),
                  full(ga), full(gb), full(hn), full(kpl), full(kone), full(dup), full(dupt)] + [tab(ROPE_LANES)] * 4,
        out_specs=[pl.BlockSpec((None, w, tm), lambda bi, i: (bi, 0, i)) if k in transposed else tok(w)
                   for k, w in enumerate(widths)],
        out_shape=[jax.ShapeDtypeStruct((b, w, t) if k in transposed else (b, t, w), dt)
                   for k, (w, dt) in enumerate(zip(widths, dtypes))],
        compiler_params=_params(("parallel", "parallel")),
        name="in_proj_heads",
    )(xs, mod, nw.reshape(1, d), w_in_p, ga, gb, hn, kpl, kone, dup, dupt, cosa, sina, cosb, sinb)


def _attn_a_kernel(lamv_ref, subln_ref, qt_ref, kx_ref, vt_ref, o_ref, kmax_scr, *,
                   tq, tk, ctx, nk, q_tile0, lam_init):
    step = pl.program_id(1)

    @pl.when(step == 0)
    def _():
        r = lax.broadcasted_iota(jnp.int32, (A_HLANES, A_HLANES), 0)
        c = lax.broadcasted_iota(jnp.int32, (A_HLANES, A_HLANES), 1)
        in_comp = jnp.logical_and(r >= c * A_QK, r < (c + 1) * A_QK)
        sel = jnp.where(jnp.logical_and(in_comp, c < 2), 1.0, 0.0).astype(BF16)
        for h in range(A_HEADS):
            kk = kx_ref[:, h * A_HLANES:(h + 1) * A_HLANES].astype(F32)
            ksq = jnp.dot((kk * kk).astype(BF16), sel, preferred_element_type=F32)
            kmax_scr[h:h + 1, :] = jnp.sqrt(jnp.max(ksq, axis=0, keepdims=True))

    lv = lamv_ref[...]
    lam = (jnp.exp(jnp.sum(lv[0:1] * lv[1:2], axis=1, keepdims=True))
           - jnp.exp(jnp.sum(lv[2:3] * lv[3:4], axis=1, keepdims=True)) + lam_init)
    qt = qt_ref[...].astype(F32)
    qcs = [qt[hc * A_QK:(hc + 1) * A_QK, :] for hc in range(2 * A_HEADS)]
    shifts = [jnp.sqrt(jnp.sum(qc * qc, axis=0, keepdims=True)) * kmax_scr[hc // 2:hc // 2 + 1, hc % 2:hc % 2 + 1]
              for hc, qc in enumerate(qcs)]
    shift_max = jnp.max(functools.reduce(jnp.maximum, shifts))

    def q_ext(h, shifted):
        z = jnp.zeros((A_QK, tq), F32)
        top = jnp.concatenate([qcs[2 * h], z], axis=1)
        mid = jnp.concatenate([z, qcs[2 * h + 1]], axis=1)
        row = lax.broadcasted_iota(jnp.int32, (A_HLANES - 2 * A_QK, 2 * tq), 0)
        if shifted:
            bot = jnp.where(row == 0, -jnp.concatenate([shifts[2 * h], shifts[2 * h + 1]], axis=1), 0.0)
        else:
            bot = jnp.zeros(row.shape, F32)
        return jnp.concatenate([top, mid, bot], axis=0).astype(BF16)

    def scores(qx, h, start, size):
        return jnp.dot(kx_ref[pl.ds(start, size), h * A_HLANES:(h + 1) * A_HLANES], qx, preferred_element_type=F32)

    def v_ext(h, start, size):
        ones = jnp.ones((A_VROWS - A_V, size), BF16)
        return jnp.concatenate([vt_ref[h * A_V:(h + 1) * A_V, pl.ds(start, size)], ones], axis=0)

    def finish(accs):
        rows = []
        for acc in accs:
            o = (acc[:A_V, :tq] / acc[A_V:A_V + 1, :tq]) - lam * (acc[:A_V, tq:] / acc[A_V:A_V + 1, tq:])
            ms = jnp.mean(o * o, axis=0, keepdims=True)
            rows.append(o * lax.rsqrt(ms + EPS) * subln_ref[...] * (1.0 - lam_init))
        o_ref[...] = jnp.concatenate(rows, axis=0).T.astype(o_ref.dtype)

    def attend(shifted):
        qx = [q_ext(h, shifted) for h in range(A_HEADS)]

        def head_step(h, start, carry):
            s = scores(qx[h], h, start, tk)
            ve = v_ext(h, start, tk)
            if shifted:
                return carry + jnp.dot(ve, jnp.exp2(s).astype(BF16), preferred_element_type=F32)
            m_run, acc = carry
            m_new = jnp.maximum(m_run, jnp.max(s, axis=0, keepdims=True))
            p = jnp.exp2(s - m_new).astype(BF16)
            return m_new, jnp.exp2(m_run - m_new) * acc + jnp.dot(ve, p, preferred_element_type=F32)

        def body(j, carries):
            start = pl.multiple_of(j * tk, tk)
            return tuple(head_step(h, start, carries[h]) for h in range(A_HEADS))

        acc0 = jnp.zeros((A_VROWS, 2 * tq), F32)
        if shifted:
            accs = lax.fori_loop(0, nk, body, (acc0,) * A_HEADS, unroll=True)
        else:
            init = (jnp.full((1, 2 * tq), NEG_INF, F32), acc0)
            accs = [c[1] for c in lax.fori_loop(0, nk, body, (init,) * A_HEADS)]
        finish(accs)

    def attend_ctx():
        accs = []
        for h in range(A_HEADS):
            s = scores(q_ext(h, False), h, 0, ctx)
            p = jnp.exp2(s - jnp.max(s, axis=0, keepdims=True)).astype(BF16)
            accs.append(jnp.dot(v_ext(h, 0, ctx), p, preferred_element_type=F32))
        finish(accs)

    safe = shift_max < A_SHIFT_CAP
    if q_tile0 * tq < ctx:
        is_ctx = (step + q_tile0) * tq < ctx
        pl.when(is_ctx)(attend_ctx)
        pl.when(jnp.logical_and(jnp.logical_not(is_ctx), safe))(lambda: attend(True))
        pl.when(jnp.logical_and(jnp.logical_not(is_ctx), jnp.logical_not(safe)))(lambda: attend(False))
    else:
        pl.when(safe)(lambda: attend(True))
        pl.when(jnp.logical_not(safe))(lambda: attend(False))


def _attn_a_call(lamv, subln_col, qa_t, ka_x, va_t, *, tq, tk, ctx, q_tile0, lam_init):
    b, w, t = qa_t.shape
    nq = t // tq - q_tile0
    return pl.pallas_call(
        functools.partial(_attn_a_kernel, tq=tq, tk=tk, ctx=ctx, nk=t // tk, q_tile0=q_tile0, lam_init=lam_init),
        grid=(b, nq),
        in_specs=[pl.BlockSpec(lamv.shape, lambda bi, qi: (0, 0)),
                  pl.BlockSpec(subln_col.shape, lambda bi, qi: (0, 0)),
                  pl.BlockSpec((None, w, tq), lambda bi, qi: (bi, 0, qi + q_tile0)),
                  pl.BlockSpec((None, t, A_HEADS * A_HLANES), lambda bi, qi: (bi, 0, 0)),
                  pl.BlockSpec((None, w, t), lambda bi, qi: (bi, 0, 0))],
        out_specs=pl.BlockSpec((None, tq, w), lambda bi, qi: (bi, qi, 0)),
        out_shape=jax.ShapeDtypeStruct((b, nq * tq, w), BF16),
        scratch_shapes=[pltpu.VMEM((8, A_HLANES), F32)],
        compiler_params=_params(("parallel", "arbitrary")),
        name="diff_attention",
    )(lamv, subln_col, qa_t, ka_x, va_t)


def _attn_b_kernel(sink_ref, band_ref, q_ref, kp_ref, k0_ref, k1_ref, kn_ref, kx_ref,
                   vp_ref, v0_ref, v1_ref, vn_ref, vx_ref, o_ref, *, blk0, ctx_blocks, nblk):
    n0 = pl.program_id(1) * B_QBLOCKS + blk0
    nq = B_QBLOCKS * B_BLOCK
    k_all = jnp.concatenate([kp_ref[...], k0_ref[...], k1_ref[...], kn_ref[...], kx_ref[...]], axis=1)
    v_all = jnp.concatenate([vp_ref[...], v0_ref[...], v1_ref[...], vn_ref[...], vx_ref[...]], axis=0)
    lane = lax.broadcasted_iota(jnp.int32, (1, k_all.shape[1]), 1)
    lat = n0 >= ctx_blocks
    pen = jnp.zeros(lane.shape, F32)
    for d in range(B_QBLOCKS + 2):
        kb = n0 - 1 + d
        ok = jnp.logical_and(lat, jnp.logical_and(kb >= ctx_blocks, kb <= nblk - 1))
        in_blk = jnp.logical_and(lane >= d * B_BLOCK, lane < (d + 1) * B_BLOCK)
        pen = jnp.where(in_blk, jnp.where(ok, 0.0, NEG_INF), pen)
    bias = band_ref[...] + pen
    hpg = B_HEADS // B_KV_HEADS
    gl = hpg * B_DIM
    qlane = lax.broadcasted_iota(jnp.int32, (nq, gl), 1) // B_DIM
    outs = []
    for g in range(B_KV_HEADS):
        q_g = q_ref[:, g * gl:(g + 1) * gl]
        o_g = jnp.zeros((nq, gl), F32)
        for e in range(hpg):
            q_e = jnp.where(qlane == e, q_g, jnp.zeros_like(q_g))
            s = jnp.dot(q_e, k_all[g * gl:(g + 1) * gl, :], preferred_element_type=F32) + bias
            sk = sink_ref[g * hpg + e]
            m = jnp.maximum(jnp.max(s, axis=1, keepdims=True), sk)
            p = jnp.exp(s - m)
            den = jnp.sum(p, axis=1, keepdims=True) + jnp.exp(sk - m)
            pv = jnp.dot(p.astype(BF16), v_all[:, g * gl:(g + 1) * gl], preferred_element_type=F32)
            o_g = jnp.where(qlane == e, pv / den, o_g)
        outs.append(o_g)
    o_ref[...] = jnp.concatenate(outs, axis=1).astype(o_ref.dtype)


def _attn_b_call(sink, band, qb, kb_t, vb2, *, ctx, blk0):
    b, t, w = qb.shape
    nblk = t // B_BLOCK
    ctx_blocks = ctx // B_BLOCK
    nq = B_QBLOCKS * B_BLOCK
    first = lambda i: i * B_QBLOCKS + blk0
    clamp = lambda n: jnp.clip(n, 0, nblk - 1)
    kspec = lambda d: pl.BlockSpec((None, w, B_BLOCK), lambda bi, i: (bi, 0, clamp(first(i) - 1 + d)))
    vspec = lambda d: pl.BlockSpec((None, B_BLOCK, w), lambda bi, i: (bi, clamp(first(i) - 1 + d), 0))
    nwb = B_QBLOCKS + 2
    return pl.pallas_call(
        functools.partial(_attn_b_kernel, blk0=blk0, ctx_blocks=ctx_blocks, nblk=nblk),
        grid=(b, (nblk - blk0) // B_QBLOCKS),
        in_specs=[pl.BlockSpec(memory_space=pltpu.SMEM),
                  pl.BlockSpec(band.shape, lambda bi, i: (0, 0)),
                  pl.BlockSpec((None, nq, w), lambda bi, i: (bi, i + blk0 // B_QBLOCKS, 0))]
                 + [kspec(d) for d in range(nwb)] + [pl.BlockSpec((None, w, ctx), lambda bi, i: (bi, 0, 0))]
                 + [vspec(d) for d in range(nwb)] + [pl.BlockSpec((None, ctx, w), lambda bi, i: (bi, 0, 0))],
        out_specs=pl.BlockSpec((None, nq, w), lambda bi, i: (bi, i, 0)),
        out_shape=jax.ShapeDtypeStruct((b, (nblk - blk0) * B_BLOCK, w), BF16),
        compiler_params=_params(("parallel", "parallel")),
        name="window_attention",
    )(sink, band, qb, *([kb_t] * (nwb + 1)), *([vb2] * (nwb + 1)))


def _conv_kernel(u_ref, up_ref, un_ref, w_ref, b_ref, xt_o, bn_o, *, tm, ctx_tiles, ntiles):
    i = pl.program_id(1)
    u = u_ref[...]
    has_prev = jnp.logical_and(i != 0, i != ctx_tiles)
    has_next = jnp.logical_and(i != ctx_tiles - 1, i != ntiles - 1)
    up = jnp.where(has_prev, up_ref[...], 0.0)
    un = jnp.where(has_next, un_ref[...], 0.0)
    full = jnp.concatenate([up, u, un], axis=0)
    w = w_ref[...]
    acc = b_ref[...] + w[0:1] * full[HALO - 2:HALO - 2 + tm]
    for k in range(1, C_CONV):
        acc = acc + w[k:k + 1] * full[HALO - 2 + k:HALO - 2 + k + tm]
    y = _silu(acc)
    xt_o[...] = y.T
    bn_o[...] = y[:, C_INNER:C_INNER + C_GN]


def _conv_call(xbc_raw, conv_w, conv_b, *, tm, ctx):
    b, t, ch = xbc_raw.shape
    ntiles = t // tm
    hb = tm // HALO
    nh = t // HALO
    w8 = jnp.zeros((8, ch), F32).at[:C_CONV].set(conv_w)
    return pl.pallas_call(
        functools.partial(_conv_kernel, tm=tm, ctx_tiles=ctx // tm, ntiles=ntiles),
        grid=(b, ntiles),
        in_specs=[pl.BlockSpec((None, tm, ch), lambda bi, i: (bi, i, 0)),
                  pl.BlockSpec((None, HALO, ch), lambda bi, i: (bi, jnp.maximum(i * hb - 1, 0), 0)),
                  pl.BlockSpec((None, HALO, ch), lambda bi, i: (bi, jnp.minimum((i + 1) * hb, nh - 1), 0)),
                  pl.BlockSpec((8, ch), lambda bi, i: (0, 0)),
                  pl.BlockSpec((1, ch), lambda bi, i: (0, 0))],
        out_specs=[pl.BlockSpec((None, ch, tm), lambda bi, i: (bi, 0, i)),
                   pl.BlockSpec((None, tm, C_GN), lambda bi, i: (bi, i, 0))],
        out_shape=[jax.ShapeDtypeStruct((b, ch, t), F32), jax.ShapeDtypeStruct((b, t, C_GN), F32)],
        compiler_params=_params(("parallel", "parallel")),
        name="ssm_conv",
    )(xbc_raw, xbc_raw, xbc_raw, w8, conv_b.reshape(1, ch))


def _softplus(v):
    return jnp.maximum(v, 0.0) + jnp.log1p(jnp.exp(-jnp.abs(v)))


def _split3(v):
    v1 = v.astype(BF16)
    r1 = v - v1.astype(F32)
    v2 = r1.astype(BF16)
    v3 = (r1 - v2.astype(F32)).astype(BF16)
    return v1, v2, v3


def _ssd_prepare(xt_all, bn, dt_raw, par, *, lane0, backward):
    c_t = xt_all[C_INNER + C_GN:]
    dt = _softplus(dt_raw + par[0:1])
    a = dt * par[1:2]
    s_idx = lax.broadcasted_iota(jnp.int32, (C_CHUNK, C_CHUNK), 0)
    l_idx = lax.broadcasted_iota(jnp.int32, (C_CHUNK, C_CHUNK), 1)
    tri = jnp.where(l_idx <= s_idx, 1.0, 0.0).astype(BF16)
    cs3 = jnp.dot(tri, jnp.concatenate(_split3(a), axis=1), preferred_element_type=F32)
    cs = cs3[:, :DT_LANES] + cs3[:, DT_LANES:2 * DT_LANES] + cs3[:, 2 * DT_LANES:]
    key = cs - a if backward else cs
    k1, k2, k3 = _split3(key)
    parts = (k1.astype(F32) + pltpu.roll(k2.astype(F32), 2 * C_HEADS, 1)
             + pltpu.roll(k3.astype(F32), 4 * C_HEADS, 1)).astype(BF16)
    src = lax.broadcasted_iota(jnp.int32, (DT_LANES, C_HEADS * C_CHUNK), 0)
    dst = lax.broadcasted_iota(jnp.int32, (DT_LANES, C_HEADS * C_CHUNK), 1)
    pick = jnp.logical_and(src < 6 * C_HEADS, (src & (2 * C_HEADS - 1)) == lane0 + dst // C_CHUNK)
    colb_all = jnp.dot(parts, jnp.where(pick, 1.0, 0.0).astype(BF16), preferred_element_type=F32)
    return dict(x_t=xt_all[:C_INNER], c_t=c_t, c16=c_t.astype(BF16), bn=bn, dt_t=dt.T, key_t=key.T, cs=cs,
                colb_all=colb_all, mask=(s_idx >= l_idx) if backward else (s_idx <= l_idx))


def _ssd_group(st, g, hs, *, lane0, backward):
    lane_g = lax.broadcasted_iota(jnp.int32, st["bn"].shape, 1) // C_STATE
    row_g = lax.broadcasted_iota(jnp.int32, st["c_t"].shape, 0) // C_STATE
    bn_g = jnp.where(lane_g == g, st["bn"], 0.0).astype(BF16)
    ct_g = jnp.where(row_g == g, st["c_t"], 0.0).astype(BF16)
    cb_t = jnp.dot(bn_g, st["c16"], preferred_element_type=F32)
    y_off = jnp.dot(hs.astype(BF16), ct_g, preferred_element_type=F32)
    y_g, xw_g, hs_new, xdts, m_ts, carries = [], [], [], [], [], []
    for r in range(C_HPG):
        h = g * C_HPG + r
        hl = lane0 + h
        colb = st["colb_all"][:, h * C_CHUNK:(h + 1) * C_CHUNK]
        row = st["key_t"][hl:hl + 1, :]
        tot_h = st["cs"][C_CHUNK - 1:C_CHUNK, hl:hl + 1]
        d = (colb - row) if backward else (row - colb)
        m_ts.append((cb_t * jnp.exp(jnp.where(st["mask"], d, NEG_INF))).astype(BF16))
        xdt = st["x_t"][h * C_HEAD_DIM:(h + 1) * C_HEAD_DIM] * st["dt_t"][hl:hl + 1, :]
        xdts.append(xdt.astype(BF16))
        carries.append(jnp.exp(tot_h - row) if backward else jnp.exp(row))
        w_row = jnp.exp(row) if backward else jnp.exp(tot_h - row)
        xw_g.append((xdt * w_row).astype(BF16))
        hs_new.append(jnp.exp(tot_h) * hs[r * C_HEAD_DIM:(r + 1) * C_HEAD_DIM])
    for r in range(0, C_HPG, 2):
        both = jnp.dot(jnp.concatenate(xdts[r:r + 2], axis=0), jnp.concatenate(m_ts[r:r + 2], axis=1),
                       preferred_element_type=F32)
        for e in range(2):
            y_diag = both[e * C_HEAD_DIM:(e + 1) * C_HEAD_DIM, e * C_CHUNK:(e + 1) * C_CHUNK]
            y_g.append(y_diag + y_off[(r + e) * C_HEAD_DIM:(r + e + 1) * C_HEAD_DIM] * carries[r + e])
    state = jnp.concatenate(hs_new, axis=0) + jnp.dot(jnp.concatenate(xw_g, axis=0), bn_g, preferred_element_type=F32)
    return y_g, state


def _ssd_kernel(par_ref, dsk_ref, xf_ref, bf_ref, dtf_ref, xb_ref, bb_ref, dtb_ref, yf_ref, yb_ref,
                hf_scr, hb_scr, dsk_scr):
    @pl.when(pl.program_id(1) == 0)
    def _():
        hf_scr[...] = jnp.zeros(hf_scr.shape, F32)
        hb_scr[...] = jnp.zeros(hb_scr.shape, F32)
        dsk_scr[...] = jnp.broadcast_to(dsk_ref[...], dsk_scr.shape)

    par = par_ref[...]
    hf = [hf_scr[g] for g in range(C_GROUPS)]
    hb = [hb_scr[g] for g in range(C_GROUPS)]
    cfs = [slice(k * C_CHUNK, (k + 1) * C_CHUNK) for k in range(C_STEP_CHUNKS)]
    cbs = cfs[::-1]
    sfs = [_ssd_prepare(xf_ref[:, c], bf_ref[c, :], dtf_ref[c, :], par, lane0=0, backward=False) for c in cfs]
    sbs = [_ssd_prepare(xb_ref[:, c], bb_ref[c, :], dtb_ref[c, :], par, lane0=C_HEADS, backward=True) for c in cbs]
    for cf, cb, sf, sb in zip(cfs, cbs, sfs, sbs):
        rows_f, rows_b = [], []
        for g in range(C_GROUPS):
            y_g, hf[g] = _ssd_group(sf, g, hf[g], lane0=0, backward=False)
            rows_f += y_g
            y_g, hb[g] = _ssd_group(sb, g, hb[g], lane0=C_HEADS, backward=True)
            rows_b += y_g
        yf_ref[cf, :] = (jnp.concatenate(rows_f, axis=0) + dsk_scr[...] * sf["x_t"]).T.astype(yf_ref.dtype)
        yb_ref[cb, :] = jnp.concatenate(rows_b, axis=0).T.astype(yb_ref.dtype)
    for g in range(C_GROUPS):
        hf_scr[g] = hf[g]
        hb_scr[g] = hb[g]


def _ssd_call(par, dsk_col, xbc_t, bn, dt, *, ctx):
    b, ch, t = xbc_t.shape
    blk = C_STEP_CHUNKS * C_CHUNK
    nblk = t // blk
    ncb = ctx // blk
    bblock = lambda j: jnp.where(j < ncb, ncb - 1 - j, nblk - 1 + ncb - j)
    fwd = lambda bi, j: (bi, j, 0)
    bwd = lambda bi, j: (bi, bblock(j), 0)
    fwd_t = lambda bi, j: (bi, 0, j)
    bwd_t = lambda bi, j: (bi, 0, bblock(j))
    tspec = lambda im: pl.BlockSpec((None, ch, blk), im)
    nspec = lambda w, im: pl.BlockSpec((None, blk, w), im)
    return pl.pallas_call(
        _ssd_kernel,
        grid=(b, nblk),
        in_specs=[pl.BlockSpec((8, DT_LANES), lambda bi, j: (0, 0)),
                  pl.BlockSpec(dsk_col.shape, lambda bi, j: (0, 0)),
                  tspec(fwd_t), nspec(C_GN, fwd), nspec(DT_LANES, fwd),
                  tspec(bwd_t), nspec(C_GN, bwd), nspec(DT_LANES, bwd)],
        out_specs=[nspec(C_INNER, fwd), nspec(C_INNER, bwd)],
        out_shape=[jax.ShapeDtypeStruct((b, t, C_INNER), BF16)] * 2,
        scratch_shapes=[pltpu.VMEM((C_GROUPS, C_HPG * C_HEAD_DIM, C_GN), F32)] * 2
                       + [pltpu.VMEM((C_INNER, C_CHUNK), F32)],
        compiler_params=_params(("parallel", "arbitrary")),
        name="ssd_scan",
    )(par, dsk_col, xbc_t, bn, dt, xbc_t, bn, dt)


def _rope_tables(seq, ctx, dim, reps):
    rows = seq // GRID_W
    row = jnp.repeat(jnp.arange(rows, dtype=F32), GRID_W)
    col = jnp.tile(jnp.arange(GRID_W, dtype=F32), rows)
    quarter = dim // 4
    inv_freq = ROPE_BASE ** (-jnp.arange(quarter, dtype=F32) / quarter)
    ar = row[:, None] * inv_freq
    ac = col[:, None] * inv_freq
    ang = jnp.concatenate([ar, ar, ac, ac], axis=-1)
    sign = jnp.where((jnp.arange(dim) % (dim // 2)) < quarter, -1.0, 1.0).astype(F32)
    cos = jnp.concatenate([jnp.ones((ctx, dim), F32), jnp.cos(ang)], axis=0)
    sin = jnp.concatenate([jnp.zeros((ctx, dim), F32), jnp.sin(ang) * sign], axis=0)
    return jnp.tile(cos, (1, reps)), jnp.tile(sin, (1, reps))


def _group_matrix(n, group):
    idx = jnp.arange(n) // group
    return (idx[:, None] == idx[None, :]).astype(BF16)


def _tiling(nctx, seq):
    t = nctx + seq
    token_tile = next(c for c in TOKEN_TILES if t % c == 0)
    key_tile = next(c for c in KEY_TILES if t % c == 0 and (t > c or c == KEY_TILES[-1]))
    return SUB_ROWS, token_tile, MXU_WIDTH, SUB_ROWS, key_tile


def kernel(x, c, ctx, c_ctx, w_mod, b_mod, norm_ffn1, ffn1_w13, ffn1_w2, norm_mix, w_in, w_out, qn_a, kn_a, lam_q1, lam_k1, lam_q2, lam_k2, subln_a, qn_b, kn_b, sink_b, conv_w, conv_b, dt_bias, a_log, d_skip, gnorm_c, norm_ffn2, ffn2_w13, ffn2_w2):
    nb, seq, d = x.shape
    nctx = ctx.shape[1]
    t = nctx + seq
    depth = w_mod.shape[0]
    tm, tm_all, fc, tq, tk = _tiling(nctx, seq)
    assert nctx == tm == SUB_ROWS and nctx % (C_STEP_CHUNKS * C_CHUNK) == 0 and seq % GRID_W == 0 and nb < 8

    xs = x
    cvec = jnp.zeros((8, d), F32).at[:nb].set(c).at[nb].set(c_ctx)
    mod = _mod_call(cvec, w_mod, b_mod).reshape(depth, 8, N_MOD, d)

    cosa, sina = _rope_tables(seq, nctx, A_QK, ROPE_LANES // A_QK)
    cosb, sinb = _rope_tables(seq, nctx, B_DIM, ROPE_LANES // B_DIM)
    ga = _group_matrix(A_QCOLS, A_QK)
    gb = _group_matrix(B_QCOLS, B_DIM)
    src = jnp.arange(A_QCOLS)
    dst = (src // (2 * A_QK)) * A_HLANES + src % (2 * A_QK)
    kpl = (dst[:, None] == jnp.arange(A_HEADS * A_HLANES)[None, :]).astype(BF16)
    kone = (jnp.arange(A_HEADS * A_HLANES) % A_HLANES == 2 * A_QK).astype(F32).reshape(1, -1)
    dsrc = jnp.arange(B_KCOLS)
    dcol = jnp.arange(B_QCOLS)
    hpg = B_HEADS // B_KV_HEADS
    dup = jnp.logical_and(dcol[None, :] // (hpg * B_DIM) == dsrc[:, None] // B_DIM,
                          dcol[None, :] % B_DIM == dsrc[:, None] % B_DIM).astype(BF16)
    qi = jnp.arange(B_QBLOCKS * B_BLOCK)[:, None]
    kj = jnp.arange((B_QBLOCKS + 2) * B_BLOCK + nctx)[None, :]
    in_band = jnp.logical_and(kj - qi >= 0, kj - qi <= 2 * B_BLOCK)
    band = jnp.where(jnp.logical_or(in_band, kj >= (B_QBLOCKS + 2) * B_BLOCK), 0.0, NEG_INF).astype(F32)
    in_cols = w_in.shape[2]
    in_pad = (-in_cols) % DT_LANES
    w13_1, w2_1 = ffn1_w13.astype(BF16), ffn1_w2.astype(BF16)
    w13_2, w2_2 = ffn2_w13.astype(BF16), ffn2_w2.astype(BF16)
    w_in_p = jnp.pad(w_in, ((0, 0), (0, 0), (0, in_pad))).astype(BF16)
    w_out_b = w_out.astype(BF16)

    for l in range(depth):
        last = l == depth - 1
        lam_init = 0.8 - 0.6 * math.exp(-0.3 * l)
        hn = jnp.stack([jnp.tile(qn_a[l], A_QCOLS // A_QK), jnp.tile(kn_a[l], A_QCOLS // A_QK),
                        jnp.tile(qn_b[l], B_QCOLS // B_DIM), jnp.tile(kn_b[l], B_QCOLS // B_DIM)]).astype(F32)
        hn = jnp.zeros((8, A_QCOLS), F32).at[:4].set(hn)

        xs = _ffn_call(xs, mod[l], norm_ffn1[l], w13_1, w2_1, l, si=0, fc=fc,
                       tm=(SUB_ROWS if l == 0 else tm_all), ctx_rows=nctx, nbatch=nb, ctx_in=(ctx if l == 0 else None))

        qa_t, ka_x, va_t, qb, kb_t, vb2, z, xbc_raw, dt = _inproj_call(
            xs, mod[l], norm_mix[l], w_in_p, l, ga, gb, hn, kpl, kone, dup, dup.T, cosa, sina, cosb, sinb,
            tm=tm_all, ctx_rows=nctx, nbatch=nb)

        lamv = jnp.zeros((8, A_QK), F32).at[:4].set(jnp.stack([lam_q1[l], lam_k1[l], lam_q2[l], lam_k2[l]]))
        oa = _attn_a_call(lamv, subln_a[l].reshape(A_V, 1), qa_t, ka_x, va_t, tq=tq, tk=tk, ctx=nctx,
                          q_tile0=(nctx // tq if last else 0), lam_init=lam_init)

        ob = _attn_b_call(sink_b[l].astype(F32), band, qb, kb_t, vb2, ctx=nctx, blk0=(nctx // B_BLOCK if last else 0))

        xbc_t, bn = _conv_call(xbc_raw, conv_w[l], conv_b[l], tm=tm, ctx=nctx)
        par = (jnp.zeros((8, DT_LANES), F32)
               .at[0, :2 * C_HEADS].set(dt_bias[l].reshape(-1))
               .at[1, :2 * C_HEADS].set(-jnp.exp(a_log[l].astype(F32)).reshape(-1)))
        dsk_col = jnp.repeat(d_skip[l].astype(F32), C_HEAD_DIM).reshape(C_INNER, 1)
        yf, yb = _ssd_call(par, dsk_col, xbc_t, bn, dt, ctx=nctx)

        xs = _ffn_call(xs, mod[l], norm_ffn2[l], w13_2, w2_2, l, si=6, fc=fc, tm=(SUB_ROWS if last else tm_all),
                       ctx_rows=nctx, nbatch=nb, row0=(nctx if last else 0),
                       mix=(oa, ob, yf, yb, z, gnorm_c[l].reshape(1, C_INNER), w_out_b))
    return xs
```

```python
import functools
import math

import jax
import jax.numpy as jnp
from jax import lax
from jax.experimental import pallas as pl
from jax.experimental.pallas import tpu as pltpu

F32 = jnp.float32
BF16 = jnp.bfloat16
NEG_INF = float("-inf")

EPS = 1e-6
ROPE_BASE = 10000.0
GRID_W = 64
ROPE_LANES = 128
N_MOD = 9

A_HEADS, A_QK, A_V = 4, 32, 64
A_QCOLS = A_HEADS * 2 * A_QK
A_SCALE = A_QK ** -0.5
LOG2E = math.log2(math.e)
A_HLANES = 128
A_VROWS = A_V + 16
A_SHIFT_CAP = 48.0
B_HEADS, B_KV_HEADS, B_DIM = 4, 2, 64
B_QCOLS = B_HEADS * B_DIM
B_KCOLS = B_KV_HEADS * B_DIM
B_SCALE = B_DIM ** -0.5
B_BLOCK = 128
B_QBLOCKS = 2
C_HEADS, C_HEAD_DIM, C_GROUPS, C_STATE, C_CONV = 8, 64, 2, 64, 5
C_INNER = C_HEADS * C_HEAD_DIM
C_GN = C_GROUPS * C_STATE
C_XBC = C_INNER + 2 * C_GN
C_CHUNK = 128
C_STEP_CHUNKS = 2
C_HPG = C_HEADS // C_GROUPS
DT_LANES = 128
HALO = 8

MXU_WIDTH = 256
SUB_ROWS = 256
TOKEN_TILES = (768, 512, 256)
KEY_TILES = (1408, 768, 256)

V7X_VMEM_LIMIT = 56 * 1024 * 1024


def _params(sem, vmem=V7X_VMEM_LIMIT):
    return pltpu.CompilerParams(dimension_semantics=sem, vmem_limit_bytes=vmem)


def _rms_mod(x, nw, shift, scale):
    ms = jnp.mean(x * x, axis=-1, keepdims=True)
    y = x * lax.rsqrt(ms + EPS) * nw
    return y * (1.0 + scale) + shift


def _silu(v):
    return v * jax.nn.sigmoid(v)


def _mod_kernel(c_ref, w_ref, b_ref, o_ref):
    sc = _silu(c_ref[...])
    c1 = sc.astype(BF16)
    c2 = (sc - c1.astype(F32)).astype(BF16)
    w = w_ref[...]
    w1 = w.astype(BF16)
    w2 = (w - w1.astype(F32)).astype(BF16)
    dot = lambda a, bb: jnp.dot(a, bb, preferred_element_type=F32)
    o_ref[...] = dot(c1, w1) + (dot(c1, w2) + dot(c2, w1)) + b_ref[...]


def _mod_call(cvec, w_mod, b_mod, tn=1152):
    nl, d, n = w_mod.shape
    return pl.pallas_call(
        _mod_kernel,
        grid=(nl, n // tn),
        in_specs=[pl.BlockSpec((8, d), lambda l, j: (0, 0)),
                  pl.BlockSpec((None, d, tn), lambda l, j: (l, 0, j)),
                  pl.BlockSpec((None, 1, tn), lambda l, j: (l, 0, j))],
        out_specs=pl.BlockSpec((None, 8, tn), lambda l, j: (l, 0, j)),
        out_shape=jax.ShapeDtypeStruct((nl, 8, n), F32),
        compiler_params=_params(("parallel", "parallel")),
        name="adaln_mod",
    )(cvec, w_mod, b_mod.reshape(nl, 1, n))


def _gated_mix(oa, ob, yf, yb, z, gn):
    y = (yf.astype(F32) + yb.astype(F32)) * _silu(z.astype(F32))
    gw = C_INNER // C_GROUPS
    oc = []
    for g in range(C_GROUPS):
        yg = y[:, g * gw:(g + 1) * gw]
        ms = jnp.mean(yg * yg, axis=1, keepdims=True)
        oc.append((yg * lax.rsqrt(ms + EPS) * gn[:, g * gw:(g + 1) * gw]).astype(BF16))
    return jnp.concatenate([oa, ob] + oc, axis=1)


def _ffn_kernel(*refs, si, fc, ff, sub, nsub, sub0, ctx_subs, nbatch, split_input, pre_mix):
    refs = list(refs)
    if split_input:
        c_ref, x_blocks = refs[0], refs[1:1 + nsub]
        refs = refs[nsub:]
    x_ref, mod_ref, nw_ref, w13_ref, w2_ref = refs[:5]
    if pre_mix:
        oa_ref, ob_ref, yf_ref, yb_ref, z_ref, gn_ref, wo_ref = refs[5:12]
    o_ref, a_scr = refs[-2:]
    m_lat, m_ctx = mod_ref[pl.program_id(0)], mod_ref[nbatch]
    for k in range(nsub):
        rows = slice(k * sub, (k + 1) * sub)
        is_ctx = pl.program_id(1) * nsub + k + sub0 < ctx_subs
        x = jnp.where(is_ctx, c_ref[...], x_blocks[k][...]) if split_input else x_ref[rows, :]
        m = jnp.where(is_ctx, m_ctx, m_lat)
        if pre_mix:
            mix = _gated_mix(oa_ref[rows, :], ob_ref[rows, :], yf_ref[rows, :], yb_ref[rows, :], z_ref[rows, :],
                             gn_ref[...])
            x = x + m[5:6] * jnp.dot(mix, wo_ref[...], preferred_element_type=F32)
        h = _rms_mod(x, nw_ref[...], m[si:si + 1], m[si + 1:si + 2]).astype(BF16)
        for c in range(ff // fc):
            g = jnp.dot(h, w13_ref[:, c * fc:(c + 1) * fc], preferred_element_type=F32)
            u = jnp.dot(h, w13_ref[:, ff + c * fc:ff + (c + 1) * fc], preferred_element_type=F32)
            a_scr[k, :, c * fc:(c + 1) * fc] = (_silu(g) * u).astype(BF16)
        y = jnp.dot(a_scr[k], w2_ref[...], preferred_element_type=F32)
        o_ref[rows, :] = x + (0.5 * m[si + 2:si + 3]) * y


def _ffn_call(xs, mod, nw, w13, w2, layer, *, si, tm, fc, ctx_rows, nbatch, ctx_in=None, mix=None, row0=0):
    b, t, d = xs.shape
    ff = w2.shape[1]
    split = ctx_in is not None
    nsub = tm // SUB_ROWS
    ctx_subs = ctx_rows // SUB_ROWS
    assert row0 % tm == 0 and (not split or (row0 == 0 and ctx_subs == 1 and (t + ctx_rows) % tm == 0))
    tile0 = row0 // tm
    ntiles = (t + ctx_rows) // tm if split else (t - row0) // tm
    tok = lambda w: pl.BlockSpec((None, tm, w), lambda bi, i: (bi, i + tile0, 0))
    tok0 = lambda w: pl.BlockSpec((None, tm, w), lambda bi, i: (bi, i, 0))
    slab = lambda a: pl.BlockSpec((None,) + a.shape[1:], lambda bi, i: (layer, 0, 0), pipeline_mode=pl.Buffered(1))
    if split:
        blk = lambda k: pl.BlockSpec((None, SUB_ROWS, d), lambda bi, i: (bi, jnp.maximum(i * nsub + k - ctx_subs, 0), 0))
        tok_specs = [pl.BlockSpec((None, SUB_ROWS, d), lambda bi, i: (bi, 0, 0))] + [blk(k) for k in range(nsub)]
        toks = (ctx_in,) + (xs,) * nsub
    else:
        tok_specs = [tok(d)]
        toks = (xs,)
    mix_specs, mix_args = [], ()
    if mix is not None:
        oa, ob, yf, yb, z, gn, w_out = mix
        mix_specs = [tok0(oa.shape[2]), tok0(ob.shape[2]), tok(C_INNER), tok(C_INNER), tok(C_INNER),
                     pl.BlockSpec(gn.shape, lambda bi, i: (0, 0)), slab(w_out)]
        mix_args = mix
    return pl.pallas_call(
        functools.partial(_ffn_kernel, si=si, fc=fc, ff=ff, sub=SUB_ROWS, nsub=nsub, sub0=row0 // SUB_ROWS,
                          ctx_subs=ctx_subs, nbatch=nbatch, split_input=split, pre_mix=mix is not None),
        grid=(b, ntiles),
        in_specs=tok_specs + [pl.BlockSpec(mod.shape, lambda bi, i: (0, 0, 0)),
                              pl.BlockSpec((1, d), lambda bi, i: (0, 0)), slab(w13), slab(w2)] + mix_specs,
        out_specs=tok0(d),
        out_shape=jax.ShapeDtypeStruct((b, ntiles * tm, d), F32),
        scratch_shapes=[pltpu.VMEM((nsub, SUB_ROWS, ff), BF16)],
        compiler_params=_params(("parallel", "parallel")),
        name="swiglu_half",
    )(*toks, mod, nw.reshape(1, d), w13, w2, *mix_args)


def _rope(v, cos, sin_signed, quarter):
    n = v.shape[-1]
    lane = lax.broadcasted_iota(jnp.int32, v.shape, 1)
    first = (lane & (2 * quarter - 1)) < quarter
    vr = jnp.where(first, pltpu.roll(v, n - quarter, 1), pltpu.roll(v, quarter, 1))
    return v * cos + vr * sin_signed


def _group_norm(v, gmat, inv_n, w):
    sq = v * v
    hi = sq.astype(BF16)
    lo = (sq - hi.astype(F32)).astype(BF16)
    ms = (jnp.dot(hi, gmat, preferred_element_type=F32) + jnp.dot(lo, gmat, preferred_element_type=F32)) * inv_n
    return v * lax.rsqrt(ms + EPS) * w


def _inproj_kernel(x_ref, mod_ref, nw_ref, w_ref, ga_ref, gb_ref, hn_ref, kpl_ref, kone_ref, dup_ref, dupt_ref,
                   cosa_ref, sina_ref, cosb_ref, sinb_ref, qa_o, ka_o, va_o, qb_o, kb_o, vb_o, z_o, xbc_o, dt_o, *,
                   sub, nsub, ctx_subs, nbatch):
    m_lat, m_ctx = mod_ref[pl.program_id(0)], mod_ref[nbatch]
    hn = hn_ref[...]
    ga, gb = ga_ref[...], gb_ref[...]
    twice = lambda a: jnp.concatenate([a, a], axis=1)
    for k in range(nsub):
        rows = slice(k * sub, (k + 1) * sub)
        m = jnp.where(pl.program_id(1) * nsub + k < ctx_subs, m_ctx, m_lat)
        h = _rms_mod(x_ref[rows, :], nw_ref[...], m[3:4], m[4:5]).astype(BF16)
        p = jnp.dot(h, w_ref[...], preferred_element_type=F32)
        cosa, sina = twice(cosa_ref[rows, :]), twice(sina_ref[rows, :])
        cosb, sinb = twice(cosb_ref[rows, :]), twice(sinb_ref[rows, :])
        o = 0
        qa = _group_norm(p[:, o:o + A_QCOLS], ga, 1.0 / A_QK, hn[0:1]); o += A_QCOLS
        ka = _group_norm(p[:, o:o + A_QCOLS], ga, 1.0 / A_QK, hn[1:2]); o += A_QCOLS
        qa_o[:, rows] = (_rope(qa, cosa, sina, A_QK // 4) * (A_SCALE * LOG2E)).T.astype(BF16)
        ka16 = _rope(ka, cosa, sina, A_QK // 4).astype(BF16)
        ka_o[rows, :] = (jnp.dot(ka16, kpl_ref[...], preferred_element_type=F32) + kone_ref[...]).astype(BF16)
        va_o[:, rows] = p[:, o:o + A_HEADS * A_V].T.astype(BF16); o += A_HEADS * A_V
        qb = _group_norm(p[:, o:o + B_QCOLS], gb, 1.0 / B_DIM, hn[2:3]); o += B_QCOLS
        kb = _group_norm(p[:, o:o + B_KCOLS], gb[:B_KCOLS, :B_KCOLS], 1.0 / B_DIM, hn[3:4, :B_KCOLS]); o += B_KCOLS
        qb_o[rows, :] = (_rope(qb, cosb, sinb, B_DIM // 4) * B_SCALE).astype(BF16)
        kb16_t = _rope(kb, cosb[:, :B_KCOLS], sinb[:, :B_KCOLS], B_DIM // 4).T.astype(BF16)
        kb_o[:, rows] = jnp.dot(dupt_ref[...], kb16_t, preferred_element_type=F32).astype(BF16)
        vb_o[rows, :] = jnp.dot(p[:, o:o + B_KCOLS].astype(BF16), dup_ref[...],
                                preferred_element_type=F32).astype(BF16); o += B_KCOLS
        z_o[rows, :] = p[:, o:o + C_INNER].astype(BF16); o += C_INNER
        xbc_o[rows, :] = p[:, o:o + C_XBC]; o += C_XBC
        dt_o[rows, :] = p[:, o:o + DT_LANES]


def _inproj_call(xs, mod, nw, w_in_p, layer, ga, gb, hn, kpl, kone, dup, dupt, cosa, sina, cosb, sinb, *,
                 tm, ctx_rows, nbatch):
    b, t, d = xs.shape
    ntiles = t // tm
    tok = lambda w: pl.BlockSpec((None, tm, w), lambda bi, i: (bi, i, 0))
    tab = lambda w: pl.BlockSpec((tm, w), lambda bi, i: (i, 0))
    full = lambda a: pl.BlockSpec(a.shape, lambda bi, i: (0,) * a.ndim)
    widths = (A_QCOLS, A_HEADS * A_HLANES, A_HEADS * A_V, B_QCOLS, B_QCOLS, B_QCOLS, C_INNER, C_XBC, DT_LANES)
    transposed = (0, 2, 4)
    dtypes = (BF16,) * 7 + (F32,) * 2
    return pl.pallas_call(
        functools.partial(_inproj_kernel, sub=SUB_ROWS, nsub=tm // SUB_ROWS, ctx_subs=ctx_rows // SUB_ROWS,
                          nbatch=nbatch),
        grid=(b, ntiles),
        in_specs=[tok(d), full(mod), pl.BlockSpec((1, d), lambda bi, i: (0, 0)),
                  pl.BlockSpec((None,) + w_in_p.shape[1:], lambda bi, i: (layer, 0, 0), pipeline_mode=pl.Buffered(1)),
                  full(ga), full(gb), full(hn), full(kpl), full(kone), full(dup), full(dupt)] + [tab(ROPE_LANES)] * 4,
        out_specs=[pl.BlockSpec((None, w, tm), lambda bi, i: (bi, 0, i)) if k in transposed else tok(w)
                   for k, w in enumerate(widths)],
        out_shape=[jax.ShapeDtypeStruct((b, w, t) if k in transposed else (b, t, w), dt)
                   for k, (w, dt) in enumerate(zip(widths, dtypes))],
        compiler_params=_params(("parallel", "parallel")),
        name="in_proj_heads",
    )(xs, mod, nw.reshape(1, d), w_in_p, ga, gb, hn, kpl, kone, dup, dupt, cosa, sina, cosb, sinb)


def _attn_a_kernel(lamv_ref, subln_ref, qt_ref, kx_ref, vt_ref, o_ref, kmax_scr, *,
                   tq, tk, ctx, nk, q_tile0, lam_init):
    step = pl.program_id(1)

    @pl.when(step == 0)
    def _():
        r = lax.broadcasted_iota(jnp.int32, (A_HLANES, A_HLANES), 0)
        c = lax.broadcasted_iota(jnp.int32, (A_HLANES, A_HLANES), 1)
        in_comp = jnp.logical_and(r >= c * A_QK, r < (c + 1) * A_QK)
        sel = jnp.where(jnp.logical_and(in_comp, c < 2), 1.0, 0.0).astype(BF16)
        for h in range(A_HEADS):
            kk = kx_ref[:, h * A_HLANES:(h + 1) * A_HLANES].astype(F32)
            ksq = jnp.dot((kk * kk).astype(BF16), sel, preferred_element_type=F32)
            kmax_scr[h:h + 1, :] = jnp.sqrt(jnp.max(ksq, axis=0, keepdims=True))

    lv = lamv_ref[...]
    lam = (jnp.exp(jnp.sum(lv[0:1] * lv[1:2], axis=1, keepdims=True))
           - jnp.exp(jnp.sum(lv[2:3] * lv[3:4], axis=1, keepdims=True)) + lam_init)
    qt = qt_ref[...].astype(F32)
    qcs = [qt[hc * A_QK:(hc + 1) * A_QK, :] for hc in range(2 * A_HEADS)]
    shifts = [jnp.sqrt(jnp.sum(qc * qc, axis=0, keepdims=True)) * kmax_scr[hc // 2:hc // 2 + 1, hc % 2:hc % 2 + 1]
              for hc, qc in enumerate(qcs)]
    shift_max = jnp.max(functools.reduce(jnp.maximum, shifts))

    def q_ext(h, shifted):
        z = jnp.zeros((A_QK, tq), F32)
        top = jnp.concatenate([qcs[2 * h], z], axis=1)
        mid = jnp.concatenate([z, qcs[2 * h + 1]], axis=1)
        row = lax.broadcasted_iota(jnp.int32, (A_HLANES - 2 * A_QK, 2 * tq), 0)
        if shifted:
            bot = jnp.where(row == 0, -jnp.concatenate([shifts[2 * h], shifts[2 * h + 1]], axis=1), 0.0)
        else:
            bot = jnp.zeros(row.shape, F32)
        return jnp.concatenate([top, mid, bot], axis=0).astype(BF16)

    def scores(qx, h, start, size):
        return jnp.dot(kx_ref[pl.ds(start, size), h * A_HLANES:(h + 1) * A_HLANES], qx, preferred_element_type=F32)

    def v_ext(h, start, size):
        ones = jnp.ones((A_VROWS - A_V, size), BF16)
        return jnp.concatenate([vt_ref[h * A_V:(h + 1) * A_V, pl.ds(start, size)], ones], axis=0)

    def finish(accs):
        rows = []
        for acc in accs:
            o = (acc[:A_V, :tq] / acc[A_V:A_V + 1, :tq]) - lam * (acc[:A_V, tq:] / acc[A_V:A_V + 1, tq:])
            ms = jnp.mean(o * o, axis=0, keepdims=True)
            rows.append(o * lax.rsqrt(ms + EPS) * subln_ref[...] * (1.0 - lam_init))
        o_ref[...] = jnp.concatenate(rows, axis=0).T.astype(o_ref.dtype)

    def attend(shifted):
        qx = [q_ext(h, shifted) for h in range(A_HEADS)]

        def head_step(h, start, carry):
            s = scores(qx[h], h, start, tk)
            ve = v_ext(h, start, tk)
            if shifted:
                return carry + jnp.dot(ve, jnp.exp2(s).astype(BF16), preferred_element_type=F32)
            m_run, acc = carry
            m_new = jnp.maximum(m_run, jnp.max(s, axis=0, keepdims=True))
            p = jnp.exp2(s - m_new).astype(BF16)
            return m_new, jnp.exp2(m_run - m_new) * acc + jnp.dot(ve, p, preferred_element_type=F32)

        def body(j, carries):
            start = pl.multiple_of(j * tk, tk)
            return tuple(head_step(h, start, carries[h]) for h in range(A_HEADS))

        acc0 = jnp.zeros((A_VROWS, 2 * tq), F32)
        if shifted:
            accs = lax.fori_loop(0, nk, body, (acc0,) * A_HEADS, unroll=True)
        else:
            init = (jnp.full((1, 2 * tq), NEG_INF, F32), acc0)
            accs = [c[1] for c in lax.fori_loop(0, nk, body, (init,) * A_HEADS)]
        finish(accs)

    def attend_ctx():
        accs = []
        for h in range(A_HEADS):
            s = scores(q_ext(h, False), h, 0, ctx)
            p = jnp.exp2(s - jnp.max(s, axis=0, keepdims=True)).astype(BF16)
            accs.append(jnp.dot(v_ext(h, 0, ctx), p, preferred_element_type=F32))
        finish(accs)

    safe = shift_max < A_SHIFT_CAP
    if q_tile0 * tq < ctx:
        is_ctx = (step + q_tile0) * tq < ctx
        pl.when(is_ctx)(attend_ctx)
        pl.when(jnp.logical_and(jnp.logical_not(is_ctx), safe))(lambda: attend(True))
        pl.when(jnp.logical_and(jnp.logical_not(is_ctx), jnp.logical_not(safe)))(lambda: attend(False))
    else:
        pl.when(safe)(lambda: attend(True))
        pl.when(jnp.logical_not(safe))(lambda: attend(False))


def _attn_a_call(lamv, subln_col, qa_t, ka_x, va_t, *, tq, tk, ctx, q_tile0, lam_init):
    b, w, t = qa_t.shape
    nq = t // tq - q_tile0
    return pl.pallas_call(
        functools.partial(_attn_a_kernel, tq=tq, tk=tk, ctx=ctx, nk=t // tk, q_tile0=q_tile0, lam_init=lam_init),
        grid=(b, nq),
        in_specs=[pl.BlockSpec(lamv.shape, lambda bi, qi: (0, 0)),
                  pl.BlockSpec(subln_col.shape, lambda bi, qi: (0, 0)),
                  pl.BlockSpec((None, w, tq), lambda bi, qi: (bi, 0, qi + q_tile0)),
                  pl.BlockSpec((None, t, A_HEADS * A_HLANES), lambda bi, qi: (bi, 0, 0)),
                  pl.BlockSpec((None, w, t), lambda bi, qi: (bi, 0, 0))],
        out_specs=pl.BlockSpec((None, tq, w), lambda bi, qi: (bi, qi, 0)),
        out_shape=jax.ShapeDtypeStruct((b, nq * tq, w), BF16),
        scratch_shapes=[pltpu.VMEM((8, A_HLANES), F32)],
        compiler_params=_params(("parallel", "arbitrary")),
        name="diff_attention",
    )(lamv, subln_col, qa_t, ka_x, va_t)


def _attn_b_kernel(sink_ref, band_ref, q_ref, kp_ref, k0_ref, k1_ref, kn_ref, kx_ref,
                   vp_ref, v0_ref, v1_ref, vn_ref, vx_ref, o_ref, *, blk0, ctx_blocks, nblk):
    n0 = pl.program_id(1) * B_QBLOCKS + blk0
    nq = B_QBLOCKS * B_BLOCK
    k_all = jnp.concatenate([kp_ref[...], k0_ref[...], k1_ref[...], kn_ref[...], kx_ref[...]], axis=1)
    v_all = jnp.concatenate([vp_ref[...], v0_ref[...], v1_ref[...], vn_ref[...], vx_ref[...]], axis=0)
    lane = lax.broadcasted_iota(jnp.int32, (1, k_all.shape[1]), 1)
    lat = n0 >= ctx_blocks
    pen = jnp.zeros(lane.shape, F32)
    for d in range(B_QBLOCKS + 2):
        kb = n0 - 1 + d
        ok = jnp.logical_and(lat, jnp.logical_and(kb >= ctx_blocks, kb <= nblk - 1))
        in_blk = jnp.logical_and(lane >= d * B_BLOCK, lane < (d + 1) * B_BLOCK)
        pen = jnp.where(in_blk, jnp.where(ok, 0.0, NEG_INF), pen)
    bias = band_ref[...] + pen
    hpg = B_HEADS // B_KV_HEADS
    gl = hpg * B_DIM
    qlane = lax.broadcasted_iota(jnp.int32, (nq, gl), 1) // B_DIM
    outs = []
    for g in range(B_KV_HEADS):
        q_g = q_ref[:, g * gl:(g + 1) * gl]
        o_g = jnp.zeros((nq, gl), F32)
        for e in range(hpg):
            q_e = jnp.where(qlane == e, q_g, jnp.zeros_like(q_g))
            s = jnp.dot(q_e, k_all[g * gl:(g + 1) * gl, :], preferred_element_type=F32) + bias
            sk = sink_ref[g * hpg + e]
            m = jnp.maximum(jnp.max(s, axis=1, keepdims=True), sk)
            p = jnp.exp(s - m)
            den = jnp.sum(p, axis=1, keepdims=True) + jnp.exp(sk - m)
            pv = jnp.dot(p.astype(BF16), v_all[:, g * gl:(g + 1) * gl], preferred_element_type=F32)
            o_g = jnp.where(qlane == e, pv / den, o_g)
        outs.append(o_g)
    o_ref[...] = jnp.concatenate(outs, axis=1).astype(o_ref.dtype)


def _attn_b_call(sink, band, qb, kb_t, vb2, *, ctx, blk0):
    b, t, w = qb.shape
    nblk = t // B_BLOCK
    ctx_blocks = ctx // B_BLOCK
    nq = B_QBLOCKS * B_BLOCK
    first = lambda i: i * B_QBLOCKS + blk0
    clamp = lambda n: jnp.clip(n, 0, nblk - 1)
    kspec = lambda d: pl.BlockSpec((None, w, B_BLOCK), lambda bi, i: (bi, 0, clamp(first(i) - 1 + d)))
    vspec = lambda d: pl.BlockSpec((None, B_BLOCK, w), lambda bi, i: (bi, clamp(first(i) - 1 + d), 0))
    nwb = B_QBLOCKS + 2
    return pl.pallas_call(
        functools.partial(_attn_b_kernel, blk0=blk0, ctx_blocks=ctx_blocks, nblk=nblk),
        grid=(b, (nblk - blk0) // B_QBLOCKS),
        in_specs=[pl.BlockSpec(memory_space=pltpu.SMEM),
                  pl.BlockSpec(band.shape, lambda bi, i: (0, 0)),
                  pl.BlockSpec((None, nq, w), lambda bi, i: (bi, i + blk0 // B_QBLOCKS, 0))]
                 + [kspec(d) for d in range(nwb)] + [pl.BlockSpec((None, w, ctx), lambda bi, i: (bi, 0, 0))]
                 + [vspec(d) for d in range(nwb)] + [pl.BlockSpec((None, ctx, w), lambda bi, i: (bi, 0, 0))],
        out_specs=pl.BlockSpec((None, nq, w), lambda bi, i: (bi, i, 0)),
        out_shape=jax.ShapeDtypeStruct((b, (nblk - blk0) * B_BLOCK, w), BF16),
        compiler_params=_params(("parallel", "parallel")),
        name="window_attention",
    )(sink, band, qb, *([kb_t] * (nwb + 1)), *([vb2] * (nwb + 1)))


def _conv_kernel(u_ref, up_ref, un_ref, w_ref, b_ref, xt_o, bn_o, *, tm, ctx_tiles, ntiles):
    i = pl.program_id(1)
    u = u_ref[...]
    has_prev = jnp.logical_and(i != 0, i != ctx_tiles)
    has_next = jnp.logical_and(i != ctx_tiles - 1, i != ntiles - 1)
    up = jnp.where(has_prev, up_ref[...], 0.0)
    un = jnp.where(has_next, un_ref[...], 0.0)
    full = jnp.concatenate([up, u, un], axis=0)
    w = w_ref[...]
    acc = b_ref[...] + w[0:1] * full[HALO - 2:HALO - 2 + tm]
    for k in range(1, C_CONV):
        acc = acc + w[k:k + 1] * full[HALO - 2 + k:HALO - 2 + k + tm]
    y = _silu(acc)
    xt_o[...] = y.T
    bn_o[...] = y[:, C_INNER:C_INNER + C_GN]


def _conv_call(xbc_raw, conv_w, conv_b, *, tm, ctx):
    b, t, ch = xbc_raw.shape
    ntiles = t // tm
    hb = tm // HALO
    nh = t // HALO
    w8 = jnp.zeros((8, ch), F32).at[:C_CONV].set(conv_w)
    return pl.pallas_call(
        functools.partial(_conv_kernel, tm=tm, ctx_tiles=ctx // tm, ntiles=ntiles),
        grid=(b, ntiles),
        in_specs=[pl.BlockSpec((None, tm, ch), lambda bi, i: (bi, i, 0)),
                  pl.BlockSpec((None, HALO, ch), lambda bi, i: (bi, jnp.maximum(i * hb - 1, 0), 0)),
                  pl.BlockSpec((None, HALO, ch), lambda bi, i: (bi, jnp.minimum((i + 1) * hb, nh - 1), 0)),
                  pl.BlockSpec((8, ch), lambda bi, i: (0, 0)),
                  pl.BlockSpec((1, ch), lambda bi, i: (0, 0))],
        out_specs=[pl.BlockSpec((None, ch, tm), lambda bi, i: (bi, 0, i)),
                   pl.BlockSpec((None, tm, C_GN), lambda bi, i: (bi, i, 0))],
        out_shape=[jax.ShapeDtypeStruct((b, ch, t), F32), jax.ShapeDtypeStruct((b, t, C_GN), F32)],
        compiler_params=_params(("parallel", "parallel")),
        name="ssm_conv",
    )(xbc_raw, xbc_raw, xbc_raw, w8, conv_b.reshape(1, ch))


def _softplus(v):
    return jnp.maximum(v, 0.0) + jnp.log1p(jnp.exp(-jnp.abs(v)))


def _split3(v):
    v1 = v.astype(BF16)
    r1 = v - v1.astype(F32)
    v2 = r1.astype(BF16)
    v3 = (r1 - v2.astype(F32)).astype(BF16)
    return v1, v2, v3


def _ssd_prepare(xt_all, bn, dt_raw, par, *, lane0, backward):
    c_t = xt_all[C_INNER + C_GN:]
    dt = _softplus(dt_raw + par[0:1])
    a = dt * par[1:2]
    s_idx = lax.broadcasted_iota(jnp.int32, (C_CHUNK, C_CHUNK), 0)
    l_idx = lax.broadcasted_iota(jnp.int32, (C_CHUNK, C_CHUNK), 1)
    tri = jnp.where(l_idx <= s_idx, 1.0, 0.0).astype(BF16)
    cs3 = jnp.dot(tri, jnp.concatenate(_split3(a), axis=1), preferred_element_type=F32)
    cs = cs3[:, :DT_LANES] + cs3[:, DT_LANES:2 * DT_LANES] + cs3[:, 2 * DT_LANES:]
    key = cs - a if backward else cs
    k1, k2, k3 = _split3(key)
    parts = (k1.astype(F32) + pltpu.roll(k2.astype(F32), 2 * C_HEADS, 1)
             + pltpu.roll(k3.astype(F32), 4 * C_HEADS, 1)).astype(BF16)
    src = lax.broadcasted_iota(jnp.int32, (DT_LANES, C_HEADS * C_CHUNK), 0)
    dst = lax.broadcasted_iota(jnp.int32, (DT_LANES, C_HEADS * C_CHUNK), 1)
    pick = jnp.logical_and(src < 6 * C_HEADS, (src & (2 * C_HEADS - 1)) == lane0 + dst // C_CHUNK)
    colb_all = jnp.dot(parts, jnp.where(pick, 1.0, 0.0).astype(BF16), preferred_element_type=F32)
    return dict(x_t=xt_all[:C_INNER], c_t=c_t, c16=c_t.astype(BF16), bn=bn, dt_t=dt.T, key_t=key.T, cs=cs,
                colb_all=colb_all, mask=(s_idx >= l_idx) if backward else (s_idx <= l_idx))


def _ssd_group(st, g, hs, *, lane0, backward):
    lane_g = lax.broadcasted_iota(jnp.int32, st["bn"].shape, 1) // C_STATE
    row_g = lax.broadcasted_iota(jnp.int32, st["c_t"].shape, 0) // C_STATE
    bn_g = jnp.where(lane_g == g, st["bn"], 0.0).astype(BF16)
    ct_g = jnp.where(row_g == g, st["c_t"], 0.0).astype(BF16)
    cb_t = jnp.dot(bn_g, st["c16"], preferred_element_type=F32)
    y_off = jnp.dot(hs.astype(BF16), ct_g, preferred_element_type=F32)
    y_g, xw_g, hs_new, xdts, m_ts, carries = [], [], [], [], [], []
    for r in range(C_HPG):
        h = g * C_HPG + r
        hl = lane0 + h
        colb = st["colb_all"][:, h * C_CHUNK:(h + 1) * C_CHUNK]
        row = st["key_t"][hl:hl + 1, :]
        tot_h = st["cs"][C_CHUNK - 1:C_CHUNK, hl:hl + 1]
        d = (colb - row) if backward else (row - colb)
        m_ts.append((cb_t * jnp.exp(jnp.where(st["mask"], d, NEG_INF))).astype(BF16))
        xdt = st["x_t"][h * C_HEAD_DIM:(h + 1) * C_HEAD_DIM] * st["dt_t"][hl:hl + 1, :]
        xdts.append(xdt.astype(BF16))
        carries.append(jnp.exp(tot_h - row) if backward else jnp.exp(row))
        w_row = jnp.exp(row) if backward else jnp.exp(tot_h - row)
        xw_g.append((xdt * w_row).astype(BF16))
        hs_new.append(jnp.exp(tot_h) * hs[r * C_HEAD_DIM:(r + 1) * C_HEAD_DIM])
    for r in range(0, C_HPG, 2):
        both = jnp.dot(jnp.concatenate(xdts[r:r + 2], axis=0), jnp.concatenate(m_ts[r:r + 2], axis=1),
                       preferred_element_type=F32)
        for e in range(2):
            y_diag = both[e * C_HEAD_DIM:(e + 1) * C_HEAD_DIM, e * C_CHUNK:(e + 1) * C_CHUNK]
            y_g.append(y_diag + y_off[(r + e) * C_HEAD_DIM:(r + e + 1) * C_HEAD_DIM] * carries[r + e])
    state = jnp.concatenate(hs_new, axis=0) + jnp.dot(jnp.concatenate(xw_g, axis=0), bn_g, preferred_element_type=F32)
    return y_g, state


def _ssd_kernel(par_ref, dsk_ref, xf_ref, bf_ref, dtf_ref, xb_ref, bb_ref, dtb_ref, yf_ref, yb_ref,
                hf_scr, hb_scr, dsk_scr):
    @pl.when(pl.program_id(1) == 0)
    def _():
        hf_scr[...] = jnp.zeros(hf_scr.shape, F32)
        hb_scr[...] = jnp.zeros(hb_scr.shape, F32)
        dsk_scr[...] = jnp.broadcast_to(dsk_ref[...], dsk_scr.shape)

    par = par_ref[...]
    hf = [hf_scr[g] for g in range(C_GROUPS)]
    hb = [hb_scr[g] for g in range(C_GROUPS)]
    cfs = [slice(k * C_CHUNK, (k + 1) * C_CHUNK) for k in range(C_STEP_CHUNKS)]
    cbs = cfs[::-1]
    sfs = [_ssd_prepare(xf_ref[:, c], bf_ref[c, :], dtf_ref[c, :], par, lane0=0, backward=False) for c in cfs]
    sbs = [_ssd_prepare(xb_ref[:, c], bb_ref[c, :], dtb_ref[c, :], par, lane0=C_HEADS, backward=True) for c in cbs]
    for cf, cb, sf, sb in zip(cfs, cbs, sfs, sbs):
        rows_f, rows_b = [], []
        for g in range(C_GROUPS):
            y_g, hf[g] = _ssd_group(sf, g, hf[g], lane0=0, backward=False)
            rows_f += y_g
            y_g, hb[g] = _ssd_group(sb, g, hb[g], lane0=C_HEADS, backward=True)
            rows_b += y_g
        yf_ref[cf, :] = (jnp.concatenate(rows_f, axis=0) + dsk_scr[...] * sf["x_t"]).T.astype(yf_ref.dtype)
        yb_ref[cb, :] = jnp.concatenate(rows_b, axis=0).T.astype(yb_ref.dtype)
    for g in range(C_GROUPS):
        hf_scr[g] = hf[g]
        hb_scr[g] = hb[g]


def _ssd_call(par, dsk_col, xbc_t, bn, dt, *, ctx):
    b, ch, t = xbc_t.shape
    blk = C_STEP_CHUNKS * C_CHUNK
    nblk = t // blk
    ncb = ctx // blk
    bblock = lambda j: jnp.where(j < ncb, ncb - 1 - j, nblk - 1 + ncb - j)
    fwd = lambda bi, j: (bi, j, 0)
    bwd = lambda bi, j: (bi, bblock(j), 0)
    fwd_t = lambda bi, j: (bi, 0, j)
    bwd_t = lambda bi, j: (bi, 0, bblock(j))
    tspec = lambda im: pl.BlockSpec((None, ch, blk), im)
    nspec = lambda w, im: pl.BlockSpec((None, blk, w), im)
    return pl.pallas_call(
        _ssd_kernel,
        grid=(b, nblk),
        in_specs=[pl.BlockSpec((8, DT_LANES), lambda bi, j: (0, 0)),
                  pl.BlockSpec(dsk_col.shape, lambda bi, j: (0, 0)),
                  tspec(fwd_t), nspec(C_GN, fwd), nspec(DT_LANES, fwd),
                  tspec(bwd_t), nspec(C_GN, bwd), nspec(DT_LANES, bwd)],
        out_specs=[nspec(C_INNER, fwd), nspec(C_INNER, bwd)],
        out_shape=[jax.ShapeDtypeStruct((b, t, C_INNER), BF16)] * 2,
        scratch_shapes=[pltpu.VMEM((C_GROUPS, C_HPG * C_HEAD_DIM, C_GN), F32)] * 2
                       + [pltpu.VMEM((C_INNER, C_CHUNK), F32)],
        compiler_params=_params(("parallel", "arbitrary")),
        name="ssd_scan",
    )(par, dsk_col, xbc_t, bn, dt, xbc_t, bn, dt)


def _rope_tables(seq, ctx, dim, reps):
    rows = seq // GRID_W
    row = jnp.repeat(jnp.arange(rows, dtype=F32), GRID_W)
    col = jnp.tile(jnp.arange(GRID_W, dtype=F32), rows)
    quarter = dim // 4
    inv_freq = ROPE_BASE ** (-jnp.arange(quarter, dtype=F32) / quarter)
    ar = row[:, None] * inv_freq
    ac = col[:, None] * inv_freq
    ang = jnp.concatenate([ar, ar, ac, ac], axis=-1)
    sign = jnp.where((jnp.arange(dim) % (dim // 2)) < quarter, -1.0, 1.0).astype(F32)
    cos = jnp.concatenate([jnp.ones((ctx, dim), F32), jnp.cos(ang)], axis=0)
    sin = jnp.concatenate([jnp.zeros((ctx, dim), F32), jnp.sin(ang) * sign], axis=0)
    return jnp.tile(cos, (1, reps)), jnp.tile(sin, (1, reps))


def _group_matrix(n, group):
    idx = jnp.arange(n) // group
    return (idx[:, None] == idx[None, :]).astype(BF16)


def _tiling(nctx, seq):
    t = nctx + seq
    token_tile = next(c for c in TOKEN_TILES if t % c == 0)
    key_tile = next(c for c in KEY_TILES if t % c == 0 and (t > c or c == KEY_TILES[-1]))
    return SUB_ROWS, token_tile, MXU_WIDTH, SUB_ROWS, key_tile


def kernel(x, c, ctx, c_ctx, w_mod, b_mod, norm_ffn1, ffn1_w13, ffn1_w2, norm_mix, w_in, w_out, qn_a, kn_a, lam_q1, lam_k1, lam_q2, lam_k2, subln_a, qn_b, kn_b, sink_b, conv_w, conv_b, dt_bias, a_log, d_skip, gnorm_c, norm_ffn2, ffn2_w13, ffn2_w2):
    nb, seq, d = x.shape
    nctx = ctx.shape[1]
    t = nctx + seq
    depth = w_mod.shape[0]
    tm, tm_all, fc, tq, tk = _tiling(nctx, seq)
    assert nctx == tm == SUB_ROWS and nctx % (C_STEP_CHUNKS * C_CHUNK) == 0 and seq % GRID_W == 0 and nb < 8

    xs = x
    cvec = jnp.zeros((8, d), F32).at[:nb].set(c).at[nb].set(c_ctx)
    mod = _mod_call(cvec, w_mod, b_mod).reshape(depth, 8, N_MOD, d)

    cosa, sina = _rope_tables(seq, nctx, A_QK, ROPE_LANES // A_QK)
    cosb, sinb = _rope_tables(seq, nctx, B_DIM, ROPE_LANES // B_DIM)
    ga = _group_matrix(A_QCOLS, A_QK)
    gb = _group_matrix(B_QCOLS, B_DIM)
    src = jnp.arange(A_QCOLS)
    dst = (src // (2 * A_QK)) * A_HLANES + src % (2 * A_QK)
    kpl = (dst[:, None] == jnp.arange(A_HEADS * A_HLANES)[None, :]).astype(BF16)
    kone = (jnp.arange(A_HEADS * A_HLANES) % A_HLANES == 2 * A_QK).astype(F32).reshape(1, -1)
    dsrc = jnp.arange(B_KCOLS)
    dcol = jnp.arange(B_QCOLS)
    hpg = B_HEADS // B_KV_HEADS
    dup = jnp.logical_and(dcol[None, :] // (hpg * B_DIM) == dsrc[:, None] // B_DIM,
                          dcol[None, :] % B_DIM == dsrc[:, None] % B_DIM).astype(BF16)
    qi = jnp.arange(B_QBLOCKS * B_BLOCK)[:, None]
    kj = jnp.arange((B_QBLOCKS + 2) * B_BLOCK + nctx)[None, :]
    in_band = jnp.logical_and(kj - qi >= 0, kj - qi <= 2 * B_BLOCK)
    band = jnp.where(jnp.logical_or(in_band, kj >= (B_QBLOCKS + 2) * B_BLOCK), 0.0, NEG_INF).astype(F32)
    in_cols = w_in.shape[2]
    in_pad = (-in_cols) % DT_LANES
    w13_1, w2_1 = ffn1_w13.astype(BF16), ffn1_w2.astype(BF16)
    w13_2, w2_2 = ffn2_w13.astype(BF16), ffn2_w2.astype(BF16)
    w_in_p = jnp.pad(w_in, ((0, 0), (0, 0), (0, in_pad))).astype(BF16)
    w_out_b = w_out.astype(BF16)

    for l in range(depth):
        last = l == depth - 1
        lam_init = 0.8 - 0.6 * math.exp(-0.3 * l)
        hn = jnp.stack([jnp.tile(qn_a[l], A_QCOLS // A_QK), jnp.tile(kn_a[l], A_QCOLS // A_QK),
                        jnp.tile(qn_b[l], B_QCOLS // B_DIM), jnp.tile(kn_b[l], B_QCOLS // B_DIM)]).astype(F32)
        hn = jnp.zeros((8, A_QCOLS), F32).at[:4].set(hn)

        xs = _ffn_call(xs, mod[l], norm_ffn1[l], w13_1, w2_1, l, si=0, fc=fc,
                       tm=tm_all, ctx_rows=nctx, nbatch=nb, ctx_in=(ctx if l == 0 else None))

        qa_t, ka_x, va_t, qb, kb_t, vb2, z, xbc_raw, dt = _inproj_call(
            xs, mod[l], norm_mix[l], w_in_p, l, ga, gb, hn, kpl, kone, dup, dup.T, cosa, sina, cosb, sinb,
            tm=tm_all, ctx_rows=nctx, nbatch=nb)

        lamv = jnp.zeros((8, A_QK), F32).at[:4].set(jnp.stack([lam_q1[l], lam_k1[l], lam_q2[l], lam_k2[l]]))
        oa = _attn_a_call(lamv, subln_a[l].reshape(A_V, 1), qa_t, ka_x, va_t, tq=tq, tk=tk, ctx=nctx,
                          q_tile0=(nctx // tq if last else 0), lam_init=lam_init)

        ob = _attn_b_call(sink_b[l].astype(F32), band, qb, kb_t, vb2, ctx=nctx, blk0=(nctx // B_BLOCK if last else 0))

        xbc_t, bn = _conv_call(xbc_raw, conv_w[l], conv_b[l], tm=tm, ctx=nctx)
        par = (jnp.zeros((8, DT_LANES), F32)
               .at[0, :2 * C_HEADS].set(dt_bias[l].reshape(-1))
               .at[1, :2 * C_HEADS].set(-jnp.exp(a_log[l].astype(F32)).reshape(-1)))
        dsk_col = jnp.repeat(d_skip[l].astype(F32), C_HEAD_DIM).reshape(C_INNER, 1)
        yf, yb = _ssd_call(par, dsk_col, xbc_t, bn, dt, ctx=nctx)

        xs = _ffn_call(xs, mod[l], norm_ffn2[l], w13_2, w2_2, l, si=6, fc=fc, tm=(SUB_ROWS if last else tm_all),
                       ctx_rows=nctx, nbatch=nb, row0=(nctx if last else 0),
                       mix=(oa, ob, yf, yb, z, gnorm_c[l].reshape(1, C_INNER), w_out_b))
    return xs
```
